```python
import math
import jax, jax.numpy as jnp
from jax import lax
import numpy as np

D_MODEL = 1024
BATCH = 8
SEQ = 2048
DEPTH = 4

N_META = 16
A_HEADS = 8
A_HEAD_DIM = 64
KV_LATENT = 128
IDX_HEADS = 8
IDX_DIM = 64
TOPK_MAX = 256
Q_BLOCK = 128
R_HEADS = 4
R_QK_DIM = 64
R_V_DIM = 128
R_CHUNK = 128
ROPE_BASE = 10000.0
N_EXPERTS = 64
N_GROUPS = 8
TOP_GROUPS = 4
TOP_K = 8
D_EXPERT = 256
D_SHARED = 256
ROUTE_SCALE = 2.5
MOE_BLOCK = 128
DN_ALPHA = (2 * DEPTH) ** 0.25
DN_BETA = (8 * DEPTH) ** -0.25
EPS = 1e-6

IN_COLS = (A_HEADS * A_HEAD_DIM, KV_LATENT, IDX_HEADS * IDX_DIM, IDX_DIM, IDX_HEADS,
           R_HEADS * R_QK_DIM, R_HEADS * R_QK_DIM, R_HEADS * R_V_DIM, R_HEADS * R_V_DIM)
D_IN = sum(IN_COLS)
SPLIT_POINTS = tuple(int(v) for v in np.cumsum(IN_COLS)[:-1])
V_COL_START = SPLIT_POINTS[6]
V_COL_END = SPLIT_POINTS[7]
MIX_WIDTH = A_HEADS * A_HEAD_DIM + R_HEADS * R_V_DIM

kernel_name = "hymba_dsa_retnet_deepnorm_moe"


def layer_norm(x, g, b):
    xf = x.astype(jnp.float32)
    mu = xf.mean(-1, keepdims=True)
    var = jnp.square(xf - mu).mean(-1, keepdims=True)
    return ((xf - mu) * lax.rsqrt(var + EPS) * g + b).astype(x.dtype)


def rms_norm(x, g):
    xf = x.astype(jnp.float32)
    return (xf * lax.rsqrt(jnp.square(xf).mean(-1, keepdims=True) + EPS) * g).astype(x.dtype)


def rotary(x, pos):
    half = x.shape[-1] // 2
    inv = ROPE_BASE ** (-jnp.arange(half, dtype=jnp.float32) / half)
    ang = pos.astype(jnp.float32)[:, None] * inv
    cos = jnp.cos(ang)[None, :, None, :]
    sin = jnp.sin(ang)[None, :, None, :]
    xf = x.astype(jnp.float32)
    x1, x2 = xf[..., :half], xf[..., half:]
    return jnp.concatenate([x1 * cos - x2 * sin, x1 * sin + x2 * cos], -1).astype(x.dtype)


def indexed_sparse_attention(q_lat, c_kv, q_idx, k_idx, w_idx, n_sel):
    b, t = c_kv.shape[:2]
    n_blk = -(-t // Q_BLOCK)
    pad = n_blk * Q_BLOCK - t

    def blocks(a):
        a = jnp.pad(a, [(0, 0), (0, pad)] + [(0, 0)] * (a.ndim - 2))
        return jnp.moveaxis(a.reshape(b, n_blk, Q_BLOCK, *a.shape[2:]), 1, 0)

    key_pos = jnp.arange(t)
    scale = A_HEAD_DIM ** -0.5

    def one_block(args):
        ql, qi, wi, t0 = args
        q_pos = t0 + jnp.arange(Q_BLOCK)
        causal = key_pos[None, :] <= q_pos[:, None]
        rel = jax.nn.relu(jnp.einsum('bqhd,bsd->bqhs', qi, k_idx).astype(jnp.float32))
        score = jnp.einsum('bqh,bqhs->bqs', wi.astype(jnp.float32), rel)
        score = jnp.where(causal[None], score, -jnp.inf)
        _, sel = lax.top_k(score, n_sel)
        valid = sel <= q_pos[None, :, None]
        c_sel = jax.vmap(lambda c, i: c[i])(c_kv, sel)
        logits = jnp.einsum('bqhc,bqkc->bqhk', ql, c_sel).astype(jnp.float32) * scale
        logits = jnp.where(valid[:, :, None, :], logits, -jnp.inf)
        p = jax.nn.softmax(logits, axis=-1).astype(c_sel.dtype)
        return jnp.einsum('bqhk,bqkc->bqhc', p, c_sel)

    t0s = jnp.arange(n_blk) * Q_BLOCK
    o = lax.map(one_block, (blocks(q_lat), blocks(q_idx), blocks(w_idx), t0s))
    o = jnp.moveaxis(o, 0, 1).reshape(b, n_blk * Q_BLOCK, *o.shape[3:])
    return o[:, :t]


def chunkwise_retention(q, k, v):
    b, t, h, dk = q.shape
    dv = v.shape[-1]
    n_c = -(-t // R_CHUNK)
    pad = n_c * R_CHUNK - t

    def chunks(a):
        a = jnp.pad(a.astype(jnp.float32), [(0, 0), (pad, 0), (0, 0), (0, 0)])
        return a.reshape(b, n_c, R_CHUNK, h, a.shape[-1]).transpose(1, 0, 3, 2, 4)

    log_g = jnp.log1p(-jnp.exp(jnp.linspace(math.log(1.0 / 32), math.log(1.0 / 512), R_HEADS))).astype(jnp.float32)
    idx = jnp.arange(R_CHUNK, dtype=jnp.float32)
    diff = idx[:, None] - idx[None, :]
    decay_mat = jnp.where(diff[None] >= 0, jnp.exp(diff[None] * log_g[:, None, None]), 0.0)
    q_decay = jnp.exp((idx + 1.0)[None, :] * log_g[:, None])[:, :, None]
    k_decay = jnp.exp((R_CHUNK - 1.0 - idx)[None, :] * log_g[:, None])[:, :, None]
    chunk_decay = jnp.exp(R_CHUNK * log_g)[:, None, None]

    def step(state, qkv):
        qc, kc, vc = qkv
        inner = jnp.einsum('bhnd,bhmd->bhnm', qc, kc) * decay_mat
        out = (jnp.einsum('bhnm,bhmv->bhnv', inner, vc)
               + jnp.einsum('bhnd,bhdv->bhnv', qc, state) * q_decay)
        state = chunk_decay * state + jnp.einsum('bhmd,bhmv->bhdv', kc * k_decay, vc)
        return state, out

    state0 = jnp.zeros((b, h, dk, dv), jnp.float32)
    _, o = lax.scan(step, state0, (chunks(q), chunks(k), chunks(v)))
    o = o.transpose(1, 0, 3, 2, 4).reshape(b, n_c * R_CHUNK, h, dv)
    return o[:, pad:]


def hybrid_mixer(x, w_in, w_uk, w_uv, kv_norm_w, ret_norm_w, w_o, n_sel):
    b, t, _ = x.shape
    proj = x @ w_in
    qa, ckv, qi, ki, wi, qr, kr, vr, gr = jnp.split(proj, SPLIT_POINTS, axis=-1)
    qa = qa.reshape(b, t, A_HEADS, A_HEAD_DIM)
    ckv = rms_norm(ckv, kv_norm_w)
    q_lat = jnp.einsum('bthd,hdc->bthc', qa, w_uk)
    o_lat = indexed_sparse_attention(q_lat, ckv, qi.reshape(b, t, IDX_HEADS, IDX_DIM), ki,
                                     wi * (IDX_HEADS ** -0.5), n_sel)
    out_a = jnp.einsum('bthc,hcd->bthd', o_lat, w_uv).reshape(b, t, A_HEADS * A_HEAD_DIM)
    pos = jnp.arange(t)
    qr = rotary(qr.reshape(b, t, R_HEADS, R_QK_DIM), pos)
    kr = rotary(kr.reshape(b, t, R_HEADS, R_QK_DIM), pos) * (R_QK_DIM ** -0.5)
    o_r = chunkwise_retention(qr, kr, vr.reshape(b, t, R_HEADS, R_V_DIM))
    mu = o_r.mean(-1, keepdims=True)
    var = jnp.square(o_r - mu).mean(-1, keepdims=True)
    o_r = ((o_r - mu) * lax.rsqrt(var + EPS)).reshape(b, t, R_HEADS * R_V_DIM) * ret_norm_w
    out_b = jax.nn.silu(gr) * o_r.astype(x.dtype)
    return jnp.concatenate([out_a, out_b], axis=-1) @ w_o


def routed_experts(xf, top_idx, top_w, w_gate, w_up, w_down):
    n_tok, d = xf.shape
    n_assign = n_tok * TOP_K
    n_blocks = -(-n_assign // MOE_BLOCK) + N_EXPERTS
    n_rows = n_blocks * MOE_BLOCK
    expert_flat = top_idx.reshape(-1)
    token_flat = jnp.arange(n_assign, dtype=jnp.int32) // TOP_K
    gate_flat = top_w.reshape(-1)
    order = jnp.argsort(expert_flat)
    e_sorted = expert_flat[order]
    counts = jnp.bincount(expert_flat, length=N_EXPERTS)
    blocks_per_e = (counts + MOE_BLOCK - 1) // MOE_BLOCK
    blk_end = jnp.cumsum(blocks_per_e)
    blk_start = blk_end - blocks_per_e
    seg_start = jnp.cumsum(counts) - counts
    row = blk_start[e_sorted] * MOE_BLOCK + (jnp.arange(n_assign) - seg_start[e_sorted])
    row_token = jnp.zeros((n_rows,), jnp.int32).at[row].set(token_flat[order])
    row_gate = jnp.zeros((n_rows,), xf.dtype).at[row].set(gate_flat[order].astype(xf.dtype))
    block_expert = jnp.minimum(jnp.searchsorted(blk_end, jnp.arange(n_blocks), side='right'),
                               N_EXPERTS - 1)
    x_rows = xf[row_token].reshape(n_blocks, MOE_BLOCK, d)

    def block_ffn(args):
        xb, e = args
        hdn = jax.nn.silu(xb @ w_gate[e]) * (xb @ w_up[e])
        return hdn @ w_down[e]

    y_rows = lax.map(block_ffn, (x_rows, block_expert)).reshape(n_rows, d)
    return jax.ops.segment_sum(y_rows * row_gate[:, None], row_token, num_segments=n_tok)


def moe_ffn(x, w_router, router_bias, w_gate, w_up, w_down, ws_gate, ws_up, ws_down):
    b, t, d = x.shape
    xf = x.reshape(-1, d)
    scores = jax.nn.sigmoid((xf @ w_router).astype(jnp.float32))
    sel_scores = scores + router_bias.astype(jnp.float32)
    grp = sel_scores.reshape(-1, N_GROUPS, N_EXPERTS // N_GROUPS)
    grp_score = lax.top_k(grp, 2)[0].sum(-1)
    _, top_grp = lax.top_k(grp_score, TOP_GROUPS)
    grp_mask = jnp.any(top_grp[..., None] == jnp.arange(N_GROUPS), axis=-2)
    expert_mask = jnp.repeat(grp_mask, N_EXPERTS // N_GROUPS, axis=-1)
    _, top_idx = lax.top_k(jnp.where(expert_mask, sel_scores, -jnp.inf), TOP_K)
    gates = jnp.take_along_axis(scores, top_idx, axis=-1)
    gates = gates / gates.sum(-1, keepdims=True) * ROUTE_SCALE
    routed = routed_experts(xf, top_idx, gates, w_gate, w_up, w_down)
    shared = (jax.nn.silu(xf @ ws_gate) * (xf @ ws_up)) @ ws_down
    return (routed + shared).reshape(b, t, d)


def setup_inputs(seed: int = 0) -> dict:
    key = jax.random.key(seed)
    ks = jax.random.split(key, 24)
    L = DEPTH

    def nrm(k, shape, scale):
        return jax.random.normal(k, shape, jnp.float32) * scale

    x = nrm(ks[0], (BATCH, SEQ, D_MODEL), 1.0)
    meta_tokens = nrm(ks[1], (N_META, D_MODEL), 1.0)
    w_in = nrm(ks[2], (L, D_MODEL, D_IN), D_MODEL ** -0.5)
    w_in = w_in.at[:, :, V_COL_START:V_COL_END].multiply(DN_BETA)
    w_uk = nrm(ks[3], (L, A_HEADS, A_HEAD_DIM, KV_LATENT), KV_LATENT ** -0.5)
    w_uv = nrm(ks[4], (L, A_HEADS, KV_LATENT, A_HEAD_DIM), KV_LATENT ** -0.5 * DN_BETA)
    kv_norm_w = 1.0 + nrm(ks[5], (L, KV_LATENT), 0.02)
    ret_norm_w = 1.0 + nrm(ks[6], (L, R_HEADS * R_V_DIM), 0.02)
    w_o = nrm(ks[7], (L, MIX_WIDTH, D_MODEL), MIX_WIDTH ** -0.5 * DN_BETA)
    ln1_g = 1.0 + nrm(ks[8], (L, D_MODEL), 0.02)
    ln1_b = nrm(ks[9], (L, D_MODEL), 0.02)
    w_router = nrm(ks[10], (L, D_MODEL, N_EXPERTS), D_MODEL ** -0.5)
    router_bias = nrm(ks[11], (L, N_EXPERTS), 0.01)
    w_gate = nrm(ks[12], (L, N_EXPERTS, D_MODEL, D_EXPERT), D_MODEL ** -0.5)
    w_up = nrm(ks[13], (L, N_EXPERTS, D_MODEL, D_EXPERT), D_MODEL ** -0.5)
    w_down = nrm(ks[14], (L, N_EXPERTS, D_EXPERT, D_MODEL), D_EXPERT ** -0.5 * DN_BETA)
    ws_gate = nrm(ks[15], (L, D_MODEL, D_SHARED), D_MODEL ** -0.5)
    ws_up = nrm(ks[16], (L, D_MODEL, D_SHARED), D_MODEL ** -0.5)
    ws_down = nrm(ks[17], (L, D_SHARED, D_MODEL), D_SHARED ** -0.5 * DN_BETA)
    ln2_g = 1.0 + nrm(ks[18], (L, D_MODEL), 0.02)
    ln2_b = nrm(ks[19], (L, D_MODEL), 0.02)
    return {"x": x, "meta_tokens": meta_tokens, "w_in": w_in, "w_uk": w_uk, "w_uv": w_uv,
            "kv_norm_w": kv_norm_w, "ret_norm_w": ret_norm_w, "w_o": w_o,
            "ln1_g": ln1_g, "ln1_b": ln1_b, "w_router": w_router, "router_bias": router_bias,
            "w_gate": w_gate, "w_up": w_up, "w_down": w_down,
            "ws_gate": ws_gate, "ws_up": ws_up, "ws_down": ws_down,
            "ln2_g": ln2_g, "ln2_b": ln2_b}


def reference(x, meta_tokens, w_in, w_uk, w_uv, kv_norm_w, ret_norm_w, w_o, ln1_g, ln1_b,
              w_router, router_bias, w_gate, w_up, w_down, ws_gate, ws_up, ws_down,
              ln2_g, ln2_b):
    b, s, d = x.shape
    n_sel = min(TOPK_MAX, s // 4)
    meta = jnp.broadcast_to(meta_tokens.astype(x.dtype)[None], (b, N_META, d))
    h = jnp.concatenate([meta, x], axis=1)
    for l in range(DEPTH):
        mix = hybrid_mixer(h, w_in[l], w_uk[l], w_uv[l], kv_norm_w[l], ret_norm_w[l], w_o[l], n_sel)
        h = layer_norm(DN_ALPHA * h + mix, ln1_g[l], ln1_b[l])
        ffn = moe_ffn(h, w_router[l], router_bias[l], w_gate[l], w_up[l], w_down[l],
                      ws_gate[l], ws_up[l], ws_down[l])
        h = layer_norm(DN_ALPHA * h + ffn, ln2_g[l], ln2_b[l])
    return h[:, N_META:]
```

```python
import functools
import math

import numpy as np
import jax
import jax.numpy as jnp
from jax import lax
from jax.experimental import pallas as pl
from jax.experimental.pallas import tpu as pltpu

D_MODEL = 1024
N_META = 16
A_HEADS = 8
A_HEAD_DIM = 64
KV_LATENT = 128
IDX_HEADS = 8
IDX_DIM = 64
TOPK_MAX = 256
R_HEADS = 4
R_QK_DIM = 64
R_V_DIM = 128
ROPE_BASE = 10000.0
N_EXPERTS = 64
N_GROUPS = 8
GROUP_SIZE = N_EXPERTS // N_GROUPS
TOP_GROUPS = 4
TOP_K = 8
D_EXPERT = 256
D_SHARED = 256
ROUTE_SCALE = 2.5
DEPTH = 4
DN_ALPHA = (2 * DEPTH) ** 0.25
EPS = 1e-6

IN_COLS = (A_HEADS * A_HEAD_DIM, KV_LATENT, IDX_HEADS * IDX_DIM, IDX_DIM, IDX_HEADS,
           R_HEADS * R_QK_DIM, R_HEADS * R_QK_DIM, R_HEADS * R_V_DIM, R_HEADS * R_V_DIM)
IN_OFFS = tuple(int(v) for v in np.cumsum((0,) + IN_COLS))

LANES = 128
SUBLANES = 8
Q_BLOCK = 128
R_CHUNK = 128
ROW_BLOCK = 256
SLABS = D_MODEL // LANES
VMEM_LIMIT = 56 * 1024 * 1024

C_QLAT = 0
C_CKV = C_QLAT + A_HEADS * KV_LATENT
C_QI = C_CKV + KV_LATENT
C_KI = C_QI + IDX_HEADS * IDX_DIM
C_WI = C_KI + LANES
C_QR = C_WI + LANES
C_KR = C_QR + R_HEADS * R_QK_DIM
C_VR = C_KR + R_HEADS * R_QK_DIM
C_GR = C_VR + R_HEADS * R_V_DIM
C_END = C_GR + R_HEADS * R_V_DIM

INT_MIN = -2 ** 31
NEG_BIG = -1e30

_NT = (((1,), (1,)), ((), ()))
_TN = (((0,), (0,)), ((), ()))


def _cparams(sem):
    return pltpu.CompilerParams(dimension_semantics=sem, vmem_limit_bytes=VMEM_LIMIT)


def _fold_kernel(a_ref, b_ref, o_ref, *, scale):
    o_ref[0, 0] = (jnp.dot(a_ref[0, 0], b_ref[0, 0], preferred_element_type=jnp.float32,
                           precision=lax.Precision.HIGHEST) * scale).astype(o_ref.dtype)


def _fold(a, b, scale):
    L, H, M, K = a.shape
    N = b.shape[-1]
    return pl.pallas_call(
        functools.partial(_fold_kernel, scale=scale),
        out_shape=jax.ShapeDtypeStruct((L, H, M, N), jnp.bfloat16),
        grid=(L, H),
        in_specs=[pl.BlockSpec((1, 1, M, K), lambda l, h: (l, h, 0, 0)),
                  pl.BlockSpec((1, 1, K, N), lambda l, h: (l, h, 0, 0))],
        out_specs=pl.BlockSpec((1, 1, M, N), lambda l, h: (l, h, 0, 0)),
        compiler_params=_cparams(("parallel", "parallel")),
        name="weight_fold",
    )(a, b)


def _inproj_kernel(x_ref, w_ref, kvg_ref, cos_ref, sin_ref,
                   qlat_ref, ckv_ref, qi_ref, ki_ref, wi_ref, qr_ref, kr_ref, vr_ref, sg_ref):
    x = x_ref[...]

    def proj(lo, hi):
        return jnp.dot(x, w_ref[:, lo:hi], preferred_element_type=jnp.float32)

    r = proj(C_QLAT, C_CKV)
    for h in range(A_HEADS):
        qlat_ref[h] = r[:, h * KV_LATENT:(h + 1) * KV_LATENT].astype(qlat_ref.dtype)

    r = proj(C_CKV, C_QI)
    r = r * lax.rsqrt(jnp.mean(r * r, axis=-1, keepdims=True) + EPS) * kvg_ref[...]
    ckv_ref[...] = r.astype(ckv_ref.dtype)

    r = proj(C_QI, C_KI)
    for h in range(IDX_HEADS):
        qi_ref[h] = r[:, h * IDX_DIM:(h + 1) * IDX_DIM].astype(qi_ref.dtype)

    r = proj(C_KI, C_WI)
    ki_ref[...] = r[:, :IDX_DIM].astype(ki_ref.dtype)
    r = proj(C_WI, C_QR)
    wi_ref[...] = r[:, :IDX_HEADS] * (IDX_HEADS ** -0.5)

    cos = cos_ref[...]
    sin = sin_ref[...]
    half = R_HEADS * R_QK_DIM // 2

    def rot(r, out_ref, scale):
        x1, x2 = r[:, :half], r[:, half:]
        out_ref[:, :half] = ((x1 * cos - x2 * sin) * scale).astype(out_ref.dtype)
        out_ref[:, half:] = ((x1 * sin + x2 * cos) * scale).astype(out_ref.dtype)

    rot(proj(C_QR, C_KR), qr_ref, 1.0)
    rot(proj(C_KR, C_VR), kr_ref, R_QK_DIM ** -0.5)
    vr_ref[...] = proj(C_VR, C_GR).astype(vr_ref.dtype)
    g = proj(C_GR, C_END)
    sg_ref[...] = (g / (1.0 + jnp.exp(-g))).astype(sg_ref.dtype)


def _inproj(xb, w_cat, kv_g, cos_t, sin_t, t_pad):
    n = xb.shape[0]
    tm = t_pad // 4
    per_b = t_pad // tm
    bf = jnp.bfloat16
    row = lambda i: (i, 0)
    head = lambda i: (0, i, 0)
    const = lambda i: (0, 0)
    pos = lambda i: (i % per_b, 0)
    return pl.pallas_call(
        _inproj_kernel,
        out_shape=(jax.ShapeDtypeStruct((A_HEADS, n, KV_LATENT), bf),
                   jax.ShapeDtypeStruct((n, KV_LATENT), bf),
                   jax.ShapeDtypeStruct((IDX_HEADS, n, IDX_DIM), bf),
                   jax.ShapeDtypeStruct((n, IDX_DIM), bf),
                   jax.ShapeDtypeStruct((n, IDX_HEADS), jnp.float32),
                   jax.ShapeDtypeStruct((n, R_HEADS * R_QK_DIM), bf),
                   jax.ShapeDtypeStruct((n, R_HEADS * R_QK_DIM), bf),
                   jax.ShapeDtypeStruct((n, R_HEADS * R_V_DIM), bf),
                   jax.ShapeDtypeStruct((n, R_HEADS * R_V_DIM), bf)),
        grid=(n // tm,),
        in_specs=[pl.BlockSpec((tm, D_MODEL), row),
                  pl.BlockSpec((D_MODEL, C_END), const),
                  pl.BlockSpec((1, KV_LATENT), const),
                  pl.BlockSpec((tm, LANES), pos),
                  pl.BlockSpec((tm, LANES), pos)],
        out_specs=(pl.BlockSpec((A_HEADS, tm, KV_LATENT), head),
                   pl.BlockSpec((tm, KV_LATENT), row),
                   pl.BlockSpec((IDX_HEADS, tm, IDX_DIM), head),
                   pl.BlockSpec((tm, IDX_DIM), row),
                   pl.BlockSpec((tm, IDX_HEADS), row),
                   pl.BlockSpec((tm, R_HEADS * R_QK_DIM), row),
                   pl.BlockSpec((tm, R_HEADS * R_QK_DIM), row),
                   pl.BlockSpec((tm, R_HEADS * R_V_DIM), row),
                   pl.BlockSpec((tm, R_HEADS * R_V_DIM), row)),
        compiler_params=_cparams(("parallel",)),
        name="in_proj",
    )(xb, w_cat, kv_g, cos_t, sin_t)


def _attn_kernel(qlat_ref, qi_ref, wi_ref, ckv_ref, ki_ref, o_ref, *, n_sel, t_pad):
    i = pl.program_id(1)
    ki = ki_ref[...]
    ckv = ckv_ref[...]
    w = wi_ref[...]
    shape = (Q_BLOCK, t_pad)

    score = jnp.zeros(shape, jnp.float32)
    for h in range(IDX_HEADS):
        qk = lax.dot_general(qi_ref[h], ki, _NT, preferred_element_type=jnp.float32)
        score = score + w[:, h:h + 1] * jnp.maximum(qk, 0.0)

    q_pos = i * Q_BLOCK + lax.broadcasted_iota(jnp.int32, shape, 0)
    k_pos = lax.broadcasted_iota(jnp.int32, shape, 1)
    causal = k_pos <= q_pos

    bits = lax.bitcast_convert_type(score, jnp.int32)
    key = bits ^ ((bits >> 31) & jnp.int32(0x7FFFFFFF))
    key = jnp.where(score == 0.0, 0, key)
    key = jnp.where(causal, key, INT_MIN)

    def count(mask):
        return jnp.sum(jnp.where(mask, 1.0, 0.0), axis=1, keepdims=True)

    def thr_step(it, t):
        cand = t ^ lax.shift_left(jnp.int32(1), 31 - it)
        return jnp.where(count(key >= cand) >= n_sel, cand, t)

    t = lax.fori_loop(0, 32, thr_step, jnp.full((Q_BLOCK, 1), INT_MIN, jnp.int32))

    gt = key > t
    eq = key == t
    need = n_sel - count(gt)
    idx_bits = max(1, (t_pad - 1).bit_length())

    def tie_step(it, c):
        cand = c | lax.shift_left(jnp.int32(1), idx_bits - 1 - it)
        return jnp.where(count(eq & (k_pos < cand)) <= need, cand, c)

    c = lax.fori_loop(0, idx_bits, tie_step, jnp.zeros((Q_BLOCK, 1), jnp.int32))
    sel = (gt | (eq & (k_pos < c))) & causal

    for h in range(A_HEADS):
        lg = lax.dot_general(qlat_ref[h], ckv, _NT, preferred_element_type=jnp.float32)
        lg = jnp.where(sel, lg, NEG_BIG)
        m = jnp.max(lg, axis=1, keepdims=True)
        p = jnp.exp(lg - m)
        l = jnp.sum(p, axis=1, keepdims=True)
        o = jnp.dot(p.astype(ckv.dtype), ckv, preferred_element_type=jnp.float32)
        o_ref[h] = (o / l).astype(o_ref.dtype)


def _attention(qlat, qi, wi, ckv, ki, batch, t_pad, n_sel):
    n = ckv.shape[0]
    nq = t_pad // Q_BLOCK
    qmap = lambda b, i: (0, b * nq + i, 0)
    return pl.pallas_call(
        functools.partial(_attn_kernel, n_sel=n_sel, t_pad=t_pad),
        out_shape=jax.ShapeDtypeStruct((A_HEADS, n, KV_LATENT), jnp.bfloat16),
        grid=(batch, nq),
        in_specs=[pl.BlockSpec((A_HEADS, Q_BLOCK, KV_LATENT), qmap),
                  pl.BlockSpec((IDX_HEADS, Q_BLOCK, IDX_DIM), qmap),
                  pl.BlockSpec((Q_BLOCK, IDX_HEADS), lambda b, i: (b * nq + i, 0)),
                  pl.BlockSpec((t_pad, KV_LATENT), lambda b, i: (b, 0)),
                  pl.BlockSpec((t_pad, IDX_DIM), lambda b, i: (b, 0))],
        out_specs=pl.BlockSpec((A_HEADS, Q_BLOCK, KV_LATENT), qmap),
        compiler_params=_cparams(("parallel", "parallel")),
        name="sparse_attention",
    )(qlat, qi, wi, ckv, ki)


def _retention_tables():
    log_g = np.log1p(-np.exp(np.linspace(math.log(1.0 / 32), math.log(1.0 / 512), R_HEADS))).astype(np.float32)
    idx = np.arange(R_CHUNK, dtype=np.float32)
    diff = idx[:, None] - idx[None, :]
    decay = np.where(diff[None] >= 0, np.exp(diff[None] * log_g[:, None, None]), 0.0).astype(np.float32)
    q_decay = np.exp((idx + 1.0)[None, :] * log_g[:, None])[:, :, None].astype(np.float32)
    k_decay = np.exp((R_CHUNK - 1.0 - idx)[None, :] * log_g[:, None])[:, :, None].astype(np.float32)
    chunk_decay = np.exp(R_CHUNK * log_g).astype(np.float32)
    return decay, q_decay, k_decay, chunk_decay


def _retention_kernel(q_ref, k_ref, v_ref, sg_ref, rw_ref, dec_ref, qd_ref, kd_ref, o_ref, state_ref,
                      *, t_pad, chunk_decay):
    state_ref[...] = jnp.zeros_like(state_ref)
    qk_w = R_HEADS * R_QK_DIM
    lane = lax.broadcasted_iota(jnp.int32, (1, qk_w), 1)
    half_w = R_QK_DIM // 2
    head_mask = [((lane % (qk_w // 2)) // half_w) == h for h in range(R_HEADS)]

    def chunk(c, carry):
        r0 = pl.multiple_of(c * R_CHUNK, R_CHUNK)
        rows = pl.ds(r0, R_CHUNK)
        qc = q_ref[rows, :]
        kc = k_ref[rows, :]
        for h in range(R_HEADS):
            cols = slice(h * R_V_DIM, (h + 1) * R_V_DIM)
            qm = jnp.where(head_mask[h], qc, jnp.zeros_like(qc))
            km = jnp.where(head_mask[h], kc, jnp.zeros_like(kc))
            vh = v_ref[rows, cols]
            inner = lax.dot_general(qm, kc, _NT, preferred_element_type=jnp.float32) * dec_ref[h]
            st = state_ref[h]
            out = (jnp.dot(inner.astype(vh.dtype), vh, preferred_element_type=jnp.float32)
                   + jnp.dot(qm, st.astype(qm.dtype), preferred_element_type=jnp.float32) * qd_ref[h])
            kdec = (km.astype(jnp.float32) * kd_ref[h]).astype(km.dtype)
            state_ref[h] = chunk_decay[h] * st + lax.dot_general(kdec, vh, _TN,
                                                                  preferred_element_type=jnp.float32)
            mu = jnp.mean(out, axis=-1, keepdims=True)
            d = out - mu
            var = jnp.mean(d * d, axis=-1, keepdims=True)
            normed = d * lax.rsqrt(var + EPS) * rw_ref[:, cols]
            o_ref[rows, cols] = (sg_ref[rows, cols].astype(jnp.float32) * normed).astype(o_ref.dtype)
        return carry

    lax.fori_loop(0, t_pad // R_CHUNK, chunk, 0)


def _retention(qr, kr, vr, sg, ret_w, batch, t_pad):
    n = qr.shape[0]
    decay, q_decay, k_decay, chunk_decay = _retention_tables()
    qk_w = R_HEADS * R_QK_DIM
    v_w = R_HEADS * R_V_DIM
    per_b = lambda b: (b, 0)
    c2 = lambda b: (0, 0)
    c3 = lambda b: (0, 0, 0)
    return pl.pallas_call(
        functools.partial(_retention_kernel, t_pad=t_pad, chunk_decay=[float(v) for v in chunk_decay]),
        out_shape=jax.ShapeDtypeStruct((n, v_w), jnp.bfloat16),
        grid=(batch,),
        in_specs=[pl.BlockSpec((t_pad, qk_w), per_b),
                  pl.BlockSpec((t_pad, qk_w), per_b),
                  pl.BlockSpec((t_pad, v_w), per_b),
                  pl.BlockSpec((t_pad, v_w), per_b),
                  pl.BlockSpec((1, v_w), c2),
                  pl.BlockSpec((R_HEADS, R_CHUNK, R_CHUNK), c3),
                  pl.BlockSpec((R_HEADS, R_CHUNK, 1), c3),
                  pl.BlockSpec((R_HEADS, R_CHUNK, 1), c3)],
        out_specs=pl.BlockSpec((t_pad, v_w), per_b),
        scratch_shapes=[pltpu.VMEM((R_HEADS, qk_w, R_V_DIM), jnp.float32)],
        compiler_params=_cparams(("parallel",)),
        name="retention",
    )(qr, kr, vr, sg, ret_w, jnp.asarray(decay), jnp.asarray(q_decay), jnp.asarray(k_decay))


def _layer_norm(y, g, b):
    mu = jnp.mean(y, axis=-1, keepdims=True)
    d = y - mu
    var = jnp.mean(d * d, axis=-1, keepdims=True)
    return d * lax.rsqrt(var + EPS) * g + b


def _mix_router_kernel(olat_ref, ob_ref, h_ref, wuvo_ref, wob_ref, g_ref, b_ref, wrt_ref, rb_ref,
                       h1_ref, h1b_ref, x3_ref, idx_ref, gate_ref, *, tm):
    mix = jnp.dot(ob_ref[...], wob_ref[...], preferred_element_type=jnp.float32)
    for h in range(A_HEADS):
        mix = mix + jnp.dot(olat_ref[h], wuvo_ref[h], preferred_element_type=jnp.float32)
    h1 = _layer_norm(DN_ALPHA * h_ref[...] + mix, g_ref[...], b_ref[...])
    h1_ref[...] = h1
    h1b_ref[...] = h1.astype(h1b_ref.dtype)
    for s in range(SLABS):
        x3_ref[pl.ds(s, tm, stride=SLABS), :] = h1[:, s * LANES:(s + 1) * LANES]

    logits = lax.dot_general(wrt_ref[...], h1, _NT, preferred_element_type=jnp.float32,
                             precision=lax.Precision.HIGHEST)
    scores = 1.0 / (1.0 + jnp.exp(-logits))
    sel = scores + rb_ref[...]
    neg = -jnp.inf
    iota_g = lax.broadcasted_iota(jnp.int32, (GROUP_SIZE, tm), 0)
    iota_n = lax.broadcasted_iota(jnp.int32, (N_GROUPS, tm), 0)

    def first_argmax(v, iota, big):
        m = jnp.max(v, axis=0, keepdims=True)
        return m, jnp.min(jnp.where(v == m, iota, big), axis=0, keepdims=True)

    grp_score = jnp.zeros((N_GROUPS, tm), jnp.float32)
    for g in range(N_GROUPS):
        blk = sel[g * GROUP_SIZE:(g + 1) * GROUP_SIZE]
        m1, i1 = first_argmax(blk, iota_g, GROUP_SIZE)
        m2 = jnp.max(jnp.where(iota_g == i1, neg, blk), axis=0, keepdims=True)
        grp_score = jnp.where(iota_n == g, m1 + m2, grp_score)

    grp_on = jnp.zeros((N_GROUPS, tm), jnp.float32)
    work = grp_score
    for _ in range(TOP_GROUPS):
        _, gi = first_argmax(work, iota_n, N_GROUPS)
        hit = iota_n == gi
        grp_on = jnp.where(hit, 1.0, grp_on)
        work = jnp.where(hit, neg, work)

    masked = jnp.concatenate(
        [jnp.where(grp_on[g:g + 1] > 0.0, sel[g * GROUP_SIZE:(g + 1) * GROUP_SIZE], neg)
         for g in range(N_GROUPS)], axis=0)
    iota_e = lax.broadcasted_iota(jnp.int32, (N_EXPERTS, tm), 0)
    iota_k = lax.broadcasted_iota(jnp.int32, (TOP_K, tm), 0)
    top_idx = jnp.zeros((TOP_K, tm), jnp.int32)
    top_gate = jnp.zeros((TOP_K, tm), jnp.float32)
    for k in range(TOP_K):
        _, ei = first_argmax(masked, iota_e, N_EXPERTS)
        hit = iota_e == ei
        gk = jnp.sum(jnp.where(hit, scores, 0.0), axis=0, keepdims=True)
        masked = jnp.where(hit, neg, masked)
        top_idx = jnp.where(iota_k == k, ei, top_idx)
        top_gate = jnp.where(iota_k == k, gk, top_gate)
    idx_ref[...] = top_idx
    gate_ref[...] = top_gate / jnp.sum(top_gate, axis=0, keepdims=True) * ROUTE_SCALE


def _mix_router(olat, ob, h, wuvo, wob, ln_g, ln_b, wrt, rbias, tm):
    n = h.shape[0]
    row = lambda i: (i, 0)
    c2 = lambda i: (0, 0)
    c3 = lambda i: (0, 0, 0)
    col = lambda i: (0, i)
    v_w = R_HEADS * R_V_DIM
    return pl.pallas_call(
        functools.partial(_mix_router_kernel, tm=tm),
        out_shape=(jax.ShapeDtypeStruct((n, D_MODEL), jnp.float32),
                   jax.ShapeDtypeStruct((n, D_MODEL), jnp.bfloat16),
                   jax.ShapeDtypeStruct((n * SLABS, LANES), jnp.float32),
                   jax.ShapeDtypeStruct((TOP_K, n), jnp.int32),
                   jax.ShapeDtypeStruct((TOP_K, n), jnp.float32)),
        grid=(n // tm,),
        in_specs=[pl.BlockSpec((A_HEADS, tm, KV_LATENT), lambda i: (0, i, 0)),
                  pl.BlockSpec((tm, v_w), row),
                  pl.BlockSpec((tm, D_MODEL), row),
                  pl.BlockSpec((A_HEADS, KV_LATENT, D_MODEL), c3),
                  pl.BlockSpec((v_w, D_MODEL), c2),
                  pl.BlockSpec((1, D_MODEL), c2),
                  pl.BlockSpec((1, D_MODEL), c2),
                  pl.BlockSpec((N_EXPERTS, D_MODEL), c2),
                  pl.BlockSpec((N_EXPERTS, 1), c2)],
        out_specs=(pl.BlockSpec((tm, D_MODEL), row),
                   pl.BlockSpec((tm, D_MODEL), row),
                   pl.BlockSpec((tm * SLABS, LANES), row),
                   pl.BlockSpec((TOP_K, tm), col),
                   pl.BlockSpec((TOP_K, tm), col)),
        compiler_params=_cparams(("parallel",)),
        name="mix_ln_router",
    )(olat, ob, h, wuvo, wob, ln_g, ln_b, wrt, rbias)


def _row_copy(src, dst, sem):
    return pltpu.make_async_copy(src, dst, sem)


def _dispatch_kernel(pad_lo_ref, pad_hi_ref, pos_ref, x3_ref, rows_ref, zero_ref, sem, *, td):
    base = pl.program_id(0) * td

    @pl.when(pl.program_id(0) == 0)
    def _():
        zero_ref[...] = jnp.zeros_like(zero_ref)

        def fill(e, carry):
            def one(r, c):
                _row_copy(zero_ref, rows_ref.at[r], sem).start()
                return c
            return lax.fori_loop(pad_lo_ref[e], pad_hi_ref[e], one, carry)

        def drain(e, carry):
            def one(r, c):
                _row_copy(zero_ref, rows_ref.at[r], sem).wait()
                return c
            return lax.fori_loop(pad_lo_ref[e], pad_hi_ref[e], one, carry)

        lax.fori_loop(0, N_EXPERTS, fill, 0)
        lax.fori_loop(0, N_EXPERTS, drain, 0)

    def send(j, carry):
        for k in range(TOP_K):
            _row_copy(x3_ref.at[base + j], rows_ref.at[pos_ref[k, j]], sem).start()
        return carry

    def drain_tok(j, carry):
        for k in range(TOP_K):
            _row_copy(x3_ref.at[base + j], rows_ref.at[pos_ref[k, j]], sem).wait()
        return carry

    lax.fori_loop(0, td, send, 0)
    lax.fori_loop(0, td, drain_tok, 0)


def _dispatch(pad_lo, pad_hi, pos, x3, n_rows, td):
    n = pos.shape[1]
    return pl.pallas_call(
        functools.partial(_dispatch_kernel, td=td),
        out_shape=jax.ShapeDtypeStruct((n_rows, SLABS, LANES), jnp.float32),
        grid_spec=pltpu.PrefetchScalarGridSpec(
            num_scalar_prefetch=2,
            grid=(n // td,),
            in_specs=[pl.BlockSpec((TOP_K, td), lambda i, lo, hi: (0, i), memory_space=pltpu.SMEM),
                      pl.BlockSpec(memory_space=pl.ANY)],
            out_specs=pl.BlockSpec(memory_space=pl.ANY),
            scratch_shapes=[pltpu.VMEM((SLABS, LANES), jnp.float32), pltpu.SemaphoreType.DMA(())]),
        compiler_params=pltpu.CompilerParams(dimension_semantics=("arbitrary",), has_side_effects=True),
        name="moe_dispatch",
    )(pad_lo, pad_hi, pos, x3)


def _expert_kernel(be_ref, nu_ref, x_ref, wg_ref, wu_ref, wd_ref, y_ref):
    @pl.when(pl.program_id(0) < nu_ref[0])
    def _():
        g = jnp.zeros((ROW_BLOCK, D_EXPERT), jnp.float32)
        u = jnp.zeros((ROW_BLOCK, D_EXPERT), jnp.float32)
        for s in range(SLABS):
            xs = x_ref[pl.ds(s, ROW_BLOCK, stride=SLABS), :].astype(jnp.bfloat16)
            ks = slice(s * LANES, (s + 1) * LANES)
            g = g + jnp.dot(xs, wg_ref[0, ks, :], preferred_element_type=jnp.float32)
            u = u + jnp.dot(xs, wu_ref[0, ks, :], preferred_element_type=jnp.float32)
        hdn = (g / (1.0 + jnp.exp(-g)) * u).astype(jnp.bfloat16)
        y = jnp.dot(hdn, wd_ref[0], preferred_element_type=jnp.float32)
        for s in range(SLABS):
            y_ref[pl.ds(s, ROW_BLOCK, stride=SLABS), :] = y[:, s * LANES:(s + 1) * LANES]


def _experts(block_expert, n_used, x_rows2, wg, wu, wd):
    n_blocks = block_expert.shape[0]
    blk = lambda i, be, nu: (jnp.minimum(i, nu[0] - 1), 0)
    wsel = lambda i, be, nu: (be[jnp.minimum(i, nu[0] - 1)], 0, 0)
    return pl.pallas_call(
        _expert_kernel,
        out_shape=jax.ShapeDtypeStruct(x_rows2.shape, jnp.float32),
        grid_spec=pltpu.PrefetchScalarGridSpec(
            num_scalar_prefetch=2,
            grid=(n_blocks,),
            in_specs=[pl.BlockSpec((ROW_BLOCK * SLABS, LANES), blk),
                      pl.BlockSpec((1, D_MODEL, D_EXPERT), wsel),
                      pl.BlockSpec((1, D_MODEL, D_EXPERT), wsel),
                      pl.BlockSpec((1, D_EXPERT, D_MODEL), wsel)],
            out_specs=pl.BlockSpec((ROW_BLOCK * SLABS, LANES), blk)),
        compiler_params=_cparams(("arbitrary",)),
        name="moe_experts",
    )(block_expert, n_used, x_rows2, wg, wu, wd)


def _combine_kernel(pos_ref, gate_ref, y3_ref, h1_ref, h1b_ref, wsg_ref, wsu_ref, wsd_ref, g_ref, b_ref,
                    h2_ref, h2b_ref, buf_ref, comb_ref, sem, *, tc):
    def fetch(j, carry):
        for k in range(TOP_K):
            _row_copy(y3_ref.at[pos_ref[k, j]], buf_ref.at[k, j], sem).start()
        return carry

    def drain(j, carry):
        for k in range(TOP_K):
            _row_copy(y3_ref.at[pos_ref[k, j]], buf_ref.at[k, j], sem).wait()
        return carry

    lax.fori_loop(0, tc, fetch, 0)

    xb = h1b_ref[...]
    gs = jnp.dot(xb, wsg_ref[...], preferred_element_type=jnp.float32)
    us = jnp.dot(xb, wsu_ref[...], preferred_element_type=jnp.float32)
    hs = (gs / (1.0 + jnp.exp(-gs)) * us).astype(jnp.bfloat16)
    shared = jnp.dot(hs, wsd_ref[...], preferred_element_type=jnp.float32)

    lax.fori_loop(0, tc, drain, 0)

    def weigh(j, carry):
        acc = gate_ref[0, j] * buf_ref[0, j]
        for k in range(1, TOP_K):
            acc = acc + gate_ref[k, j] * buf_ref[k, j]
        comb_ref[pl.ds(pl.multiple_of(j * SLABS, SLABS), SLABS), :] = acc
        return carry

    lax.fori_loop(0, tc, weigh, 0)
    routed = jnp.concatenate([comb_ref[pl.ds(s, tc, stride=SLABS), :] for s in range(SLABS)], axis=1)
    h2 = _layer_norm(DN_ALPHA * h1_ref[...] + (routed + shared), g_ref[...], b_ref[...])
    h2_ref[...] = h2
    h2b_ref[...] = h2.astype(h2b_ref.dtype)


def _combine(pos, gates, y3, h1, h1b, wsg, wsu, wsd, ln_g, ln_b, tc):
    n = h1.shape[0]
    row = lambda i: (i, 0)
    c2 = lambda i: (0, 0)
    col = lambda i: (0, i)
    return pl.pallas_call(
        functools.partial(_combine_kernel, tc=tc),
        out_shape=(jax.ShapeDtypeStruct((n, D_MODEL), jnp.float32),
                   jax.ShapeDtypeStruct((n, D_MODEL), jnp.bfloat16)),
        grid=(n // tc,),
        in_specs=[pl.BlockSpec((TOP_K, tc), col, memory_space=pltpu.SMEM),
                  pl.BlockSpec((TOP_K, tc), col, memory_space=pltpu.SMEM),
                  pl.BlockSpec(memory_space=pl.ANY),
                  pl.BlockSpec((tc, D_MODEL), row),
                  pl.BlockSpec((tc, D_MODEL), row),
                  pl.BlockSpec((D_MODEL, D_SHARED), c2),
                  pl.BlockSpec((D_MODEL, D_SHARED), c2),
                  pl.BlockSpec((D_SHARED, D_MODEL), c2),
                  pl.BlockSpec((1, D_MODEL), c2),
                  pl.BlockSpec((1, D_MODEL), c2)],
        out_specs=(pl.BlockSpec((tc, D_MODEL), row),
                   pl.BlockSpec((tc, D_MODEL), row)),
        scratch_shapes=[pltpu.VMEM((TOP_K, tc, SLABS, LANES), jnp.float32),
                        pltpu.VMEM((tc * SLABS, LANES), jnp.float32),
                        pltpu.SemaphoreType.DMA(())],
        compiler_params=_cparams(("arbitrary",)),
        name="moe_combine",
    )(pos, gates, y3, h1, h1b, wsg, wsu, wsd, ln_g, ln_b)


def _routing_plan(top_idx, n_blocks):
    n = top_idx.shape[1]
    onehot = jnp.sum((top_idx[:, :, None] == jnp.arange(N_EXPERTS, dtype=jnp.int32)).astype(jnp.int32), axis=0)
    cum = jnp.cumsum(onehot, axis=0)
    counts = cum[-1]
    rank = cum - onehot
    blocks_per_e = (counts + ROW_BLOCK - 1) // ROW_BLOCK
    blk_end = jnp.cumsum(blocks_per_e)
    row_base = (blk_end - blocks_per_e) * ROW_BLOCK
    pos = row_base[top_idx] + jnp.take_along_axis(rank, top_idx.T, axis=1).T
    block_expert = jnp.minimum(jnp.searchsorted(blk_end, jnp.arange(n_blocks, dtype=jnp.int32), side='right'),
                               N_EXPERTS - 1).astype(jnp.int32)
    n_used = blk_end[-1:].astype(jnp.int32)
    pad_lo = (row_base + counts).astype(jnp.int32)
    pad_hi = (blk_end * ROW_BLOCK).astype(jnp.int32)
    return pos.astype(jnp.int32), block_expert, n_used, pad_lo, pad_hi


def _prepare_weights(w_in, w_uk, w_uv, w_o):
    L = w_in.shape[0]
    o = IN_OFFS
    bf = jnp.bfloat16
    w_qa = w_in[:, :, o[0]:o[1]].reshape(L, D_MODEL, A_HEADS, A_HEAD_DIM).transpose(0, 2, 1, 3)
    w_qlat = _fold(w_qa, w_uk, A_HEAD_DIM ** -0.5)
    w_qlat = w_qlat.transpose(0, 2, 1, 3).reshape(L, D_MODEL, A_HEADS * KV_LATENT)
    w_oa = w_o[:, :A_HEADS * A_HEAD_DIM].reshape(L, A_HEADS, A_HEAD_DIM, D_MODEL)
    w_uvo = _fold(w_uv, w_oa, 1.0)
    w_ob = w_o[:, A_HEADS * A_HEAD_DIM:].astype(bf)

    def pad_cols(w, width):
        return jnp.pad(w, ((0, 0), (0, 0), (0, width - w.shape[-1])))

    half = R_QK_DIM // 2
    perm = np.concatenate([np.arange(half) + R_QK_DIM * h for h in range(R_HEADS)]
                          + [np.arange(half) + R_QK_DIM * h + half for h in range(R_HEADS)])
    w_cat = jnp.concatenate([
        w_qlat,
        w_in[:, :, o[1]:o[2]].astype(bf),
        w_in[:, :, o[2]:o[3]].astype(bf),
        pad_cols(w_in[:, :, o[3]:o[4]], LANES).astype(bf),
        pad_cols(w_in[:, :, o[4]:o[5]], LANES).astype(bf),
        w_in[:, :, o[5]:o[6]][:, :, perm].astype(bf),
        w_in[:, :, o[6]:o[7]][:, :, perm].astype(bf),
        w_in[:, :, o[7]:o[8]].astype(bf),
        w_in[:, :, o[8]:o[9]].astype(bf)], axis=-1)
    return w_cat, w_uvo, w_ob


def _rotary_tables(t_pad):
    half = R_QK_DIM // 2
    inv = ROPE_BASE ** (-jnp.arange(half, dtype=jnp.float32) / half)
    ang = jnp.arange(t_pad, dtype=jnp.float32)[:, None] * inv
    return jnp.tile(jnp.cos(ang), (1, R_HEADS)), jnp.tile(jnp.sin(ang), (1, R_HEADS))


def _pick_tile(n, prefer):
    for t in prefer:
        if n % t == 0:
            return t
    raise ValueError(f"no tile for {n}")


def kernel(x, meta_tokens, w_in, w_uk, w_uv, kv_norm_w, ret_norm_w, w_o, ln1_g, ln1_b, w_router, router_bias,
           w_gate, w_up, w_down, ws_gate, ws_up, ws_down, ln2_g, ln2_b):
    b, s, d = x.shape
    assert d == D_MODEL
    L = w_in.shape[0]
    t = s + N_META
    n_sel = min(TOPK_MAX, s // 4)
    t_pad = -(-t // LANES) * LANES
    n = b * t_pad
    bf = jnp.bfloat16

    meta = jnp.broadcast_to(meta_tokens.astype(x.dtype)[None], (b, N_META, d))
    h = jnp.concatenate([meta, x, jnp.zeros((b, t_pad - t, d), x.dtype)], axis=1).reshape(n, d)
    hb = h.astype(bf)

    w_cat, w_uvo, w_ob = _prepare_weights(w_in, w_uk, w_uv, w_o)
    cos_t, sin_t = _rotary_tables(t_pad)
    wg, wu, wd = w_gate.astype(bf), w_up.astype(bf), w_down.astype(bf)
    wsg, wsu, wsd = ws_gate.astype(bf), ws_up.astype(bf), ws_down.astype(bf)
    w_rt = jnp.swapaxes(w_router, 1, 2)

    tm = _pick_tile(n, (512, 256, 128))
    tc = _pick_tile(n, (256, 128))
    n_blocks = -(-(n * TOP_K) // ROW_BLOCK) + N_EXPERTS
    n_rows = n_blocks * ROW_BLOCK

    for l in range(L):
        qlat, ckv, qi, ki, wi, qr, kr, vr, sg = _inproj(hb, w_cat[l], kv_norm_w[l][None], cos_t, sin_t, t_pad)
        olat = _attention(qlat, qi, wi, ckv, ki, b, t_pad, n_sel)
        ob = _retention(qr, kr, vr, sg, ret_norm_w[l][None], b, t_pad)
        h1, h1b, x3, top_idx, gates = _mix_router(olat, ob, h, w_uvo[l], w_ob[l], ln1_g[l][None], ln1_b[l][None],
                                                  w_rt[l], router_bias[l][:, None], tm)
        pos, block_expert, n_used, pad_lo, pad_hi = _routing_plan(top_idx, n_blocks)
        x_rows = _dispatch(pad_lo, pad_hi, pos, x3.reshape(n, SLABS, LANES), n_rows, tm)
        y_rows = _experts(block_expert, n_used, x_rows.reshape(n_rows * SLABS, LANES), wg[l], wu[l], wd[l])
        h, hb = _combine(pos, gates, y_rows.reshape(n_rows, SLABS, LANES), h1, h1b, wsg[l], wsu[l], wsd[l],
                         ln2_g[l][None], ln2_b[l][None], tc)
    return h.reshape(b, t_pad, d)[:, N_META:t]
```

```python
import functools
import math

import numpy as np
import jax
import jax.numpy as jnp
from jax import lax
from jax.experimental import pallas as pl
from jax.experimental.pallas import tpu as pltpu

D_MODEL = 1024
N_META = 16
A_HEADS = 8
A_HEAD_DIM = 64
KV_LATENT = 128
IDX_HEADS = 8
IDX_DIM = 64
TOPK_MAX = 256
R_HEADS = 4
R_QK_DIM = 64
R_V_DIM = 128
ROPE_BASE = 10000.0
N_EXPERTS = 64
N_GROUPS = 8
GROUP_SIZE = N_EXPERTS // N_GROUPS
TOP_GROUPS = 4
TOP_K = 8
D_EXPERT = 256
D_SHARED = 256
ROUTE_SCALE = 2.5
DEPTH = 4
DN_ALPHA = (2 * DEPTH) ** 0.25
EPS = 1e-6

IN_COLS = (A_HEADS * A_HEAD_DIM, KV_LATENT, IDX_HEADS * IDX_DIM, IDX_DIM, IDX_HEADS,
           R_HEADS * R_QK_DIM, R_HEADS * R_QK_DIM, R_HEADS * R_V_DIM, R_HEADS * R_V_DIM)
IN_OFFS = tuple(int(v) for v in np.cumsum((0,) + IN_COLS))

LANES = 128
SUBLANES = 8
Q_BLOCK = 128
R_CHUNK = 128
ROW_BLOCK = 256
SLABS = D_MODEL // LANES
VMEM_LIMIT = 56 * 1024 * 1024

C_QLAT = 0
C_CKV = C_QLAT + A_HEADS * KV_LATENT
C_QI = C_CKV + KV_LATENT
C_KI = C_QI + IDX_HEADS * IDX_DIM
C_WI = C_KI + LANES
C_QR = C_WI + LANES
C_KR = C_QR + R_HEADS * R_QK_DIM
C_VR = C_KR + R_HEADS * R_QK_DIM
C_GR = C_VR + R_HEADS * R_V_DIM
C_END = C_GR + R_HEADS * R_V_DIM

INT_MIN = -2 ** 31
NEG_BIG = -1e30

_NT = (((1,), (1,)), ((), ()))
_TN = (((0,), (0,)), ((), ()))


def _cparams(sem):
    return pltpu.CompilerParams(dimension_semantics=sem, vmem_limit_bytes=VMEM_LIMIT)


def _fold_kernel(a_ref, b_ref, o_ref, *, scale):
    o_ref[0, 0] = (jnp.dot(a_ref[0, 0], b_ref[0, 0], preferred_element_type=jnp.float32,
                           precision=lax.Precision.HIGHEST) * scale).astype(o_ref.dtype)


def _fold(a, b, scale):
    L, H, M, K = a.shape
    N = b.shape[-1]
    return pl.pallas_call(
        functools.partial(_fold_kernel, scale=scale),
        out_shape=jax.ShapeDtypeStruct((L, H, M, N), jnp.bfloat16),
        grid=(L, H),
        in_specs=[pl.BlockSpec((1, 1, M, K), lambda l, h: (l, h, 0, 0)),
                  pl.BlockSpec((1, 1, K, N), lambda l, h: (l, h, 0, 0))],
        out_specs=pl.BlockSpec((1, 1, M, N), lambda l, h: (l, h, 0, 0)),
        compiler_params=_cparams(("parallel", "parallel")),
        name="weight_fold",
    )(a, b)


def _inproj_kernel(x_ref, w_ref, kvg_ref, cos_ref, sin_ref,
                   qlat_ref, ckv_ref, qi_ref, ki_ref, wi_ref, qr_ref, kr_ref, vr_ref, sg_ref):
    x = x_ref[...]

    def proj(lo, hi):
        return jnp.dot(x, w_ref[:, lo:hi], preferred_element_type=jnp.float32)

    r = proj(C_QLAT, C_CKV)
    for h in range(A_HEADS):
        qlat_ref[h] = r[:, h * KV_LATENT:(h + 1) * KV_LATENT].astype(qlat_ref.dtype)

    r = proj(C_CKV, C_QI)
    r = r * lax.rsqrt(jnp.mean(r * r, axis=-1, keepdims=True) + EPS) * kvg_ref[...]
    ckv_ref[...] = r.astype(ckv_ref.dtype)

    r = proj(C_QI, C_KI)
    for h in range(IDX_HEADS):
        qi_ref[h] = r[:, h * IDX_DIM:(h + 1) * IDX_DIM].astype(qi_ref.dtype)

    r = proj(C_KI, C_WI)
    ki_ref[...] = r[:, :IDX_DIM].astype(ki_ref.dtype)
    r = proj(C_WI, C_QR)
    wi_ref[...] = r[:, :IDX_HEADS] * (IDX_HEADS ** -0.5)

    cos = cos_ref[...]
    sin = sin_ref[...]
    half = R_HEADS * R_QK_DIM // 2

    def rot(r, out_ref, scale):
        x1, x2 = r[:, :half], r[:, half:]
        out_ref[:, :half] = ((x1 * cos - x2 * sin) * scale).astype(out_ref.dtype)
        out_ref[:, half:] = ((x1 * sin + x2 * cos) * scale).astype(out_ref.dtype)

    rot(proj(C_QR, C_KR), qr_ref, 1.0)
    rot(proj(C_KR, C_VR), kr_ref, R_QK_DIM ** -0.5)
    vr_ref[...] = proj(C_VR, C_GR).astype(vr_ref.dtype)
    g = proj(C_GR, C_END)
    sg_ref[...] = (g / (1.0 + jnp.exp(-g))).astype(sg_ref.dtype)


def _inproj(xb, w_cat, kv_g, cos_t, sin_t, t_pad):
    n = xb.shape[0]
    tm = t_pad // 4
    per_b = t_pad // tm
    bf = jnp.bfloat16
    row = lambda i: (i, 0)
    head = lambda i: (0, i, 0)
    const = lambda i: (0, 0)
    pos = lambda i: (i % per_b, 0)
    return pl.pallas_call(
        _inproj_kernel,
        out_shape=(jax.ShapeDtypeStruct((A_HEADS, n, KV_LATENT), bf),
                   jax.ShapeDtypeStruct((n, KV_LATENT), bf),
                   jax.ShapeDtypeStruct((IDX_HEADS, n, IDX_DIM), bf),
                   jax.ShapeDtypeStruct((n, IDX_DIM), bf),
                   jax.ShapeDtypeStruct((n, IDX_HEADS), jnp.float32),
                   jax.ShapeDtypeStruct((n, R_HEADS * R_QK_DIM), bf),
                   jax.ShapeDtypeStruct((n, R_HEADS * R_QK_DIM), bf),
                   jax.ShapeDtypeStruct((n, R_HEADS * R_V_DIM), bf),
                   jax.ShapeDtypeStruct((n, R_HEADS * R_V_DIM), bf)),
        grid=(n // tm,),
        in_specs=[pl.BlockSpec((tm, D_MODEL), row),
                  pl.BlockSpec((D_MODEL, C_END), const),
                  pl.BlockSpec((1, KV_LATENT), const),
                  pl.BlockSpec((tm, LANES), pos),
                  pl.BlockSpec((tm, LANES), pos)],
        out_specs=(pl.BlockSpec((A_HEADS, tm, KV_LATENT), head),
                   pl.BlockSpec((tm, KV_LATENT), row),
                   pl.BlockSpec((IDX_HEADS, tm, IDX_DIM), head),
                   pl.BlockSpec((tm, IDX_DIM), row),
                   pl.BlockSpec((tm, IDX_HEADS), row),
                   pl.BlockSpec((tm, R_HEADS * R_QK_DIM), row),
                   pl.BlockSpec((tm, R_HEADS * R_QK_DIM), row),
                   pl.BlockSpec((tm, R_HEADS * R_V_DIM), row),
                   pl.BlockSpec((tm, R_HEADS * R_V_DIM), row)),
        compiler_params=_cparams(("parallel",)),
        name="in_proj",
    )(xb, w_cat, kv_g, cos_t, sin_t)


def _attn_kernel(qlat_ref, qi_ref, wi_ref, ckv_ref, ki_ref, o_ref, key_scr, lg_scr, m_scr, l_scr, acc_scr,
                 *, n_sel, t_pad):
    i = pl.program_id(1)
    nk = i + 1
    tile = (Q_BLOCK, Q_BLOCK)
    rows_all = A_HEADS * Q_BLOCK
    q_all = qlat_ref[...].reshape(rows_all, KV_LATENT)
    qi_all = qi_ref[...].reshape(IDX_HEADS * Q_BLOCK, IDX_DIM)
    w = wi_ref[...]
    w_b = [jnp.broadcast_to(w[:, h:h + 1], tile) for h in range(IDX_HEADS)]
    row = lax.broadcasted_iota(jnp.int32, tile, 0)
    col = lax.broadcasted_iota(jnp.int32, tile, 1)

    def key_rows(kt):
        return pl.ds(pl.multiple_of(kt * Q_BLOCK, Q_BLOCK), Q_BLOCK)

    def index_tile(kt, carry):
        qk = lax.dot_general(qi_all, ki_ref[key_rows(kt), :], _NT, preferred_element_type=jnp.float32)
        score = w_b[0] * jnp.maximum(qk[:Q_BLOCK], 0.0)
        for h in range(1, IDX_HEADS):
            score = score + w_b[h] * jnp.maximum(qk[h * Q_BLOCK:(h + 1) * Q_BLOCK], 0.0)
        bits = lax.bitcast_convert_type(score, jnp.int32)
        key = bits ^ ((bits >> 31) & jnp.int32(0x7FFFFFFF))
        key = jnp.where(score == 0.0, 0, key)
        causal = (kt * Q_BLOCK + col) <= (i * Q_BLOCK + row)
        key_scr[kt] = jnp.where(causal, key, INT_MIN)
        return carry

    lax.fori_loop(0, nk, index_tile, 0)

    def count(pred):
        def body(kt, acc):
            return acc + jnp.where(pred(key_scr[kt], kt), 1.0, 0.0)
        acc = lax.fori_loop(0, nk, body, jnp.zeros(tile, jnp.float32))
        return jnp.sum(acc, axis=1, keepdims=True)

    def thr_step(it, t):
        cand = t ^ lax.shift_left(jnp.int32(1), 31 - it)
        return jnp.where(count(lambda k, kt: k >= cand) >= n_sel, cand, t)

    t = lax.fori_loop(0, 32, thr_step, jnp.full((Q_BLOCK, 1), INT_MIN, jnp.int32))

    need = n_sel - count(lambda k, kt: k > t)
    n_eq = count(lambda k, kt: k == t)
    idx_bits = max(1, (t_pad - 1).bit_length())
    surplus = jnp.max(jnp.where((t > INT_MIN) & (n_eq > need), 1.0, 0.0)) > 0.0

    def tie_search():
        def tie_step(it, c):
            cand = c | lax.shift_left(jnp.int32(1), idx_bits - 1 - it)
            below = count(lambda k, kt: (k == t) & ((kt * Q_BLOCK + col) < cand))
            return jnp.where(below <= need, cand, c)
        return lax.fori_loop(0, idx_bits, tie_step, jnp.zeros((Q_BLOCK, 1), jnp.int32))

    c = lax.cond(surplus, tie_search, lambda: jnp.full((Q_BLOCK, 1), 1 << idx_bits, jnp.int32))

    m_scr[...] = jnp.full(m_scr.shape, NEG_BIG, jnp.float32)

    def logit_tile(kt, carry):
        key = key_scr[kt]
        sel = ((key > t) | ((key == t) & ((kt * Q_BLOCK + col) < c))) & (key != INT_MIN)
        lg = lax.dot_general(q_all, ckv_ref[key_rows(kt), :], _NT, preferred_element_type=jnp.float32)
        lg = jnp.where(sel[None], lg.reshape(A_HEADS, Q_BLOCK, Q_BLOCK), NEG_BIG).reshape(rows_all, Q_BLOCK)
        lg_scr[kt] = lg
        m_scr[...] = jnp.maximum(m_scr[...], lg)
        return carry

    lax.fori_loop(0, nk, logit_tile, 0)
    m_b = jnp.broadcast_to(jnp.max(m_scr[...], axis=1, keepdims=True), (rows_all, Q_BLOCK))
    l_scr[...] = jnp.zeros_like(l_scr)
    acc_scr[...] = jnp.zeros_like(acc_scr)

    def pv_tile(kt, carry):
        p = jnp.exp(lg_scr[kt] - m_b)
        l_scr[...] += p
        ckv_t = ckv_ref[key_rows(kt), :]
        acc_scr[...] += jnp.dot(p.astype(ckv_t.dtype), ckv_t, preferred_element_type=jnp.float32)
        return carry

    lax.fori_loop(0, nk, pv_tile, 0)
    o = acc_scr[...] / jnp.sum(l_scr[...], axis=1, keepdims=True)
    o_ref[...] = o.reshape(A_HEADS, Q_BLOCK, KV_LATENT).astype(o_ref.dtype)


def _attention(qlat, qi, wi, ckv, ki, batch, t_pad, n_sel):
    n = ckv.shape[0]
    nq = t_pad // Q_BLOCK
    qmap = lambda b, i: (0, b * nq + i, 0)
    rows_all = A_HEADS * Q_BLOCK
    scratch = [pltpu.VMEM((nq, Q_BLOCK, Q_BLOCK), jnp.int32),
               pltpu.VMEM((nq, rows_all, Q_BLOCK), jnp.float32),
               pltpu.VMEM((rows_all, Q_BLOCK), jnp.float32),
               pltpu.VMEM((rows_all, Q_BLOCK), jnp.float32),
               pltpu.VMEM((rows_all, KV_LATENT), jnp.float32)]
    return pl.pallas_call(
        functools.partial(_attn_kernel, n_sel=n_sel, t_pad=t_pad),
        out_shape=jax.ShapeDtypeStruct((A_HEADS, n, KV_LATENT), jnp.bfloat16),
        grid=(batch, nq),
        in_specs=[pl.BlockSpec((A_HEADS, Q_BLOCK, KV_LATENT), qmap),
                  pl.BlockSpec((IDX_HEADS, Q_BLOCK, IDX_DIM), qmap),
                  pl.BlockSpec((Q_BLOCK, IDX_HEADS), lambda b, i: (b * nq + i, 0)),
                  pl.BlockSpec((t_pad, KV_LATENT), lambda b, i: (b, 0)),
                  pl.BlockSpec((t_pad, IDX_DIM), lambda b, i: (b, 0))],
        out_specs=pl.BlockSpec((A_HEADS, Q_BLOCK, KV_LATENT), qmap),
        scratch_shapes=scratch,
        compiler_params=_cparams(("parallel", "parallel")),
        name="sparse_attention",
    )(qlat, qi, wi, ckv, ki)


def _retention_tables():
    log_g = np.log1p(-np.exp(np.linspace(math.log(1.0 / 32), math.log(1.0 / 512), R_HEADS))).astype(np.float32)
    idx = np.arange(R_CHUNK, dtype=np.float32)
    diff = idx[:, None] - idx[None, :]
    decay = np.where(diff[None] >= 0, np.exp(diff[None] * log_g[:, None, None]), 0.0).astype(np.float32)
    q_decay = np.exp((idx + 1.0)[None, :] * log_g[:, None])[:, :, None].astype(np.float32)
    k_decay = np.exp((R_CHUNK - 1.0 - idx)[None, :] * log_g[:, None])[:, :, None].astype(np.float32)
    chunk_decay = np.exp(R_CHUNK * log_g).astype(np.float32)
    return decay, q_decay, k_decay, chunk_decay


def _retention_kernel(q_ref, k_ref, v_ref, sg_ref, rw_ref, dec_ref, qd_ref, kd_ref, o_ref, state_ref,
                      *, t_pad, chunk_decay):
    state_ref[...] = jnp.zeros_like(state_ref)
    qk_w = R_HEADS * R_QK_DIM
    lane = lax.broadcasted_iota(jnp.int32, (1, qk_w), 1)
    half_w = R_QK_DIM // 2
    head_mask = [((lane % (qk_w // 2)) // half_w) == h for h in range(R_HEADS)]

    def chunk(c, carry):
        r0 = pl.multiple_of(c * R_CHUNK, R_CHUNK)
        rows = pl.ds(r0, R_CHUNK)
        qc = q_ref[rows, :]
        kc = k_ref[rows, :]
        for h in range(R_HEADS):
            cols = slice(h * R_V_DIM, (h + 1) * R_V_DIM)
            qm = jnp.where(head_mask[h], qc, jnp.zeros_like(qc))
            km = jnp.where(head_mask[h], kc, jnp.zeros_like(kc))
            vh = v_ref[rows, cols]
            inner = lax.dot_general(qm, kc, _NT, preferred_element_type=jnp.float32) * dec_ref[h]
            st = state_ref[h]
            out = (jnp.dot(inner.astype(vh.dtype), vh, preferred_element_type=jnp.float32)
                   + jnp.dot(qm, st.astype(qm.dtype), preferred_element_type=jnp.float32) * qd_ref[h])
            kdec = (km.astype(jnp.float32) * kd_ref[h]).astype(km.dtype)
            state_ref[h] = chunk_decay[h] * st + lax.dot_general(kdec, vh, _TN,
                                                                  preferred_element_type=jnp.float32)
            mu = jnp.mean(out, axis=-1, keepdims=True)
            d = out - mu
            var = jnp.mean(d * d, axis=-1, keepdims=True)
            normed = d * lax.rsqrt(var + EPS) * rw_ref[:, cols]
            o_ref[rows, cols] = (sg_ref[rows, cols].astype(jnp.float32) * normed).astype(o_ref.dtype)
        return carry

    lax.fori_loop(0, t_pad // R_CHUNK, chunk, 0)


def _retention(qr, kr, vr, sg, ret_w, batch, t_pad):
    n = qr.shape[0]
    decay, q_decay, k_decay, chunk_decay = _retention_tables()
    qk_w = R_HEADS * R_QK_DIM
    v_w = R_HEADS * R_V_DIM
    per_b = lambda b: (b, 0)
    c2 = lambda b: (0, 0)
    c3 = lambda b: (0, 0, 0)
    return pl.pallas_call(
        functools.partial(_retention_kernel, t_pad=t_pad, chunk_decay=[float(v) for v in chunk_decay]),
        out_shape=jax.ShapeDtypeStruct((n, v_w), jnp.bfloat16),
        grid=(batch,),
        in_specs=[pl.BlockSpec((t_pad, qk_w), per_b),
                  pl.BlockSpec((t_pad, qk_w), per_b),
                  pl.BlockSpec((t_pad, v_w), per_b),
                  pl.BlockSpec((t_pad, v_w), per_b),
                  pl.BlockSpec((1, v_w), c2),
                  pl.BlockSpec((R_HEADS, R_CHUNK, R_CHUNK), c3),
                  pl.BlockSpec((R_HEADS, R_CHUNK, 1), c3),
                  pl.BlockSpec((R_HEADS, R_CHUNK, 1), c3)],
        out_specs=pl.BlockSpec((t_pad, v_w), per_b),
        scratch_shapes=[pltpu.VMEM((R_HEADS, qk_w, R_V_DIM), jnp.float32)],
        compiler_params=_cparams(("parallel",)),
        name="retention",
    )(qr, kr, vr, sg, ret_w, jnp.asarray(decay), jnp.asarray(q_decay), jnp.asarray(k_decay))


def _layer_norm(y, g, b):
    mu = jnp.mean(y, axis=-1, keepdims=True)
    d = y - mu
    var = jnp.mean(d * d, axis=-1, keepdims=True)
    return d * lax.rsqrt(var + EPS) * g + b


def _mix_router_kernel(olat_ref, ob_ref, h_ref, wuvo_ref, wob_ref, g_ref, b_ref, wrt_ref, rb_ref, tri_ref,
                       h1_ref, h1b_ref, x3_ref, idx_ref, gate_ref, rank_ref, cnt_ref, run_ref, *, tm):
    mix = jnp.dot(ob_ref[...], wob_ref[...], preferred_element_type=jnp.float32)
    for h in range(A_HEADS):
        mix = mix + jnp.dot(olat_ref[h], wuvo_ref[h], preferred_element_type=jnp.float32)
    h1 = _layer_norm(DN_ALPHA * h_ref[...] + mix, g_ref[...], b_ref[...])
    h1_ref[...] = h1
    h1b_ref[...] = h1.astype(h1b_ref.dtype)
    for s in range(SLABS):
        x3_ref[pl.ds(s, tm, stride=SLABS), :] = h1[:, s * LANES:(s + 1) * LANES]

    logits = lax.dot_general(wrt_ref[...], h1, _NT, preferred_element_type=jnp.float32,
                             precision=lax.Precision.HIGHEST)
    scores = 1.0 / (1.0 + jnp.exp(-logits))
    sel = scores + rb_ref[...]
    neg = -jnp.inf
    iota_g = lax.broadcasted_iota(jnp.int32, (GROUP_SIZE, tm), 0)
    iota_n = lax.broadcasted_iota(jnp.int32, (N_GROUPS, tm), 0)

    def first_argmax(v, iota, big):
        m = jnp.max(v, axis=0, keepdims=True)
        return m, jnp.min(jnp.where(v == m, iota, big), axis=0, keepdims=True)

    grp_score = jnp.zeros((N_GROUPS, tm), jnp.float32)
    for g in range(N_GROUPS):
        blk = sel[g * GROUP_SIZE:(g + 1) * GROUP_SIZE]
        m1, i1 = first_argmax(blk, iota_g, GROUP_SIZE)
        m2 = jnp.max(jnp.where(iota_g == i1, neg, blk), axis=0, keepdims=True)
        grp_score = jnp.where(iota_n == g, m1 + m2, grp_score)

    grp_on = jnp.zeros((N_GROUPS, tm), jnp.float32)
    work = grp_score
    for _ in range(TOP_GROUPS):
        _, gi = first_argmax(work, iota_n, N_GROUPS)
        hit = iota_n == gi
        grp_on = jnp.where(hit, 1.0, grp_on)
        work = jnp.where(hit, neg, work)

    masked = jnp.concatenate(
        [jnp.where(grp_on[g:g + 1] > 0.0, sel[g * GROUP_SIZE:(g + 1) * GROUP_SIZE], neg)
         for g in range(N_GROUPS)], axis=0)
    iota_e = lax.broadcasted_iota(jnp.int32, (N_EXPERTS, tm), 0)
    iota_k = lax.broadcasted_iota(jnp.int32, (TOP_K, tm), 0)
    top_idx = jnp.zeros((TOP_K, tm), jnp.int32)
    top_gate = jnp.zeros((TOP_K, tm), jnp.float32)
    hits = []
    for k in range(TOP_K):
        _, ei = first_argmax(masked, iota_e, N_EXPERTS)
        hit = iota_e == ei
        hits.append(hit)
        gk = jnp.sum(jnp.where(hit, scores, 0.0), axis=0, keepdims=True)
        masked = jnp.where(hit, neg, masked)
        top_idx = jnp.where(iota_k == k, ei, top_idx)
        top_gate = jnp.where(iota_k == k, gk, top_gate)
    idx_ref[...] = top_idx
    gate_ref[...] = top_gate / jnp.sum(top_gate, axis=0, keepdims=True) * ROUTE_SCALE

    @pl.when(pl.program_id(0) == 0)
    def _():
        run_ref[...] = jnp.zeros_like(run_ref)

    onehot = jnp.zeros((N_EXPERTS, tm), jnp.float32)
    for hit in hits:
        onehot = jnp.where(hit, 1.0, onehot)
    before = jnp.dot(onehot.astype(jnp.bfloat16), tri_ref[...], preferred_element_type=jnp.float32)
    before = before + run_ref[...]
    rank = jnp.zeros((TOP_K, tm), jnp.float32)
    for k in range(TOP_K):
        rank = jnp.where(iota_k == k, jnp.sum(jnp.where(hits[k], before, 0.0), axis=0, keepdims=True), rank)
    rank_ref[...] = rank.astype(jnp.int32)
    run_ref[...] += jnp.sum(onehot, axis=1, keepdims=True)
    cnt_ref[...] = run_ref[...].astype(jnp.int32)


def _mix_router(olat, ob, h, wuvo, wob, ln_g, ln_b, wrt, rbias, tm):
    n = h.shape[0]
    row = lambda i: (i, 0)
    c2 = lambda i: (0, 0)
    c3 = lambda i: (0, 0, 0)
    col = lambda i: (0, i)
    v_w = R_HEADS * R_V_DIM
    tri = jnp.triu(jnp.ones((tm, tm), jnp.bfloat16), k=1)
    return pl.pallas_call(
        functools.partial(_mix_router_kernel, tm=tm),
        out_shape=(jax.ShapeDtypeStruct((n, D_MODEL), jnp.float32),
                   jax.ShapeDtypeStruct((n, D_MODEL), jnp.bfloat16),
                   jax.ShapeDtypeStruct((n * SLABS, LANES), jnp.float32),
                   jax.ShapeDtypeStruct((TOP_K, n), jnp.int32),
                   jax.ShapeDtypeStruct((TOP_K, n), jnp.float32),
                   jax.ShapeDtypeStruct((TOP_K, n), jnp.int32),
                   jax.ShapeDtypeStruct((N_EXPERTS, 1), jnp.int32)),
        grid=(n // tm,),
        in_specs=[pl.BlockSpec((A_HEADS, tm, KV_LATENT), lambda i: (0, i, 0)),
                  pl.BlockSpec((tm, v_w), row),
                  pl.BlockSpec((tm, D_MODEL), row),
                  pl.BlockSpec((A_HEADS, KV_LATENT, D_MODEL), c3),
                  pl.BlockSpec((v_w, D_MODEL), c2),
                  pl.BlockSpec((1, D_MODEL), c2),
                  pl.BlockSpec((1, D_MODEL), c2),
                  pl.BlockSpec((N_EXPERTS, D_MODEL), c2),
                  pl.BlockSpec((N_EXPERTS, 1), c2),
                  pl.BlockSpec((tm, tm), c2)],
        out_specs=(pl.BlockSpec((tm, D_MODEL), row),
                   pl.BlockSpec((tm, D_MODEL), row),
                   pl.BlockSpec((tm * SLABS, LANES), row),
                   pl.BlockSpec((TOP_K, tm), col),
                   pl.BlockSpec((TOP_K, tm), col),
                   pl.BlockSpec((TOP_K, tm), col),
                   pl.BlockSpec((N_EXPERTS, 1), c2)),
        scratch_shapes=[pltpu.VMEM((N_EXPERTS, 1), jnp.float32)],
        compiler_params=_cparams(("arbitrary",)),
        name="mix_ln_router",
    )(olat, ob, h, wuvo, wob, ln_g, ln_b, wrt, rbias, tri)


def _row_copy(src, dst, sem):
    return pltpu.make_async_copy(src, dst, sem)


def _dispatch_kernel(base_ref, pad_lo_ref, pad_hi_ref, idx_ref, rank_ref, x3_ref, rows_ref, zero_ref, sem, *, td):
    @pl.when(pl.program_id(0) == 0)
    def _():
        zero_ref[...] = jnp.zeros_like(zero_ref)

        def fill(e, carry):
            def one(r, c):
                _row_copy(zero_ref, rows_ref.at[r], sem).start()
                return c
            return lax.fori_loop(pad_lo_ref[e], pad_hi_ref[e], one, carry)

        def drain(e, carry):
            def one(r, c):
                _row_copy(zero_ref, rows_ref.at[r], sem).wait()
                return c
            return lax.fori_loop(pad_lo_ref[e], pad_hi_ref[e], one, carry)

        lax.fori_loop(0, N_EXPERTS, fill, 0)
        lax.fori_loop(0, N_EXPERTS, drain, 0)

    def row_of(k, j):
        return base_ref[idx_ref[k, j]] + rank_ref[k, j]

    def send(j, carry):
        for k in range(TOP_K):
            _row_copy(x3_ref.at[j], rows_ref.at[row_of(k, j)], sem).start()
        return carry

    def drain_tok(j, carry):
        for k in range(TOP_K):
            _row_copy(x3_ref.at[j], rows_ref.at[row_of(k, j)], sem).wait()
        return carry

    lax.fori_loop(0, td, send, 0)
    lax.fori_loop(0, td, drain_tok, 0)


def _dispatch(row_base, pad_lo, pad_hi, top_idx, rank, x3, n_rows, td):
    n = top_idx.shape[1]
    col = lambda i, *_: (0, i)
    return pl.pallas_call(
        functools.partial(_dispatch_kernel, td=td),
        out_shape=jax.ShapeDtypeStruct((n_rows, SLABS, LANES), jnp.float32),
        grid_spec=pltpu.PrefetchScalarGridSpec(
            num_scalar_prefetch=3,
            grid=(n // td,),
            in_specs=[pl.BlockSpec((TOP_K, td), col, memory_space=pltpu.SMEM),
                      pl.BlockSpec((TOP_K, td), col, memory_space=pltpu.SMEM),
                      pl.BlockSpec((td, SLABS, LANES), lambda i, *_: (i, 0, 0))],
            out_specs=pl.BlockSpec(memory_space=pl.ANY),
            scratch_shapes=[pltpu.VMEM((SLABS, LANES), jnp.float32), pltpu.SemaphoreType.DMA(())]),
        compiler_params=pltpu.CompilerParams(dimension_semantics=("arbitrary",), has_side_effects=True),
        name="moe_dispatch",
    )(row_base, pad_lo, pad_hi, top_idx, rank, x3)


def _expert_kernel(be_ref, nu_ref, x_ref, wg_ref, wu_ref, wd_ref, y_ref):
    @pl.when(pl.program_id(0) < nu_ref[0])
    def _():
        g = jnp.zeros((ROW_BLOCK, D_EXPERT), jnp.float32)
        u = jnp.zeros((ROW_BLOCK, D_EXPERT), jnp.float32)
        for s in range(SLABS):
            xs = x_ref[pl.ds(s, ROW_BLOCK, stride=SLABS), :].astype(jnp.bfloat16)
            ks = slice(s * LANES, (s + 1) * LANES)
            g = g + jnp.dot(xs, wg_ref[0, ks, :], preferred_element_type=jnp.float32)
            u = u + jnp.dot(xs, wu_ref[0, ks, :], preferred_element_type=jnp.float32)
        hdn = (g / (1.0 + jnp.exp(-g)) * u).astype(jnp.bfloat16)
        y = jnp.dot(hdn, wd_ref[0], preferred_element_type=jnp.float32)
        for s in range(SLABS):
            y_ref[pl.ds(s, ROW_BLOCK, stride=SLABS), :] = y[:, s * LANES:(s + 1) * LANES]


def _experts(block_expert, n_used, x_rows2, wg, wu, wd):
    n_blocks = block_expert.shape[0]
    blk = lambda i, be, nu: (jnp.minimum(i, nu[0] - 1), 0)
    wsel = lambda i, be, nu: (be[jnp.minimum(i, nu[0] - 1)], 0, 0)
    return pl.pallas_call(
        _expert_kernel,
        out_shape=jax.ShapeDtypeStruct(x_rows2.shape, jnp.float32),
        grid_spec=pltpu.PrefetchScalarGridSpec(
            num_scalar_prefetch=2,
            grid=(n_blocks,),
            in_specs=[pl.BlockSpec((ROW_BLOCK * SLABS, LANES), blk),
                      pl.BlockSpec((1, D_MODEL, D_EXPERT), wsel),
                      pl.BlockSpec((1, D_MODEL, D_EXPERT), wsel),
                      pl.BlockSpec((1, D_EXPERT, D_MODEL), wsel)],
            out_specs=pl.BlockSpec((ROW_BLOCK * SLABS, LANES), blk)),
        compiler_params=_cparams(("arbitrary",)),
        name="moe_experts",
    )(block_expert, n_used, x_rows2, wg, wu, wd)


def _combine_kernel(base_ref, idx_ref, rank_ref, gate_ref, y3_ref, h1_ref, h1b_ref, wsg_ref, wsu_ref, wsd_ref,
                    g_ref, b_ref, h2_ref, h2b_ref, buf_ref, comb_ref, sem, *, tc):
    def row_of(k, j):
        return base_ref[idx_ref[k, j]] + rank_ref[k, j]

    def fetch(j, carry):
        for k in range(TOP_K):
            _row_copy(y3_ref.at[row_of(k, j)], buf_ref.at[k, j], sem).start()
        return carry

    def drain(j, carry):
        for k in range(TOP_K):
            _row_copy(y3_ref.at[row_of(k, j)], buf_ref.at[k, j], sem).wait()
        return carry

    lax.fori_loop(0, tc, fetch, 0)

    xb = h1b_ref[...]
    gs = jnp.dot(xb, wsg_ref[...], preferred_element_type=jnp.float32)
    us = jnp.dot(xb, wsu_ref[...], preferred_element_type=jnp.float32)
    hs = (gs / (1.0 + jnp.exp(-gs)) * us).astype(jnp.bfloat16)
    shared = jnp.dot(hs, wsd_ref[...], preferred_element_type=jnp.float32)

    lax.fori_loop(0, tc, drain, 0)

    def weigh(j, carry):
        acc = gate_ref[0, j] * buf_ref[0, j]
        for k in range(1, TOP_K):
            acc = acc + gate_ref[k, j] * buf_ref[k, j]
        comb_ref[pl.ds(pl.multiple_of(j * SLABS, SLABS), SLABS), :] = acc
        return carry

    lax.fori_loop(0, tc, weigh, 0)
    routed = jnp.concatenate([comb_ref[pl.ds(s, tc, stride=SLABS), :] for s in range(SLABS)], axis=1)
    h2 = _layer_norm(DN_ALPHA * h1_ref[...] + (routed + shared), g_ref[...], b_ref[...])
    h2_ref[...] = h2
    h2b_ref[...] = h2.astype(h2b_ref.dtype)


def _combine(row_base, top_idx, rank, gates, y3, h1, h1b, wsg, wsu, wsd, ln_g, ln_b, tc):
    n = h1.shape[0]
    row = lambda i, *_: (i, 0)
    c2 = lambda i, *_: (0, 0)
    col = lambda i, *_: (0, i)
    smem_col = pl.BlockSpec((TOP_K, tc), col, memory_space=pltpu.SMEM)
    return pl.pallas_call(
        functools.partial(_combine_kernel, tc=tc),
        out_shape=(jax.ShapeDtypeStruct((n, D_MODEL), jnp.float32),
                   jax.ShapeDtypeStruct((n, D_MODEL), jnp.bfloat16)),
        grid_spec=pltpu.PrefetchScalarGridSpec(
            num_scalar_prefetch=1,
            grid=(n // tc,),
            in_specs=[smem_col, smem_col, smem_col,
                      pl.BlockSpec(memory_space=pl.ANY),
                      pl.BlockSpec((tc, D_MODEL), row),
                      pl.BlockSpec((tc, D_MODEL), row),
                      pl.BlockSpec((D_MODEL, D_SHARED), c2),
                      pl.BlockSpec((D_MODEL, D_SHARED), c2),
                      pl.BlockSpec((D_SHARED, D_MODEL), c2),
                      pl.BlockSpec((1, D_MODEL), c2),
                      pl.BlockSpec((1, D_MODEL), c2)],
            out_specs=(pl.BlockSpec((tc, D_MODEL), row),
                       pl.BlockSpec((tc, D_MODEL), row)),
            scratch_shapes=[pltpu.VMEM((TOP_K, tc, SLABS, LANES), jnp.float32),
                            pltpu.VMEM((tc * SLABS, LANES), jnp.float32),
                            pltpu.SemaphoreType.DMA(())]),
        compiler_params=_cparams(("arbitrary",)),
        name="moe_combine",
    )(row_base, top_idx, rank, gates, y3, h1, h1b, wsg, wsu, wsd, ln_g, ln_b)


def _routing_plan(counts, n_blocks):
    blocks_per_e = (counts + ROW_BLOCK - 1) // ROW_BLOCK
    blk_end = jnp.cumsum(blocks_per_e)
    row_base = (blk_end - blocks_per_e) * ROW_BLOCK
    block_ids = jnp.arange(n_blocks, dtype=jnp.int32)
    block_expert = jnp.minimum(jnp.sum((blk_end[None, :] <= block_ids[:, None]).astype(jnp.int32), axis=1),
                               N_EXPERTS - 1)
    n_used = blk_end[-1:]
    pad_lo = row_base + counts
    pad_hi = blk_end * ROW_BLOCK
    return row_base, block_expert, n_used, pad_lo, pad_hi


def _prepare_weights(w_in, w_uk, w_uv, w_o):
    L = w_in.shape[0]
    o = IN_OFFS
    bf = jnp.bfloat16
    w_qa = w_in[:, :, o[0]:o[1]].reshape(L, D_MODEL, A_HEADS, A_HEAD_DIM).transpose(0, 2, 1, 3)
    w_qlat = _fold(w_qa, w_uk, A_HEAD_DIM ** -0.5)
    w_qlat = w_qlat.transpose(0, 2, 1, 3).reshape(L, D_MODEL, A_HEADS * KV_LATENT)
    w_oa = w_o[:, :A_HEADS * A_HEAD_DIM].reshape(L, A_HEADS, A_HEAD_DIM, D_MODEL)
    w_uvo = _fold(w_uv, w_oa, 1.0)
    w_ob = w_o[:, A_HEADS * A_HEAD_DIM:].astype(bf)

    def pad_cols(w, width):
        return jnp.pad(w, ((0, 0), (0, 0), (0, width - w.shape[-1])))

    half = R_QK_DIM // 2
    perm = np.concatenate([np.arange(half) + R_QK_DIM * h for h in range(R_HEADS)]
                          + [np.arange(half) + R_QK_DIM * h + half for h in range(R_HEADS)])
    w_cat = jnp.concatenate([
        w_qlat,
        w_in[:, :, o[1]:o[2]].astype(bf),
        w_in[:, :, o[2]:o[3]].astype(bf),
        pad_cols(w_in[:, :, o[3]:o[4]], LANES).astype(bf),
        pad_cols(w_in[:, :, o[4]:o[5]], LANES).astype(bf),
        w_in[:, :, o[5]:o[6]][:, :, perm].astype(bf),
        w_in[:, :, o[6]:o[7]][:, :, perm].astype(bf),
        w_in[:, :, o[7]:o[8]].astype(bf),
        w_in[:, :, o[8]:o[9]].astype(bf)], axis=-1)
    return w_cat, w_uvo, w_ob


def _rotary_tables(t_pad):
    half = R_QK_DIM // 2
    inv = ROPE_BASE ** (-jnp.arange(half, dtype=jnp.float32) / half)
    ang = jnp.arange(t_pad, dtype=jnp.float32)[:, None] * inv
    return jnp.tile(jnp.cos(ang), (1, R_HEADS)), jnp.tile(jnp.sin(ang), (1, R_HEADS))


def _pick_tile(n, prefer):
    for t in prefer:
        if n % t == 0:
            return t
    raise ValueError(f"no tile for {n}")


def kernel(x, meta_tokens, w_in, w_uk, w_uv, kv_norm_w, ret_norm_w, w_o, ln1_g, ln1_b, w_router, router_bias,
           w_gate, w_up, w_down, ws_gate, ws_up, ws_down, ln2_g, ln2_b):
    b, s, d = x.shape
    assert d == D_MODEL
    L = w_in.shape[0]
    t = s + N_META
    n_sel = min(TOPK_MAX, s // 4)
    t_pad = -(-t // LANES) * LANES
    n = b * t_pad
    bf = jnp.bfloat16

    meta = jnp.broadcast_to(meta_tokens.astype(x.dtype)[None], (b, N_META, d))
    h = jnp.concatenate([meta, x, jnp.zeros((b, t_pad - t, d), x.dtype)], axis=1).reshape(n, d)
    hb = h.astype(bf)

    w_cat, w_uvo, w_ob = _prepare_weights(w_in, w_uk, w_uv, w_o)
    cos_t, sin_t = _rotary_tables(t_pad)
    wg, wu, wd = w_gate.astype(bf), w_up.astype(bf), w_down.astype(bf)
    wsg, wsu, wsd = ws_gate.astype(bf), ws_up.astype(bf), ws_down.astype(bf)
    w_rt = jnp.swapaxes(w_router, 1, 2)

    tm = _pick_tile(n, (512, 256, 128))
    tc = _pick_tile(n, (256, 128))
    n_blocks = -(-(n * TOP_K) // ROW_BLOCK) + N_EXPERTS
    n_rows = n_blocks * ROW_BLOCK

    for l in range(L):
        qlat, ckv, qi, ki, wi, qr, kr, vr, sg = _inproj(hb, w_cat[l], kv_norm_w[l][None], cos_t, sin_t, t_pad)
        olat = _attention(qlat, qi, wi, ckv, ki, b, t_pad, n_sel)
        ob = _retention(qr, kr, vr, sg, ret_norm_w[l][None], b, t_pad)
        h1, h1b, x3, top_idx, gates, rank, counts = _mix_router(
            olat, ob, h, w_uvo[l], w_ob[l], ln1_g[l][None], ln1_b[l][None], w_rt[l], router_bias[l][:, None], tm)
        row_base, block_expert, n_used, pad_lo, pad_hi = _routing_plan(counts[:, 0], n_blocks)
        x_rows = _dispatch(row_base, pad_lo, pad_hi, top_idx, rank, x3.reshape(n, SLABS, LANES), n_rows, tm)
        y_rows = _experts(block_expert, n_used, x_rows.reshape(n_rows * SLABS, LANES), wg[l], wu[l], wd[l])
        h, hb = _combine(row_base, top_idx, rank, gates, y_rows.reshape(n_rows, SLABS, LANES), h1, h1b,
                         wsg[l], wsu[l], wsd[l], ln2_g[l][None], ln2_b[l][None], tc)
    return h.reshape(b, t_pad, d)[:, N_META:t]
```

```python
import functools
import math

import numpy as np
import jax
import jax.numpy as jnp
from jax import lax
from jax.experimental import pallas as pl
from jax.experimental.pallas import tpu as pltpu

D_MODEL = 1024
N_META = 16
A_HEADS = 8
A_HEAD_DIM = 64
KV_LATENT = 128
IDX_HEADS = 8
IDX_DIM = 64
TOPK_MAX = 256
R_HEADS = 4
R_QK_DIM = 64
R_V_DIM = 128
ROPE_BASE = 10000.0
N_EXPERTS = 64
N_GROUPS = 8
GROUP_SIZE = N_EXPERTS // N_GROUPS
TOP_GROUPS = 4
TOP_K = 8
D_EXPERT = 256
D_SHARED = 256
ROUTE_SCALE = 2.5
DEPTH = 4
DN_ALPHA = (2 * DEPTH) ** 0.25
EPS = 1e-6

IN_COLS = (A_HEADS * A_HEAD_DIM, KV_LATENT, IDX_HEADS * IDX_DIM, IDX_DIM, IDX_HEADS,
           R_HEADS * R_QK_DIM, R_HEADS * R_QK_DIM, R_HEADS * R_V_DIM, R_HEADS * R_V_DIM)
IN_OFFS = tuple(int(v) for v in np.cumsum((0,) + IN_COLS))

LANES = 128
SUBLANES = 8
Q_BLOCK = 128
KEY_CHUNK = 256
R_CHUNK = 128
ROW_BLOCK = 256
SLABS = D_MODEL // LANES
VMEM_LIMIT = 56 * 1024 * 1024

C_QLAT = 0
C_CKV = C_QLAT + A_HEADS * KV_LATENT
C_QI = C_CKV + KV_LATENT
C_KI = C_QI + IDX_HEADS * IDX_DIM
C_WI = C_KI + LANES
C_QR = C_WI + LANES
C_KR = C_QR + R_HEADS * R_QK_DIM
C_VR = C_KR + R_HEADS * R_QK_DIM
C_GR = C_VR + R_HEADS * R_V_DIM
C_END = C_GR + R_HEADS * R_V_DIM

INT_MIN = -2 ** 31
NEG_BIG = -1e30

_NT = (((1,), (1,)), ((), ()))
_TN = (((0,), (0,)), ((), ()))


def _cparams(sem):
    return pltpu.CompilerParams(dimension_semantics=sem, vmem_limit_bytes=VMEM_LIMIT)


def _fold_kernel(a_ref, b_ref, o_ref, *, scale):
    o_ref[0, 0] = (jnp.dot(a_ref[0, 0], b_ref[0, 0], preferred_element_type=jnp.float32,
                           precision=lax.Precision.HIGHEST) * scale).astype(o_ref.dtype)


def _fold(a, b, scale):
    L, H, M, K = a.shape
    N = b.shape[-1]
    return pl.pallas_call(
        functools.partial(_fold_kernel, scale=scale),
        out_shape=jax.ShapeDtypeStruct((L, H, M, N), jnp.bfloat16),
        grid=(L, H),
        in_specs=[pl.BlockSpec((1, 1, M, K), lambda l, h: (l, h, 0, 0)),
                  pl.BlockSpec((1, 1, K, N), lambda l, h: (l, h, 0, 0))],
        out_specs=pl.BlockSpec((1, 1, M, N), lambda l, h: (l, h, 0, 0)),
        compiler_params=_cparams(("parallel", "parallel")),
        name="weight_fold",
    )(a, b)


def _inproj_kernel(x_ref, w_ref, kvg_ref, cos_ref, sin_ref,
                   qlat_ref, ckv_ref, qi_ref, ki_ref, wi_ref, qr_ref, kr_ref, vr_ref, sg_ref):
    x = x_ref[...]

    def proj(lo, hi):
        return jnp.dot(x, w_ref[:, lo:hi], preferred_element_type=jnp.float32)

    r = proj(C_QLAT, C_CKV)
    for h in range(A_HEADS):
        qlat_ref[h] = r[:, h * KV_LATENT:(h + 1) * KV_LATENT].astype(qlat_ref.dtype)

    r = proj(C_CKV, C_QI)
    r = r * lax.rsqrt(jnp.mean(r * r, axis=-1, keepdims=True) + EPS) * kvg_ref[...]
    ckv_ref[...] = r.astype(ckv_ref.dtype)

    r = proj(C_QI, C_KI)
    for h in range(IDX_HEADS):
        qi_ref[h] = r[:, h * IDX_DIM:(h + 1) * IDX_DIM].astype(qi_ref.dtype)

    r = proj(C_KI, C_WI)
    ki_ref[...] = r[:, :IDX_DIM].astype(ki_ref.dtype)
    r = proj(C_WI, C_QR)
    wi_ref[...] = r[:, :IDX_HEADS] * (IDX_HEADS ** -0.5)

    cos = cos_ref[...]
    sin = sin_ref[...]
    half = R_HEADS * R_QK_DIM // 2

    def rot(r, out_ref, scale):
        x1, x2 = r[:, :half], r[:, half:]
        out_ref[:, :half] = ((x1 * cos - x2 * sin) * scale).astype(out_ref.dtype)
        out_ref[:, half:] = ((x1 * sin + x2 * cos) * scale).astype(out_ref.dtype)

    rot(proj(C_QR, C_KR), qr_ref, 1.0)
    rot(proj(C_KR, C_VR), kr_ref, R_QK_DIM ** -0.5)
    vr_ref[...] = proj(C_VR, C_GR).astype(vr_ref.dtype)
    g = proj(C_GR, C_END)
    sg_ref[...] = (g / (1.0 + jnp.exp(-g))).astype(sg_ref.dtype)


def _inproj(xb, w_cat, kv_g, cos_t, sin_t, t_pad):
    n = xb.shape[0]
    tm = t_pad // 4
    per_b = t_pad // tm
    bf = jnp.bfloat16
    row = lambda i: (i, 0)
    head = lambda i: (0, i, 0)
    const = lambda i: (0, 0)
    pos = lambda i: (i % per_b, 0)
    return pl.pallas_call(
        _inproj_kernel,
        out_shape=(jax.ShapeDtypeStruct((A_HEADS, n, KV_LATENT), bf),
                   jax.ShapeDtypeStruct((n, KV_LATENT), bf),
                   jax.ShapeDtypeStruct((IDX_HEADS, n, IDX_DIM), bf),
                   jax.ShapeDtypeStruct((n, IDX_DIM), bf),
                   jax.ShapeDtypeStruct((n, IDX_HEADS), jnp.float32),
                   jax.ShapeDtypeStruct((n, R_HEADS * R_QK_DIM), bf),
                   jax.ShapeDtypeStruct((n, R_HEADS * R_QK_DIM), bf),
                   jax.ShapeDtypeStruct((n, R_HEADS * R_V_DIM), bf),
                   jax.ShapeDtypeStruct((n, R_HEADS * R_V_DIM), bf)),
        grid=(n // tm,),
        in_specs=[pl.BlockSpec((tm, D_MODEL), row),
                  pl.BlockSpec((D_MODEL, C_END), const),
                  pl.BlockSpec((1, KV_LATENT), const),
                  pl.BlockSpec((tm, LANES), pos),
                  pl.BlockSpec((tm, LANES), pos)],
        out_specs=(pl.BlockSpec((A_HEADS, tm, KV_LATENT), head),
                   pl.BlockSpec((tm, KV_LATENT), row),
                   pl.BlockSpec((IDX_HEADS, tm, IDX_DIM), head),
                   pl.BlockSpec((tm, IDX_DIM), row),
                   pl.BlockSpec((tm, IDX_HEADS), row),
                   pl.BlockSpec((tm, R_HEADS * R_QK_DIM), row),
                   pl.BlockSpec((tm, R_HEADS * R_QK_DIM), row),
                   pl.BlockSpec((tm, R_HEADS * R_V_DIM), row),
                   pl.BlockSpec((tm, R_HEADS * R_V_DIM), row)),
        compiler_params=_cparams(("parallel",)),
        name="in_proj",
    )(xb, w_cat, kv_g, cos_t, sin_t)


def _fold_rows(x, op):
    out = x[:SUBLANES]
    for r in range(1, x.shape[0] // SUBLANES):
        out = op(out, x[r * SUBLANES:(r + 1) * SUBLANES])
    return out


def _attn_kernel(qlat_ref, qi_ref, wit_ref, ckv_ref, ckvt_ref, ki_ref, o_ref, key_scr, lg_scr, acc_scr,
                 *, n_sel, t_pad):
    i = pl.program_id(1)
    nk = ((i + 1) * Q_BLOCK + KEY_CHUNK - 1) // KEY_CHUNK
    tile = (KEY_CHUNK, Q_BLOCK)
    cols_all = A_HEADS * Q_BLOCK
    q_all = qlat_ref[...].reshape(cols_all, KV_LATENT)
    qi_all = qi_ref[...].reshape(IDX_HEADS * Q_BLOCK, IDX_DIM)
    wt = wit_ref[...]
    k_off = lax.broadcasted_iota(jnp.int32, tile, 0)
    q_off = lax.broadcasted_iota(jnp.int32, tile, 1)

    def key_rows(kt):
        return pl.ds(pl.multiple_of(kt * KEY_CHUNK, KEY_CHUNK), KEY_CHUNK)

    def key_pos(kt):
        return kt * KEY_CHUNK + k_off

    def index_tile(kt, carry):
        qk = lax.dot_general(ki_ref[key_rows(kt), :], qi_all, _NT, preferred_element_type=jnp.float32)
        score = wt[0:1] * jnp.maximum(qk[:, :Q_BLOCK], 0.0)
        for h in range(1, IDX_HEADS):
            score = score + wt[h:h + 1] * jnp.maximum(qk[:, h * Q_BLOCK:(h + 1) * Q_BLOCK], 0.0)
        bits = lax.bitcast_convert_type(score, jnp.int32)
        key = bits ^ ((bits >> 31) & jnp.int32(0x7FFFFFFF))
        key = jnp.where(score == 0.0, 0, key)
        causal = key_pos(kt) <= (i * Q_BLOCK + q_off)
        key_scr[kt] = jnp.where(causal, key, INT_MIN)
        return carry

    lax.fori_loop(0, nk, index_tile, 0)

    def count(pred):
        def body(kt, acc):
            return acc + jnp.where(pred(key_scr[kt], kt), 1.0, 0.0)
        acc = lax.fori_loop(0, nk, body, jnp.zeros(tile, jnp.float32))
        return jnp.sum(acc, axis=0, keepdims=True)

    def thr_step(it, t):
        cand = t ^ lax.shift_left(jnp.int32(1), 31 - it)
        return jnp.where(count(lambda k, kt: k >= cand) >= n_sel, cand, t)

    t = lax.fori_loop(0, 32, thr_step, jnp.full((1, Q_BLOCK), INT_MIN, jnp.int32))

    need = n_sel - count(lambda k, kt: k > t)
    n_eq = count(lambda k, kt: k == t)
    idx_bits = max(1, (t_pad - 1).bit_length())
    surplus = jnp.max(jnp.where((t > INT_MIN) & (n_eq > need), 1.0, 0.0)) > 0.0

    def tie_search():
        def tie_step(it, c):
            cand = c | lax.shift_left(jnp.int32(1), idx_bits - 1 - it)
            below = count(lambda k, kt: (k == t) & (key_pos(kt) < cand))
            return jnp.where(below <= need, cand, c)
        return lax.fori_loop(0, idx_bits, tie_step, jnp.zeros((1, Q_BLOCK), jnp.int32))

    c = lax.cond(surplus, tie_search, lambda: jnp.full((1, Q_BLOCK), 1 << idx_bits, jnp.int32))

    def logit_tile(kt, m8):
        key = key_scr[kt]
        sel = ((key > t) | ((key == t) & (key_pos(kt) < c))) & (key != INT_MIN)
        lg = lax.dot_general(ckv_ref[key_rows(kt), :], q_all, _NT, preferred_element_type=jnp.float32)
        lg = jnp.where(jnp.concatenate([sel] * A_HEADS, axis=1), lg, NEG_BIG)
        lg_scr[kt] = lg
        return jnp.maximum(m8, _fold_rows(lg, jnp.maximum))

    m8 = lax.fori_loop(0, nk, logit_tile, jnp.full((SUBLANES, cols_all), NEG_BIG, jnp.float32))
    m = jnp.max(m8, axis=0, keepdims=True)
    acc_scr[...] = jnp.zeros_like(acc_scr)

    def pv_tile(kt, l8):
        p = jnp.exp(lg_scr[kt] - m)
        acc_scr[...] += jnp.dot(ckvt_ref[kt], p.astype(ckvt_ref.dtype), preferred_element_type=jnp.float32)
        return l8 + _fold_rows(p, jnp.add)

    l8 = lax.fori_loop(0, nk, pv_tile, jnp.zeros((SUBLANES, cols_all), jnp.float32))
    o_t = acc_scr[...] / jnp.sum(l8, axis=0, keepdims=True)
    for h in range(A_HEADS):
        o_ref[h] = o_t[:, h * Q_BLOCK:(h + 1) * Q_BLOCK].T.astype(o_ref.dtype)


def _attention(qlat, qi, wi, ckv, ki, batch, t_pad, n_sel):
    n = ckv.shape[0]
    nq = t_pad // Q_BLOCK
    nkc = -(-t_pad // KEY_CHUNK)
    t_keys = nkc * KEY_CHUNK

    def pad_keys(a):
        a = a.reshape(batch, t_pad, a.shape[-1])
        return jnp.pad(a, ((0, 0), (0, t_keys - t_pad), (0, 0))).reshape(batch * t_keys, a.shape[-1])

    ckv_p, ki_p = pad_keys(ckv), pad_keys(ki)
    ckv_t = jnp.swapaxes(ckv_p.reshape(batch * nkc, KEY_CHUNK, KV_LATENT), 1, 2)
    qmap = lambda b, i: (0, b * nq + i, 0)
    cols_all = A_HEADS * Q_BLOCK
    scratch = [pltpu.VMEM((nkc, KEY_CHUNK, Q_BLOCK), jnp.int32),
               pltpu.VMEM((nkc, KEY_CHUNK, cols_all), jnp.float32),
               pltpu.VMEM((KV_LATENT, cols_all), jnp.float32)]
    return pl.pallas_call(
        functools.partial(_attn_kernel, n_sel=n_sel, t_pad=t_keys),
        out_shape=jax.ShapeDtypeStruct((A_HEADS, n, KV_LATENT), jnp.bfloat16),
        grid=(batch, nq),
        in_specs=[pl.BlockSpec((A_HEADS, Q_BLOCK, KV_LATENT), qmap),
                  pl.BlockSpec((IDX_HEADS, Q_BLOCK, IDX_DIM), qmap),
                  pl.BlockSpec((IDX_HEADS, Q_BLOCK), lambda b, i: (0, b * nq + i)),
                  pl.BlockSpec((t_keys, KV_LATENT), lambda b, i: (b, 0)),
                  pl.BlockSpec((nkc, KV_LATENT, KEY_CHUNK), lambda b, i: (b, 0, 0)),
                  pl.BlockSpec((t_keys, IDX_DIM), lambda b, i: (b, 0))],
        out_specs=pl.BlockSpec((A_HEADS, Q_BLOCK, KV_LATENT), qmap),
        scratch_shapes=scratch,
        compiler_params=_cparams(("parallel", "parallel")),
        name="sparse_attention",
    )(qlat, qi, wi.T, ckv_p, ckv_t, ki_p)


def _retention_tables():
    log_g = np.log1p(-np.exp(np.linspace(math.log(1.0 / 32), math.log(1.0 / 512), R_HEADS))).astype(np.float32)
    idx = np.arange(R_CHUNK, dtype=np.float32)
    diff = idx[:, None] - idx[None, :]
    decay = np.where(diff[None] >= 0, np.exp(diff[None] * log_g[:, None, None]), 0.0).astype(np.float32)
    q_decay = np.exp((idx + 1.0)[None, :] * log_g[:, None])[:, :, None].astype(np.float32)
    k_decay = np.exp((R_CHUNK - 1.0 - idx)[None, :] * log_g[:, None])[:, :, None].astype(np.float32)
    chunk_decay = np.exp(R_CHUNK * log_g).astype(np.float32)
    return decay, q_decay, k_decay, chunk_decay


def _retention_kernel(q_ref, k_ref, v_ref, sg_ref, rw_ref, dec_ref, qd_ref, kd_ref, o_ref, state_ref,
                      *, t_pad, chunk_decay):
    state_ref[...] = jnp.zeros_like(state_ref)
    qk_w = R_HEADS * R_QK_DIM
    lane = lax.broadcasted_iota(jnp.int32, (1, qk_w), 1)
    half_w = R_QK_DIM // 2
    head_mask = [((lane % (qk_w // 2)) // half_w) == h for h in range(R_HEADS)]

    def chunk(c, carry):
        r0 = pl.multiple_of(c * R_CHUNK, R_CHUNK)
        rows = pl.ds(r0, R_CHUNK)
        qc = q_ref[rows, :]
        kc = k_ref[rows, :]
        for h in range(R_HEADS):
            cols = slice(h * R_V_DIM, (h + 1) * R_V_DIM)
            qm = jnp.where(head_mask[h], qc, jnp.zeros_like(qc))
            km = jnp.where(head_mask[h], kc, jnp.zeros_like(kc))
            vh = v_ref[rows, cols]
            inner = lax.dot_general(qm, kc, _NT, preferred_element_type=jnp.float32) * dec_ref[h]
            st = state_ref[h]
            out = (jnp.dot(inner.astype(vh.dtype), vh, preferred_element_type=jnp.float32)
                   + jnp.dot(qm, st.astype(qm.dtype), preferred_element_type=jnp.float32) * qd_ref[h])
            kdec = (km.astype(jnp.float32) * kd_ref[h]).astype(km.dtype)
            state_ref[h] = chunk_decay[h] * st + lax.dot_general(kdec, vh, _TN,
                                                                  preferred_element_type=jnp.float32)
            mu = jnp.mean(out, axis=-1, keepdims=True)
            d = out - mu
            var = jnp.mean(d * d, axis=-1, keepdims=True)
            normed = d * lax.rsqrt(var + EPS) * rw_ref[:, cols]
            o_ref[rows, cols] = (sg_ref[rows, cols].astype(jnp.float32) * normed).astype(o_ref.dtype)
        return carry

    lax.fori_loop(0, t_pad // R_CHUNK, chunk, 0)


def _retention(qr, kr, vr, sg, ret_w, batch, t_pad):
    n = qr.shape[0]
    decay, q_decay, k_decay, chunk_decay = _retention_tables()
    qk_w = R_HEADS * R_QK_DIM
    v_w = R_HEADS * R_V_DIM
    per_b = lambda b: (b, 0)
    c2 = lambda b: (0, 0)
    c3 = lambda b: (0, 0, 0)
    return pl.pallas_call(
        functools.partial(_retention_kernel, t_pad=t_pad, chunk_decay=[float(v) for v in chunk_decay]),
        out_shape=jax.ShapeDtypeStruct((n, v_w), jnp.bfloat16),
        grid=(batch,),
        in_specs=[pl.BlockSpec((t_pad, qk_w), per_b),
                  pl.BlockSpec((t_pad, qk_w), per_b),
                  pl.BlockSpec((t_pad, v_w), per_b),
                  pl.BlockSpec((t_pad, v_w), per_b),
                  pl.BlockSpec((1, v_w), c2),
                  pl.BlockSpec((R_HEADS, R_CHUNK, R_CHUNK), c3),
                  pl.BlockSpec((R_HEADS, R_CHUNK, 1), c3),
                  pl.BlockSpec((R_HEADS, R_CHUNK, 1), c3)],
        out_specs=pl.BlockSpec((t_pad, v_w), per_b),
        scratch_shapes=[pltpu.VMEM((R_HEADS, qk_w, R_V_DIM), jnp.float32)],
        compiler_params=_cparams(("parallel",)),
        name="retention",
    )(qr, kr, vr, sg, ret_w, jnp.asarray(decay), jnp.asarray(q_decay), jnp.asarray(k_decay))


def _layer_norm(y, g, b):
    mu = jnp.mean(y, axis=-1, keepdims=True)
    d = y - mu
    var = jnp.mean(d * d, axis=-1, keepdims=True)
    return d * lax.rsqrt(var + EPS) * g + b


def _mix_router_kernel(olat_ref, ob_ref, h_ref, wuvo_ref, wob_ref, g_ref, b_ref, wrt_ref, rb_ref, tri_ref,
                       h1_ref, h1b_ref, x3_ref, gate_ref, rank_ref, cnt_ref, run_ref, *, tm, cap):
    mix = jnp.dot(ob_ref[...], wob_ref[...], preferred_element_type=jnp.float32)
    for h in range(A_HEADS):
        mix = mix + jnp.dot(olat_ref[h], wuvo_ref[h], preferred_element_type=jnp.float32)
    h1 = _layer_norm(DN_ALPHA * h_ref[...] + mix, g_ref[...], b_ref[...])
    h1_ref[...] = h1
    h1b_ref[...] = h1.astype(h1b_ref.dtype)
    for s in range(SLABS):
        x3_ref[pl.ds(s, tm, stride=SLABS), :] = h1[:, s * LANES:(s + 1) * LANES]

    logits = lax.dot_general(wrt_ref[...], h1, _NT, preferred_element_type=jnp.float32,
                             precision=lax.Precision.HIGHEST)
    scores = 1.0 / (1.0 + jnp.exp(-logits))
    sel = scores + rb_ref[...]
    neg = -jnp.inf
    iota_g = lax.broadcasted_iota(jnp.int32, (GROUP_SIZE, tm), 0)
    iota_n = lax.broadcasted_iota(jnp.int32, (N_GROUPS, tm), 0)

    def first_argmax(v, iota, big):
        m = jnp.max(v, axis=0, keepdims=True)
        return m, jnp.min(jnp.where(v == m, iota, big), axis=0, keepdims=True)

    grp_score = jnp.zeros((N_GROUPS, tm), jnp.float32)
    for g in range(N_GROUPS):
        blk = sel[g * GROUP_SIZE:(g + 1) * GROUP_SIZE]
        m1, i1 = first_argmax(blk, iota_g, GROUP_SIZE)
        m2 = jnp.max(jnp.where(iota_g == i1, neg, blk), axis=0, keepdims=True)
        grp_score = jnp.where(iota_n == g, m1 + m2, grp_score)

    grp_on = jnp.zeros((N_GROUPS, tm), jnp.float32)
    work = grp_score
    for _ in range(TOP_GROUPS):
        _, gi = first_argmax(work, iota_n, N_GROUPS)
        hit = iota_n == gi
        grp_on = jnp.where(hit, 1.0, grp_on)
        work = jnp.where(hit, neg, work)

    masked = jnp.concatenate(
        [jnp.where(grp_on[g:g + 1] > 0.0, sel[g * GROUP_SIZE:(g + 1) * GROUP_SIZE], neg)
         for g in range(N_GROUPS)], axis=0)
    iota_e = lax.broadcasted_iota(jnp.int32, (N_EXPERTS, tm), 0)
    iota_k = lax.broadcasted_iota(jnp.int32, (TOP_K, tm), 0)
    top_idx = jnp.zeros((TOP_K, tm), jnp.int32)
    top_gate = jnp.zeros((TOP_K, tm), jnp.float32)
    hits = []
    for k in range(TOP_K):
        _, ei = first_argmax(masked, iota_e, N_EXPERTS)
        hit = iota_e == ei
        hits.append(hit)
        gk = jnp.sum(jnp.where(hit, scores, 0.0), axis=0, keepdims=True)
        masked = jnp.where(hit, neg, masked)
        top_idx = jnp.where(iota_k == k, ei, top_idx)
        top_gate = jnp.where(iota_k == k, gk, top_gate)
    gate_ref[...] = top_gate / jnp.sum(top_gate, axis=0, keepdims=True) * ROUTE_SCALE

    @pl.when(pl.program_id(0) == 0)
    def _():
        run_ref[...] = jnp.zeros_like(run_ref)

    onehot = jnp.zeros((N_EXPERTS, tm), jnp.float32)
    for hit in hits:
        onehot = jnp.where(hit, 1.0, onehot)
    before = jnp.dot(onehot.astype(jnp.bfloat16), tri_ref[...], preferred_element_type=jnp.float32)
    before = before + run_ref[...]
    rank = jnp.zeros((TOP_K, tm), jnp.float32)
    for k in range(TOP_K):
        rank = jnp.where(iota_k == k, jnp.sum(jnp.where(hits[k], before, 0.0), axis=0, keepdims=True), rank)
    rank_ref[...] = top_idx * cap + rank.astype(jnp.int32)
    run_ref[...] += jnp.sum(onehot, axis=1, keepdims=True)
    cnt_ref[...] = run_ref[...].astype(jnp.int32)


def _mix_router(olat, ob, h, wuvo, wob, ln_g, ln_b, wrt, rbias, tm, cap):
    n = h.shape[0]
    row = lambda i: (i, 0)
    c2 = lambda i: (0, 0)
    c3 = lambda i: (0, 0, 0)
    col = lambda i: (0, i)
    v_w = R_HEADS * R_V_DIM
    tri = jnp.triu(jnp.ones((tm, tm), jnp.bfloat16), k=1)
    return pl.pallas_call(
        functools.partial(_mix_router_kernel, tm=tm, cap=cap),
        out_shape=(jax.ShapeDtypeStruct((n, D_MODEL), jnp.float32),
                   jax.ShapeDtypeStruct((n, D_MODEL), jnp.bfloat16),
                   jax.ShapeDtypeStruct((n * SLABS, LANES), jnp.float32),
                   jax.ShapeDtypeStruct((TOP_K, n), jnp.float32),
                   jax.ShapeDtypeStruct((TOP_K, n), jnp.int32),
                   jax.ShapeDtypeStruct((N_EXPERTS, 1), jnp.int32)),
        grid=(n // tm,),
        in_specs=[pl.BlockSpec((A_HEADS, tm, KV_LATENT), lambda i: (0, i, 0)),
                  pl.BlockSpec((tm, v_w), row),
                  pl.BlockSpec((tm, D_MODEL), row),
                  pl.BlockSpec((A_HEADS, KV_LATENT, D_MODEL), c3),
                  pl.BlockSpec((v_w, D_MODEL), c2),
                  pl.BlockSpec((1, D_MODEL), c2),
                  pl.BlockSpec((1, D_MODEL), c2),
                  pl.BlockSpec((N_EXPERTS, D_MODEL), c2),
                  pl.BlockSpec((N_EXPERTS, 1), c2),
                  pl.BlockSpec((tm, tm), c2)],
        out_specs=(pl.BlockSpec((tm, D_MODEL), row),
                   pl.BlockSpec((tm, D_MODEL), row),
                   pl.BlockSpec((tm * SLABS, LANES), row),
                   pl.BlockSpec((TOP_K, tm), col),
                   pl.BlockSpec((TOP_K, tm), col),
                   pl.BlockSpec((N_EXPERTS, 1), c2)),
        scratch_shapes=[pltpu.VMEM((N_EXPERTS, 1), jnp.float32)],
        compiler_params=_cparams(("arbitrary",)),
        name="mix_ln_router",
    )(olat, ob, h, wuvo, wob, ln_g, ln_b, wrt, rbias, tri)


def _row_copy(src, dst, sem):
    return pltpu.make_async_copy(src, dst, sem)


def _dispatch_kernel(pad_lo_ref, pad_hi_ref, pos_ref, x3_ref, rows_ref, zero_ref, sem, *, td):
    @pl.when(pl.program_id(0) == 0)
    def _():
        zero_ref[...] = jnp.zeros_like(zero_ref)

        def fill(e, carry):
            def one(r, c):
                _row_copy(zero_ref, rows_ref.at[r], sem).start()
                return c
            return lax.fori_loop(pad_lo_ref[e], pad_hi_ref[e], one, carry)

        def drain(e, carry):
            def one(r, c):
                _row_copy(zero_ref, rows_ref.at[r], sem).wait()
                return c
            return lax.fori_loop(pad_lo_ref[e], pad_hi_ref[e], one, carry)

        lax.fori_loop(0, N_EXPERTS, fill, 0)
        lax.fori_loop(0, N_EXPERTS, drain, 0)

    def send(j, carry):
        for k in range(TOP_K):
            _row_copy(x3_ref.at[j], rows_ref.at[pos_ref[k, j]], sem).start()
        return carry

    def drain_tok(j, carry):
        for k in range(TOP_K):
            _row_copy(x3_ref.at[j], rows_ref.at[pos_ref[k, j]], sem).wait()
        return carry

    lax.fori_loop(0, td, send, 0)
    lax.fori_loop(0, td, drain_tok, 0)


def _dispatch(pad_lo, pad_hi, pos, x3, n_rows, td):
    n = pos.shape[1]
    return pl.pallas_call(
        functools.partial(_dispatch_kernel, td=td),
        out_shape=jax.ShapeDtypeStruct((n_rows, SLABS, LANES), jnp.float32),
        grid_spec=pltpu.PrefetchScalarGridSpec(
            num_scalar_prefetch=2,
            grid=(n // td,),
            in_specs=[pl.BlockSpec((TOP_K, td), lambda i, *_: (0, i), memory_space=pltpu.SMEM),
                      pl.BlockSpec((td, SLABS, LANES), lambda i, *_: (i, 0, 0))],
            out_specs=pl.BlockSpec(memory_space=pl.ANY),
            scratch_shapes=[pltpu.VMEM((SLABS, LANES), jnp.float32), pltpu.SemaphoreType.DMA(())]),
        compiler_params=pltpu.CompilerParams(dimension_semantics=("arbitrary",), has_side_effects=True),
        name="moe_dispatch",
    )(pad_lo, pad_hi, pos, x3)


def _expert_kernel(be_ref, br_ref, nu_ref, x_ref, wgu_ref, wd_ref, y_ref):
    @pl.when(pl.program_id(0) < nu_ref[0])
    def _():
        pair = 2 * LANES
        gu = jnp.zeros((ROW_BLOCK, 2 * D_EXPERT), jnp.float32)
        for s in range(0, SLABS, 2):
            xs = jnp.concatenate([x_ref[pl.ds(s, ROW_BLOCK, stride=SLABS), :],
                                  x_ref[pl.ds(s + 1, ROW_BLOCK, stride=SLABS), :]], axis=1).astype(jnp.bfloat16)
            gu = gu + jnp.dot(xs, wgu_ref[0, s * LANES:s * LANES + pair, :], preferred_element_type=jnp.float32)
        g, u = gu[:, :D_EXPERT], gu[:, D_EXPERT:]
        hdn = (g / (1.0 + jnp.exp(-g)) * u).astype(jnp.bfloat16)
        y = jnp.dot(hdn, wd_ref[0], preferred_element_type=jnp.float32)
        for s in range(SLABS):
            y_ref[pl.ds(s, ROW_BLOCK, stride=SLABS), :] = y[:, s * LANES:(s + 1) * LANES]


def _experts(block_expert, block_row, n_used, x_rows2, wgu, wd):
    n_blocks = block_expert.shape[0]
    blk = lambda i, be, br, nu: (br[jnp.minimum(i, nu[0] - 1)], 0)
    wsel = lambda i, be, br, nu: (be[jnp.minimum(i, nu[0] - 1)], 0, 0)
    return pl.pallas_call(
        _expert_kernel,
        out_shape=jax.ShapeDtypeStruct(x_rows2.shape, jnp.float32),
        grid_spec=pltpu.PrefetchScalarGridSpec(
            num_scalar_prefetch=3,
            grid=(n_blocks,),
            in_specs=[pl.BlockSpec((ROW_BLOCK * SLABS, LANES), blk),
                      pl.BlockSpec((1, D_MODEL, 2 * D_EXPERT), wsel),
                      pl.BlockSpec((1, D_EXPERT, D_MODEL), wsel)],
            out_specs=pl.BlockSpec((ROW_BLOCK * SLABS, LANES), blk)),
        compiler_params=_cparams(("arbitrary",)),
        name="moe_experts",
    )(block_expert, block_row, n_used, x_rows2, wgu, wd)


def _combine_kernel(pos_ref, gate_ref, y3_ref, h1_ref, h1b_ref, wsg_ref, wsu_ref, wsd_ref,
                    g_ref, b_ref, h2_ref, h2b_ref, buf_ref, comb_ref, sem, *, tc):
    def fetch(j, carry):
        for k in range(TOP_K):
            _row_copy(y3_ref.at[pos_ref[k, j]], buf_ref.at[k, j], sem).start()
        return carry

    def drain(j, carry):
        for k in range(TOP_K):
            _row_copy(y3_ref.at[pos_ref[k, j]], buf_ref.at[k, j], sem).wait()
        return carry

    lax.fori_loop(0, tc, fetch, 0)

    xb = h1b_ref[...]
    gs = jnp.dot(xb, wsg_ref[...], preferred_element_type=jnp.float32)
    us = jnp.dot(xb, wsu_ref[...], preferred_element_type=jnp.float32)
    hs = (gs / (1.0 + jnp.exp(-gs)) * us).astype(jnp.bfloat16)
    shared = jnp.dot(hs, wsd_ref[...], preferred_element_type=jnp.float32)

    lax.fori_loop(0, tc, drain, 0)

    def weigh(j, carry):
        acc = gate_ref[0, j] * buf_ref[0, j]
        for k in range(1, TOP_K):
            acc = acc + gate_ref[k, j] * buf_ref[k, j]
        comb_ref[pl.ds(pl.multiple_of(j * SLABS, SLABS), SLABS), :] = acc
        return carry

    lax.fori_loop(0, tc, weigh, 0)
    routed = jnp.concatenate([comb_ref[pl.ds(s, tc, stride=SLABS), :] for s in range(SLABS)], axis=1)
    h2 = _layer_norm(DN_ALPHA * h1_ref[...] + (routed + shared), g_ref[...], b_ref[...])
    h2_ref[...] = h2
    h2b_ref[...] = h2.astype(h2b_ref.dtype)


def _combine(pos, gates, y3, h1, h1b, wsg, wsu, wsd, ln_g, ln_b, tc):
    n = h1.shape[0]
    row = lambda i: (i, 0)
    c2 = lambda i: (0, 0)
    smem_col = pl.BlockSpec((TOP_K, tc), lambda i: (0, i), memory_space=pltpu.SMEM)
    return pl.pallas_call(
        functools.partial(_combine_kernel, tc=tc),
        out_shape=(jax.ShapeDtypeStruct((n, D_MODEL), jnp.float32),
                   jax.ShapeDtypeStruct((n, D_MODEL), jnp.bfloat16)),
        grid=(n // tc,),
        in_specs=[smem_col, smem_col,
                  pl.BlockSpec(memory_space=pl.ANY),
                  pl.BlockSpec((tc, D_MODEL), row),
                  pl.BlockSpec((tc, D_MODEL), row),
                  pl.BlockSpec((D_MODEL, D_SHARED), c2),
                  pl.BlockSpec((D_MODEL, D_SHARED), c2),
                  pl.BlockSpec((D_SHARED, D_MODEL), c2),
                  pl.BlockSpec((1, D_MODEL), c2),
                  pl.BlockSpec((1, D_MODEL), c2)],
        out_specs=(pl.BlockSpec((tc, D_MODEL), row),
                   pl.BlockSpec((tc, D_MODEL), row)),
        scratch_shapes=[pltpu.VMEM((TOP_K, tc, SLABS, LANES), jnp.float32),
                        pltpu.VMEM((tc * SLABS, LANES), jnp.float32),
                        pltpu.SemaphoreType.DMA(())],
        compiler_params=_cparams(("arbitrary",)),
        name="moe_combine",
    )(pos, gates, y3, h1, h1b, wsg, wsu, wsd, ln_g, ln_b)


def _routing_plan(counts, n_blocks, cap):
    blocks_per_e = (counts + ROW_BLOCK - 1) // ROW_BLOCK
    blk_end = jnp.cumsum(blocks_per_e)
    blk_start = blk_end - blocks_per_e
    block_ids = jnp.arange(n_blocks, dtype=jnp.int32)
    block_expert = jnp.minimum(jnp.sum((blk_end[None, :] <= block_ids[:, None]).astype(jnp.int32), axis=1),
                               N_EXPERTS - 1)
    start_of = jnp.sum(jnp.where(block_expert[:, None] == jnp.arange(N_EXPERTS, dtype=jnp.int32)[None, :],
                                 blk_start[None, :], 0), axis=1)
    block_row = block_expert * (cap // ROW_BLOCK) + (block_ids - start_of)
    expert_row0 = jnp.arange(N_EXPERTS, dtype=jnp.int32) * cap
    pad_lo = expert_row0 + counts
    pad_hi = expert_row0 + blocks_per_e * ROW_BLOCK
    return block_expert, block_row.astype(jnp.int32), blk_end[-1:], pad_lo, pad_hi


def _prepare_weights(w_in, w_uk, w_uv, w_o):
    L = w_in.shape[0]
    o = IN_OFFS
    bf = jnp.bfloat16
    w_qa = w_in[:, :, o[0]:o[1]].reshape(L, D_MODEL, A_HEADS, A_HEAD_DIM).transpose(0, 2, 1, 3)
    w_qlat = _fold(w_qa, w_uk, A_HEAD_DIM ** -0.5)
    w_qlat = w_qlat.transpose(0, 2, 1, 3).reshape(L, D_MODEL, A_HEADS * KV_LATENT)
    w_oa = w_o[:, :A_HEADS * A_HEAD_DIM].reshape(L, A_HEADS, A_HEAD_DIM, D_MODEL)
    w_uvo = _fold(w_uv, w_oa, 1.0)
    w_ob = w_o[:, A_HEADS * A_HEAD_DIM:].astype(bf)

    def pad_cols(w, width):
        return jnp.pad(w, ((0, 0), (0, 0), (0, width - w.shape[-1])))

    half = R_QK_DIM // 2
    perm = np.concatenate([np.arange(half) + R_QK_DIM * h for h in range(R_HEADS)]
                          + [np.arange(half) + R_QK_DIM * h + half for h in range(R_HEADS)])
    w_cat = jnp.concatenate([
        w_qlat,
        w_in[:, :, o[1]:o[2]].astype(bf),
        w_in[:, :, o[2]:o[3]].astype(bf),
        pad_cols(w_in[:, :, o[3]:o[4]], LANES).astype(bf),
        pad_cols(w_in[:, :, o[4]:o[5]], LANES).astype(bf),
        w_in[:, :, o[5]:o[6]][:, :, perm].astype(bf),
        w_in[:, :, o[6]:o[7]][:, :, perm].astype(bf),
        w_in[:, :, o[7]:o[8]].astype(bf),
        w_in[:, :, o[8]:o[9]].astype(bf)], axis=-1)
    return w_cat, w_uvo, w_ob


def _rotary_tables(t_pad):
    half = R_QK_DIM // 2
    inv = ROPE_BASE ** (-jnp.arange(half, dtype=jnp.float32) / half)
    ang = jnp.arange(t_pad, dtype=jnp.float32)[:, None] * inv
    return jnp.tile(jnp.cos(ang), (1, R_HEADS)), jnp.tile(jnp.sin(ang), (1, R_HEADS))


def _pick_tile(n, prefer):
    for t in prefer:
        if n % t == 0:
            return t
    raise ValueError(f"no tile for {n}")


def kernel(x, meta_tokens, w_in, w_uk, w_uv, kv_norm_w, ret_norm_w, w_o, ln1_g, ln1_b, w_router, router_bias,
           w_gate, w_up, w_down, ws_gate, ws_up, ws_down, ln2_g, ln2_b):
    b, s, d = x.shape
    assert d == D_MODEL
    L = w_in.shape[0]
    t = s + N_META
    n_sel = min(TOPK_MAX, s // 4)
    t_pad = -(-t // LANES) * LANES
    n = b * t_pad
    bf = jnp.bfloat16

    meta = jnp.broadcast_to(meta_tokens.astype(x.dtype)[None], (b, N_META, d))
    h = jnp.concatenate([meta, x, jnp.zeros((b, t_pad - t, d), x.dtype)], axis=1).reshape(n, d)
    hb = h.astype(bf)

    w_cat, w_uvo, w_ob = _prepare_weights(w_in, w_uk, w_uv, w_o)
    cos_t, sin_t = _rotary_tables(t_pad)
    wgu = jnp.concatenate([w_gate.astype(bf), w_up.astype(bf)], axis=-1)
    wd = w_down.astype(bf)
    wsg, wsu, wsd = ws_gate.astype(bf), ws_up.astype(bf), ws_down.astype(bf)
    w_rt = jnp.swapaxes(w_router, 1, 2)

    tm = _pick_tile(n, (512, 256, 128))
    tc = _pick_tile(n, (256, 128))
    n_blocks = -(-(n * TOP_K) // ROW_BLOCK) + N_EXPERTS
    cap = -(-n // ROW_BLOCK) * ROW_BLOCK
    n_rows = N_EXPERTS * cap

    for l in range(L):
        qlat, ckv, qi, ki, wi, qr, kr, vr, sg = _inproj(hb, w_cat[l], kv_norm_w[l][None], cos_t, sin_t, t_pad)
        olat = _attention(qlat, qi, wi, ckv, ki, b, t_pad, n_sel)
        ob = _retention(qr, kr, vr, sg, ret_norm_w[l][None], b, t_pad)
        h1, h1b, x3, gates, pos, counts = _mix_router(
            olat, ob, h, w_uvo[l], w_ob[l], ln1_g[l][None], ln1_b[l][None], w_rt[l], router_bias[l][:, None], tm, cap)
        block_expert, block_row, n_used, pad_lo, pad_hi = _routing_plan(counts[:, 0], n_blocks, cap)
        x_rows = _dispatch(pad_lo, pad_hi, pos, x3.reshape(n, SLABS, LANES), n_rows, tm)
        y_rows = _experts(block_expert, block_row, n_used, x_rows.reshape(n_rows * SLABS, LANES), wgu[l], wd[l])
        h, hb = _combine(pos, gates, y_rows.reshape(n_rows, SLABS, LANES), h1, h1b,
                         wsg[l], wsu[l], wsd[l], ln2_g[l][None], ln2_b[l][None], tc)
    return h.reshape(b, t_pad, d)[:, N_META:t]
```

```python
import functools
import math

import numpy as np
import jax
import jax.numpy as jnp
from jax import lax
from jax.experimental import pallas as pl
from jax.experimental.pallas import tpu as pltpu

D_MODEL = 1024
N_META = 16
A_HEADS = 8
A_HEAD_DIM = 64
KV_LATENT = 128
IDX_HEADS = 8
IDX_DIM = 64
TOPK_MAX = 256
R_HEADS = 4
R_QK_DIM = 64
R_V_DIM = 128
ROPE_BASE = 10000.0
N_EXPERTS = 64
N_GROUPS = 8
GROUP_SIZE = N_EXPERTS // N_GROUPS
TOP_GROUPS = 4
TOP_K = 8
D_EXPERT = 256
D_SHARED = 256
ROUTE_SCALE = 2.5
DEPTH = 4
DN_ALPHA = (2 * DEPTH) ** 0.25
EPS = 1e-6

IN_COLS = (A_HEADS * A_HEAD_DIM, KV_LATENT, IDX_HEADS * IDX_DIM, IDX_DIM, IDX_HEADS,
           R_HEADS * R_QK_DIM, R_HEADS * R_QK_DIM, R_HEADS * R_V_DIM, R_HEADS * R_V_DIM)
IN_OFFS = tuple(int(v) for v in np.cumsum((0,) + IN_COLS))

LANES = 128
SUBLANES = 8
Q_BLOCK = 128
KEY_CHUNK = 256
R_CHUNK = 128
ROW_BLOCK = 256
MOE_TILE = 256
SEG_ROWS = 40
SLABS = D_MODEL // LANES
VMEM_LIMIT = 56 * 1024 * 1024

C_QLAT = 0
C_CKV = C_QLAT + A_HEADS * KV_LATENT
C_QI = C_CKV + KV_LATENT
C_KI = C_QI + IDX_HEADS * IDX_DIM
C_WI = C_KI + LANES
C_QR = C_WI + LANES
C_KR = C_QR + R_HEADS * R_QK_DIM
C_VR = C_KR + R_HEADS * R_QK_DIM
C_GR = C_VR + R_HEADS * R_V_DIM
C_END = C_GR + R_HEADS * R_V_DIM

INT_MIN = -2 ** 31
NEG_BIG = -1e30

_NT = (((1,), (1,)), ((), ()))
_TN = (((0,), (0,)), ((), ()))


def _cparams(sem):
    return pltpu.CompilerParams(dimension_semantics=sem, vmem_limit_bytes=VMEM_LIMIT)


def _fold_kernel(a_ref, b_ref, o_ref, *, scale):
    o_ref[0, 0] = (jnp.dot(a_ref[0, 0], b_ref[0, 0], preferred_element_type=jnp.float32,
                           precision=lax.Precision.HIGHEST) * scale).astype(o_ref.dtype)


def _fold(a, b, scale):
    L, H, M, K = a.shape
    N = b.shape[-1]
    return pl.pallas_call(
        functools.partial(_fold_kernel, scale=scale),
        out_shape=jax.ShapeDtypeStruct((L, H, M, N), jnp.bfloat16),
        grid=(L, H),
        in_specs=[pl.BlockSpec((1, 1, M, K), lambda l, h: (l, h, 0, 0)),
                  pl.BlockSpec((1, 1, K, N), lambda l, h: (l, h, 0, 0))],
        out_specs=pl.BlockSpec((1, 1, M, N), lambda l, h: (l, h, 0, 0)),
        compiler_params=_cparams(("parallel", "parallel")),
        name="weight_fold",
    )(a, b)


def _inproj_kernel(x_ref, w_ref, kvg_ref, cos_ref, sin_ref,
                   qlat_ref, ckv_ref, qi_ref, ki_ref, wi_ref, qr_ref, kr_ref, vr_ref, sg_ref):
    x = x_ref[...]

    def proj(lo, hi):
        return jnp.dot(x, w_ref[:, lo:hi], preferred_element_type=jnp.float32)

    r = proj(C_QLAT, C_CKV)
    for h in range(A_HEADS):
        qlat_ref[h] = r[:, h * KV_LATENT:(h + 1) * KV_LATENT].astype(qlat_ref.dtype)

    r = proj(C_CKV, C_QI)
    r = r * lax.rsqrt(jnp.mean(r * r, axis=-1, keepdims=True) + EPS) * kvg_ref[...]
    ckv_ref[...] = r.astype(ckv_ref.dtype)

    r = proj(C_QI, C_KI)
    for h in range(IDX_HEADS):
        qi_ref[h] = r[:, h * IDX_DIM:(h + 1) * IDX_DIM].astype(qi_ref.dtype)

    r = proj(C_KI, C_WI)
    ki_ref[...] = r[:, :IDX_DIM].astype(ki_ref.dtype)
    r = proj(C_WI, C_QR)
    wi_ref[...] = r[:, :IDX_HEADS] * (IDX_HEADS ** -0.5)

    cos = cos_ref[...]
    sin = sin_ref[...]
    half = R_HEADS * R_QK_DIM // 2

    def rot(r, out_ref, scale):
        x1, x2 = r[:, :half], r[:, half:]
        out_ref[:, :half] = ((x1 * cos - x2 * sin) * scale).astype(out_ref.dtype)
        out_ref[:, half:] = ((x1 * sin + x2 * cos) * scale).astype(out_ref.dtype)

    rot(proj(C_QR, C_KR), qr_ref, 1.0)
    rot(proj(C_KR, C_VR), kr_ref, R_QK_DIM ** -0.5)
    vr_ref[...] = proj(C_VR, C_GR).astype(vr_ref.dtype)
    g = proj(C_GR, C_END)
    sg_ref[...] = (g / (1.0 + jnp.exp(-g))).astype(sg_ref.dtype)


def _inproj(xb, w_cat, kv_g, cos_t, sin_t, t_pad):
    n = xb.shape[0]
    tm = t_pad // 4
    per_b = t_pad // tm
    bf = jnp.bfloat16
    row = lambda i: (i, 0)
    head = lambda i: (0, i, 0)
    const = lambda i: (0, 0)
    pos = lambda i: (i % per_b, 0)
    return pl.pallas_call(
        _inproj_kernel,
        out_shape=(jax.ShapeDtypeStruct((A_HEADS, n, KV_LATENT), bf),
                   jax.ShapeDtypeStruct((n, KV_LATENT), bf),
                   jax.ShapeDtypeStruct((IDX_HEADS, n, IDX_DIM), bf),
                   jax.ShapeDtypeStruct((n, IDX_DIM), bf),
                   jax.ShapeDtypeStruct((n, IDX_HEADS), jnp.float32),
                   jax.ShapeDtypeStruct((n, R_HEADS * R_QK_DIM), bf),
                   jax.ShapeDtypeStruct((n, R_HEADS * R_QK_DIM), bf),
                   jax.ShapeDtypeStruct((n, R_HEADS * R_V_DIM), bf),
                   jax.ShapeDtypeStruct((n, R_HEADS * R_V_DIM), bf)),
        grid=(n // tm,),
        in_specs=[pl.BlockSpec((tm, D_MODEL), row),
                  pl.BlockSpec((D_MODEL, C_END), const),
                  pl.BlockSpec((1, KV_LATENT), const),
                  pl.BlockSpec((tm, LANES), pos),
                  pl.BlockSpec((tm, LANES), pos)],
        out_specs=(pl.BlockSpec((A_HEADS, tm, KV_LATENT), head),
                   pl.BlockSpec((tm, KV_LATENT), row),
                   pl.BlockSpec((IDX_HEADS, tm, IDX_DIM), head),
                   pl.BlockSpec((tm, IDX_DIM), row),
                   pl.BlockSpec((tm, IDX_HEADS), row),
                   pl.BlockSpec((tm, R_HEADS * R_QK_DIM), row),
                   pl.BlockSpec((tm, R_HEADS * R_QK_DIM), row),
                   pl.BlockSpec((tm, R_HEADS * R_V_DIM), row),
                   pl.BlockSpec((tm, R_HEADS * R_V_DIM), row)),
        compiler_params=_cparams(("parallel",)),
        name="in_proj",
    )(xb, w_cat, kv_g, cos_t, sin_t)


def _fold_rows(x, op):
    out = x[:SUBLANES]
    for r in range(1, x.shape[0] // SUBLANES):
        out = op(out, x[r * SUBLANES:(r + 1) * SUBLANES])
    return out


def _attn_kernel(qlat_ref, qi_ref, wit_ref, ckv_ref, ckvt_ref, ki_ref, o_ref, key_scr, lg_scr, acc_scr,
                 *, n_sel, t_pad):
    i = pl.program_id(1)
    nk = ((i + 1) * Q_BLOCK + KEY_CHUNK - 1) // KEY_CHUNK
    tile = (KEY_CHUNK, Q_BLOCK)
    cols_all = A_HEADS * Q_BLOCK
    q_all = qlat_ref[...].reshape(cols_all, KV_LATENT)
    qi_all = qi_ref[...].reshape(IDX_HEADS * Q_BLOCK, IDX_DIM)
    wt = wit_ref[...]
    k_off = lax.broadcasted_iota(jnp.int32, tile, 0)
    q_off = lax.broadcasted_iota(jnp.int32, tile, 1)

    def key_rows(kt):
        return pl.ds(pl.multiple_of(kt * KEY_CHUNK, KEY_CHUNK), KEY_CHUNK)

    def key_pos(kt):
        return kt * KEY_CHUNK + k_off

    def index_tile(kt, carry):
        qk = lax.dot_general(ki_ref[key_rows(kt), :], qi_all, _NT, preferred_element_type=jnp.float32)
        score = wt[0:1] * jnp.maximum(qk[:, :Q_BLOCK], 0.0)
        for h in range(1, IDX_HEADS):
            score = score + wt[h:h + 1] * jnp.maximum(qk[:, h * Q_BLOCK:(h + 1) * Q_BLOCK], 0.0)
        bits = lax.bitcast_convert_type(score, jnp.int32)
        key = bits ^ ((bits >> 31) & jnp.int32(0x7FFFFFFF))
        key = jnp.where(score == 0.0, 0, key)
        causal = key_pos(kt) <= (i * Q_BLOCK + q_off)
        key_scr[kt] = jnp.where(causal, key, INT_MIN)
        return carry

    lax.fori_loop(0, nk, index_tile, 0)

    def count(pred):
        def body(kt, acc):
            return acc + jnp.where(pred(key_scr[kt], kt), 1.0, 0.0)
        acc = lax.fori_loop(0, nk, body, jnp.zeros(tile, jnp.float32))
        return jnp.sum(acc, axis=0, keepdims=True)

    def thr_step(it, t):
        cand = t ^ lax.shift_left(jnp.int32(1), 31 - it)
        return jnp.where(count(lambda k, kt: k >= cand) >= n_sel, cand, t)

    t = lax.fori_loop(0, 32, thr_step, jnp.full((1, Q_BLOCK), INT_MIN, jnp.int32))

    need = n_sel - count(lambda k, kt: k > t)
    n_eq = count(lambda k, kt: k == t)
    idx_bits = max(1, (t_pad - 1).bit_length())
    surplus = jnp.max(jnp.where((t > INT_MIN) & (n_eq > need), 1.0, 0.0)) > 0.0

    def tie_search():
        def tie_step(it, c):
            cand = c | lax.shift_left(jnp.int32(1), idx_bits - 1 - it)
            below = count(lambda k, kt: (k == t) & (key_pos(kt) < cand))
            return jnp.where(below <= need, cand, c)
        return lax.fori_loop(0, idx_bits, tie_step, jnp.zeros((1, Q_BLOCK), jnp.int32))

    c = lax.cond(surplus, tie_search, lambda: jnp.full((1, Q_BLOCK), 1 << idx_bits, jnp.int32))

    def logit_tile(kt, m8):
        key = key_scr[kt]
        sel = ((key > t) | ((key == t) & (key_pos(kt) < c))) & (key != INT_MIN)
        lg = lax.dot_general(ckv_ref[key_rows(kt), :], q_all, _NT, preferred_element_type=jnp.float32)
        lg = jnp.where(jnp.concatenate([sel] * A_HEADS, axis=1), lg, NEG_BIG)
        lg_scr[kt] = lg
        return jnp.maximum(m8, _fold_rows(lg, jnp.maximum))

    m8 = lax.fori_loop(0, nk, logit_tile, jnp.full((SUBLANES, cols_all), NEG_BIG, jnp.float32))
    m = jnp.max(m8, axis=0, keepdims=True)
    acc_scr[...] = jnp.zeros_like(acc_scr)

    def pv_tile(kt, l8):
        p = jnp.exp(lg_scr[kt] - m)
        acc_scr[...] += jnp.dot(ckvt_ref[kt], p.astype(ckvt_ref.dtype), preferred_element_type=jnp.float32)
        return l8 + _fold_rows(p, jnp.add)

    l8 = lax.fori_loop(0, nk, pv_tile, jnp.zeros((SUBLANES, cols_all), jnp.float32))
    o_t = acc_scr[...] / jnp.sum(l8, axis=0, keepdims=True)
    for h in range(A_HEADS):
        o_ref[h] = o_t[:, h * Q_BLOCK:(h + 1) * Q_BLOCK].T.astype(o_ref.dtype)


def _attention(qlat, qi, wi, ckv, ki, batch, t_pad, n_sel):
    n = ckv.shape[0]
    nq = t_pad // Q_BLOCK
    nkc = -(-t_pad // KEY_CHUNK)
    t_keys = nkc * KEY_CHUNK

    def pad_keys(a):
        a = a.reshape(batch, t_pad, a.shape[-1])
        return jnp.pad(a, ((0, 0), (0, t_keys - t_pad), (0, 0))).reshape(batch * t_keys, a.shape[-1])

    ckv_p, ki_p = pad_keys(ckv), pad_keys(ki)
    ckv_t = jnp.swapaxes(ckv_p.reshape(batch * nkc, KEY_CHUNK, KV_LATENT), 1, 2)
    qmap = lambda b, i: (0, b * nq + i, 0)
    cols_all = A_HEADS * Q_BLOCK
    scratch = [pltpu.VMEM((nkc, KEY_CHUNK, Q_BLOCK), jnp.int32),
               pltpu.VMEM((nkc, KEY_CHUNK, cols_all), jnp.float32),
               pltpu.VMEM((KV_LATENT, cols_all), jnp.float32)]
    return pl.pallas_call(
        functools.partial(_attn_kernel, n_sel=n_sel, t_pad=t_keys),
        out_shape=jax.ShapeDtypeStruct((A_HEADS, n, KV_LATENT), jnp.bfloat16),
        grid=(batch, nq),
        in_specs=[pl.BlockSpec((A_HEADS, Q_BLOCK, KV_LATENT), qmap),
                  pl.BlockSpec((IDX_HEADS, Q_BLOCK, IDX_DIM), qmap),
                  pl.BlockSpec((IDX_HEADS, Q_BLOCK), lambda b, i: (0, b * nq + i)),
                  pl.BlockSpec((t_keys, KV_LATENT), lambda b, i: (b, 0)),
                  pl.BlockSpec((nkc, KV_LATENT, KEY_CHUNK), lambda b, i: (b, 0, 0)),
                  pl.BlockSpec((t_keys, IDX_DIM), lambda b, i: (b, 0))],
        out_specs=pl.BlockSpec((A_HEADS, Q_BLOCK, KV_LATENT), qmap),
        scratch_shapes=scratch,
        compiler_params=_cparams(("parallel", "parallel")),
        name="sparse_attention",
    )(qlat, qi, wi.T, ckv_p, ckv_t, ki_p)


def _retention_tables():
    log_g = np.log1p(-np.exp(np.linspace(math.log(1.0 / 32), math.log(1.0 / 512), R_HEADS))).astype(np.float32)
    idx = np.arange(R_CHUNK, dtype=np.float32)
    diff = idx[:, None] - idx[None, :]
    decay = np.where(diff[None] >= 0, np.exp(diff[None] * log_g[:, None, None]), 0.0).astype(np.float32)
    q_decay = np.exp((idx + 1.0)[None, :] * log_g[:, None])[:, :, None].astype(np.float32)
    k_decay = np.exp((R_CHUNK - 1.0 - idx)[None, :] * log_g[:, None])[:, :, None].astype(np.float32)
    chunk_decay = np.exp(R_CHUNK * log_g).astype(np.float32)
    return decay, q_decay, k_decay, chunk_decay


def _retention_kernel(q_ref, k_ref, v_ref, sg_ref, rw_ref, dec_ref, qd_ref, kd_ref, o_ref, state_ref,
                      *, t_pad, chunk_decay):
    state_ref[...] = jnp.zeros_like(state_ref)
    qk_w = R_HEADS * R_QK_DIM
    lane = lax.broadcasted_iota(jnp.int32, (1, qk_w), 1)
    half_w = R_QK_DIM // 2
    head_mask = [((lane % (qk_w // 2)) // half_w) == h for h in range(R_HEADS)]

    def chunk(c, carry):
        r0 = pl.multiple_of(c * R_CHUNK, R_CHUNK)
        rows = pl.ds(r0, R_CHUNK)
        qc = q_ref[rows, :]
        kc = k_ref[rows, :]
        for h in range(R_HEADS):
            cols = slice(h * R_V_DIM, (h + 1) * R_V_DIM)
            qm = jnp.where(head_mask[h], qc, jnp.zeros_like(qc))
            km = jnp.where(head_mask[h], kc, jnp.zeros_like(kc))
            vh = v_ref[rows, cols]
            inner = lax.dot_general(qm, kc, _NT, preferred_element_type=jnp.float32) * dec_ref[h]
            st = state_ref[h]
            out = (jnp.dot(inner.astype(vh.dtype), vh, preferred_element_type=jnp.float32)
                   + jnp.dot(qm, st.astype(qm.dtype), preferred_element_type=jnp.float32) * qd_ref[h])
            kdec = (km.astype(jnp.float32) * kd_ref[h]).astype(km.dtype)
            state_ref[h] = chunk_decay[h] * st + lax.dot_general(kdec, vh, _TN,
                                                                  preferred_element_type=jnp.float32)
            mu = jnp.mean(out, axis=-1, keepdims=True)
            d = out - mu
            var = jnp.mean(d * d, axis=-1, keepdims=True)
            normed = d * lax.rsqrt(var + EPS) * rw_ref[:, cols]
            o_ref[rows, cols] = (sg_ref[rows, cols].astype(jnp.float32) * normed).astype(o_ref.dtype)
        return carry

    lax.fori_loop(0, t_pad // R_CHUNK, chunk, 0)


def _retention(qr, kr, vr, sg, ret_w, batch, t_pad):
    n = qr.shape[0]
    decay, q_decay, k_decay, chunk_decay = _retention_tables()
    qk_w = R_HEADS * R_QK_DIM
    v_w = R_HEADS * R_V_DIM
    per_b = lambda b: (b, 0)
    c2 = lambda b: (0, 0)
    c3 = lambda b: (0, 0, 0)
    return pl.pallas_call(
        functools.partial(_retention_kernel, t_pad=t_pad, chunk_decay=[float(v) for v in chunk_decay]),
        out_shape=jax.ShapeDtypeStruct((n, v_w), jnp.bfloat16),
        grid=(batch,),
        in_specs=[pl.BlockSpec((t_pad, qk_w), per_b),
                  pl.BlockSpec((t_pad, qk_w), per_b),
                  pl.BlockSpec((t_pad, v_w), per_b),
                  pl.BlockSpec((t_pad, v_w), per_b),
                  pl.BlockSpec((1, v_w), c2),
                  pl.BlockSpec((R_HEADS, R_CHUNK, R_CHUNK), c3),
                  pl.BlockSpec((R_HEADS, R_CHUNK, 1), c3),
                  pl.BlockSpec((R_HEADS, R_CHUNK, 1), c3)],
        out_specs=pl.BlockSpec((t_pad, v_w), per_b),
        scratch_shapes=[pltpu.VMEM((R_HEADS, qk_w, R_V_DIM), jnp.float32)],
        compiler_params=_cparams(("parallel",)),
        name="retention",
    )(qr, kr, vr, sg, ret_w, jnp.asarray(decay), jnp.asarray(q_decay), jnp.asarray(k_decay))


def _layer_norm(y, g, b):
    mu = jnp.mean(y, axis=-1, keepdims=True)
    d = y - mu
    var = jnp.mean(d * d, axis=-1, keepdims=True)
    return d * lax.rsqrt(var + EPS) * g + b


def _mix_router_kernel(olat_ref, ob_ref, h_ref, wuvo_ref, wob_ref, g_ref, b_ref, wrt_ref, rb_ref, tri_ref,
                       h1_ref, h1b_ref, x3_ref, gate_ref, idx_ref, rin_ref, cnt_ref, tstart_ref, tcnt_ref, run_ref,
                       *, tm):
    mix = jnp.dot(ob_ref[...], wob_ref[...], preferred_element_type=jnp.float32)
    for h in range(A_HEADS):
        mix = mix + jnp.dot(olat_ref[h], wuvo_ref[h], preferred_element_type=jnp.float32)
    h1 = _layer_norm(DN_ALPHA * h_ref[...] + mix, g_ref[...], b_ref[...])
    h1_ref[...] = h1
    h1b_ref[...] = h1.astype(h1b_ref.dtype)
    for s in range(SLABS):
        x3_ref[pl.ds(s, tm, stride=SLABS), :] = h1[:, s * LANES:(s + 1) * LANES]

    logits = lax.dot_general(wrt_ref[...], h1, _NT, preferred_element_type=jnp.float32,
                             precision=lax.Precision.HIGHEST)
    scores = 1.0 / (1.0 + jnp.exp(-logits))
    sel = scores + rb_ref[...]
    neg = -jnp.inf
    iota_g = lax.broadcasted_iota(jnp.int32, (GROUP_SIZE, tm), 0)
    iota_n = lax.broadcasted_iota(jnp.int32, (N_GROUPS, tm), 0)

    def first_argmax(v, iota, big):
        m = jnp.max(v, axis=0, keepdims=True)
        return m, jnp.min(jnp.where(v == m, iota, big), axis=0, keepdims=True)

    grp_score = jnp.zeros((N_GROUPS, tm), jnp.float32)
    for g in range(N_GROUPS):
        blk = sel[g * GROUP_SIZE:(g + 1) * GROUP_SIZE]
        m1, i1 = first_argmax(blk, iota_g, GROUP_SIZE)
        m2 = jnp.max(jnp.where(iota_g == i1, neg, blk), axis=0, keepdims=True)
        grp_score = jnp.where(iota_n == g, m1 + m2, grp_score)

    grp_on = jnp.zeros((N_GROUPS, tm), jnp.float32)
    work = grp_score
    for _ in range(TOP_GROUPS):
        _, gi = first_argmax(work, iota_n, N_GROUPS)
        hit = iota_n == gi
        grp_on = jnp.where(hit, 1.0, grp_on)
        work = jnp.where(hit, neg, work)

    masked = jnp.concatenate(
        [jnp.where(grp_on[g:g + 1] > 0.0, sel[g * GROUP_SIZE:(g + 1) * GROUP_SIZE], neg)
         for g in range(N_GROUPS)], axis=0)
    iota_e = lax.broadcasted_iota(jnp.int32, (N_EXPERTS, tm), 0)
    iota_k = lax.broadcasted_iota(jnp.int32, (TOP_K, tm), 0)
    top_idx = jnp.zeros((TOP_K, tm), jnp.int32)
    top_gate = jnp.zeros((TOP_K, tm), jnp.float32)
    hits = []
    for k in range(TOP_K):
        _, ei = first_argmax(masked, iota_e, N_EXPERTS)
        hit = iota_e == ei
        hits.append(hit)
        gk = jnp.sum(jnp.where(hit, scores, 0.0), axis=0, keepdims=True)
        masked = jnp.where(hit, neg, masked)
        top_idx = jnp.where(iota_k == k, ei, top_idx)
        top_gate = jnp.where(iota_k == k, gk, top_gate)
    gate_ref[...] = top_gate / jnp.sum(top_gate, axis=0, keepdims=True) * ROUTE_SCALE

    @pl.when(pl.program_id(0) == 0)
    def _():
        run_ref[...] = jnp.zeros_like(run_ref)

    onehot = jnp.zeros((N_EXPERTS, tm), jnp.float32)
    for hit in hits:
        onehot = jnp.where(hit, 1.0, onehot)
    before = jnp.dot(onehot.astype(jnp.bfloat16), tri_ref[...], preferred_element_type=jnp.float32)
    rank = jnp.zeros((TOP_K, tm), jnp.float32)
    for k in range(TOP_K):
        rank = jnp.where(iota_k == k, jnp.sum(jnp.where(hits[k], before, 0.0), axis=0, keepdims=True), rank)
    idx_ref[...] = top_idx
    rin_ref[...] = rank.astype(jnp.int32)

    @pl.when(pl.program_id(0) == 0)
    def _():
        tstart_ref[...] = jnp.zeros_like(tstart_ref)
        tcnt_ref[...] = jnp.zeros_like(tcnt_ref)

    this_tile = lax.broadcasted_iota(jnp.int32, tstart_ref.shape, 1) == pl.program_id(0)
    tile_cnt = jnp.sum(onehot, axis=1, keepdims=True)
    tstart_ref[...] = jnp.where(this_tile, run_ref[...].astype(jnp.int32), tstart_ref[...])
    tcnt_ref[...] = jnp.where(this_tile, tile_cnt.astype(jnp.int32), tcnt_ref[...])
    run_ref[...] += tile_cnt
    cnt_ref[...] = run_ref[...].astype(jnp.int32)


def _mix_router(olat, ob, h, wuvo, wob, ln_g, ln_b, wrt, rbias, tm):
    n = h.shape[0]
    row = lambda i: (i, 0)
    c2 = lambda i: (0, 0)
    c3 = lambda i: (0, 0, 0)
    col = lambda i: (0, i)
    v_w = R_HEADS * R_V_DIM
    tiles_pad = -(-(n // tm) // LANES) * LANES
    tri = jnp.triu(jnp.ones((tm, tm), jnp.bfloat16), k=1)
    return pl.pallas_call(
        functools.partial(_mix_router_kernel, tm=tm),
        out_shape=(jax.ShapeDtypeStruct((n, D_MODEL), jnp.float32),
                   jax.ShapeDtypeStruct((n, D_MODEL), jnp.bfloat16),
                   jax.ShapeDtypeStruct((n * SLABS, LANES), jnp.float32),
                   jax.ShapeDtypeStruct((TOP_K, n), jnp.float32),
                   jax.ShapeDtypeStruct((TOP_K, n), jnp.int32),
                   jax.ShapeDtypeStruct((TOP_K, n), jnp.int32),
                   jax.ShapeDtypeStruct((N_EXPERTS, 1), jnp.int32),
                   jax.ShapeDtypeStruct((N_EXPERTS, tiles_pad), jnp.int32),
                   jax.ShapeDtypeStruct((N_EXPERTS, tiles_pad), jnp.int32)),
        grid=(n // tm,),
        in_specs=[pl.BlockSpec((A_HEADS, tm, KV_LATENT), lambda i: (0, i, 0)),
                  pl.BlockSpec((tm, v_w), row),
                  pl.BlockSpec((tm, D_MODEL), row),
                  pl.BlockSpec((A_HEADS, KV_LATENT, D_MODEL), c3),
                  pl.BlockSpec((v_w, D_MODEL), c2),
                  pl.BlockSpec((1, D_MODEL), c2),
                  pl.BlockSpec((1, D_MODEL), c2),
                  pl.BlockSpec((N_EXPERTS, D_MODEL), c2),
                  pl.BlockSpec((N_EXPERTS, 1), c2),
                  pl.BlockSpec((tm, tm), c2)],
        out_specs=(pl.BlockSpec((tm, D_MODEL), row),
                   pl.BlockSpec((tm, D_MODEL), row),
                   pl.BlockSpec((tm * SLABS, LANES), row),
                   pl.BlockSpec((TOP_K, tm), col),
                   pl.BlockSpec((TOP_K, tm), col),
                   pl.BlockSpec((TOP_K, tm), col),
                   pl.BlockSpec((N_EXPERTS, 1), c2),
                   pl.BlockSpec((N_EXPERTS, tiles_pad), c2),
                   pl.BlockSpec((N_EXPERTS, tiles_pad), c2)),
        scratch_shapes=[pltpu.VMEM((N_EXPERTS, 1), jnp.float32)],
        compiler_params=_cparams(("arbitrary",)),
        name="mix_ln_router",
    )(olat, ob, h, wuvo, wob, ln_g, ln_b, wrt, rbias, tri)


def _row_copy(src, dst, sem):
    return pltpu.make_async_copy(src, dst, sem)


def _segment_offsets(tcnt_ref, off_ref):
    def per_expert(e, off):
        off_ref[e] = off
        return off + (tcnt_ref[0, 0, e] + SEG_ROWS - 1) // SEG_ROWS * SEG_ROWS
    return lax.fori_loop(0, N_EXPERTS, per_expert, 0)


def _for_each_chunk(tcnt_ref, fn):
    def per_expert(e, carry):
        def per_chunk(c, inner):
            fn(e, c)
            return inner
        return lax.fori_loop(0, (tcnt_ref[0, 0, e] + SEG_ROWS - 1) // SEG_ROWS, per_chunk, carry)
    lax.fori_loop(0, N_EXPERTS, per_expert, 0)


def _dispatch_kernel(pad_lo_ref, pad_hi_ref, idx_ref, rin_ref, tstart_ref, tcnt_ref, x3_ref, rows_ref,
                     stage_ref, off_ref, zero_ref, sem, *, td, cap):
    @pl.when(pl.program_id(0) == 0)
    def _():
        stage_ref[...] = jnp.zeros_like(stage_ref)

    _segment_offsets(tcnt_ref, off_ref)

    def place(j, carry):
        row = x3_ref[j]
        for k in range(TOP_K):
            stage_ref[off_ref[idx_ref[k, j]] + rin_ref[k, j]] = row
        return carry

    lax.fori_loop(0, td, place, 0)

    def chunk_copy(e, c):
        return _row_copy(stage_ref.at[pl.ds(off_ref[e] + c * SEG_ROWS, SEG_ROWS)],
                         rows_ref.at[pl.ds(e * cap + tstart_ref[0, 0, e] + c * SEG_ROWS, SEG_ROWS)], sem)

    _for_each_chunk(tcnt_ref, lambda e, c: chunk_copy(e, c).start())
    _for_each_chunk(tcnt_ref, lambda e, c: chunk_copy(e, c).wait())

    @pl.when(pl.program_id(0) == pl.num_programs(0) - 1)
    def _():
        zero_ref[...] = jnp.zeros_like(zero_ref)

        def fill(e, carry):
            def one(r, c):
                _row_copy(zero_ref, rows_ref.at[r], sem).start()
                return c
            return lax.fori_loop(pad_lo_ref[e], pad_hi_ref[e], one, carry)

        def drain(e, carry):
            def one(r, c):
                _row_copy(zero_ref, rows_ref.at[r], sem).wait()
                return c
            return lax.fori_loop(pad_lo_ref[e], pad_hi_ref[e], one, carry)

        lax.fori_loop(0, N_EXPERTS, fill, 0)
        lax.fori_loop(0, N_EXPERTS, drain, 0)


def _stage_rows(td):
    return -(-(td * TOP_K + N_EXPERTS * (SEG_ROWS - 1)) // SEG_ROWS) * SEG_ROWS


def _dispatch(pad_lo, pad_hi, top_idx, rin, tstart, tcnt, x3, n_rows, td, cap):
    n = top_idx.shape[1]
    col = pl.BlockSpec((TOP_K, td), lambda i, *_: (0, i), memory_space=pltpu.SMEM)
    per_tile = pl.BlockSpec((1, 1, N_EXPERTS), lambda i, *_: (i, 0, 0), memory_space=pltpu.SMEM)
    return pl.pallas_call(
        functools.partial(_dispatch_kernel, td=td, cap=cap),
        out_shape=jax.ShapeDtypeStruct((n_rows + ROW_BLOCK, SLABS, LANES), jnp.float32),
        grid_spec=pltpu.PrefetchScalarGridSpec(
            num_scalar_prefetch=2,
            grid=(n // td,),
            in_specs=[col, col, per_tile, per_tile,
                      pl.BlockSpec((td, SLABS, LANES), lambda i, *_: (i, 0, 0))],
            out_specs=pl.BlockSpec(memory_space=pl.ANY),
            scratch_shapes=[pltpu.VMEM((_stage_rows(td), SLABS, LANES), jnp.float32),
                            pltpu.SMEM((N_EXPERTS,), jnp.int32),
                            pltpu.VMEM((SLABS, LANES), jnp.float32),
                            pltpu.SemaphoreType.DMA(())]),
        compiler_params=pltpu.CompilerParams(dimension_semantics=("arbitrary",), has_side_effects=True,
                                             vmem_limit_bytes=VMEM_LIMIT),
        name="moe_dispatch",
    )(pad_lo, pad_hi, top_idx, rin, tstart, tcnt, x3)


def _expert_kernel(be_ref, br_ref, nu_ref, x_ref, wgu_ref, wd_ref, y_ref):
    @pl.when(pl.program_id(0) < nu_ref[0])
    def _():
        pair = 2 * LANES
        gu = jnp.zeros((ROW_BLOCK, 2 * D_EXPERT), jnp.float32)
        for s in range(0, SLABS, 2):
            xs = jnp.concatenate([x_ref[pl.ds(s, ROW_BLOCK, stride=SLABS), :],
                                  x_ref[pl.ds(s + 1, ROW_BLOCK, stride=SLABS), :]], axis=1).astype(jnp.bfloat16)
            gu = gu + jnp.dot(xs, wgu_ref[0, s * LANES:s * LANES + pair, :], preferred_element_type=jnp.float32)
        g, u = gu[:, :D_EXPERT], gu[:, D_EXPERT:]
        hdn = (g / (1.0 + jnp.exp(-g)) * u).astype(jnp.bfloat16)
        y = jnp.dot(hdn, wd_ref[0], preferred_element_type=jnp.float32)
        for s in range(SLABS):
            y_ref[pl.ds(s, ROW_BLOCK, stride=SLABS), :] = y[:, s * LANES:(s + 1) * LANES]


def _experts(block_expert, block_row, n_used, x_rows2, wgu, wd):
    n_blocks = block_expert.shape[0]
    blk = lambda i, be, br, nu: (br[jnp.minimum(i, nu[0] - 1)], 0)
    wsel = lambda i, be, br, nu: (be[jnp.minimum(i, nu[0] - 1)], 0, 0)
    return pl.pallas_call(
        _expert_kernel,
        out_shape=jax.ShapeDtypeStruct(x_rows2.shape, jnp.float32),
        grid_spec=pltpu.PrefetchScalarGridSpec(
            num_scalar_prefetch=3,
            grid=(n_blocks,),
            in_specs=[pl.BlockSpec((ROW_BLOCK * SLABS, LANES), blk),
                      pl.BlockSpec((1, D_MODEL, 2 * D_EXPERT), wsel),
                      pl.BlockSpec((1, D_EXPERT, D_MODEL), wsel)],
            out_specs=pl.BlockSpec((ROW_BLOCK * SLABS, LANES), blk)),
        compiler_params=_cparams(("arbitrary",)),
        name="moe_experts",
    )(block_expert, block_row, n_used, x_rows2, wgu, wd)


def _combine_kernel(idx_ref, rin_ref, gate_ref, tstart_ref, tcnt_ref, y3_ref, h1_ref, h1b_ref,
                    wsg_ref, wsu_ref, wsd_ref, g_ref, b_ref, h2_ref, h2b_ref, buf_ref, off_ref, comb_ref, sem,
                    *, tc, cap):
    _segment_offsets(tcnt_ref, off_ref)

    def chunk_copy(e, c):
        return _row_copy(y3_ref.at[pl.ds(e * cap + tstart_ref[0, 0, e] + c * SEG_ROWS, SEG_ROWS)],
                         buf_ref.at[pl.ds(off_ref[e] + c * SEG_ROWS, SEG_ROWS)], sem)

    _for_each_chunk(tcnt_ref, lambda e, c: chunk_copy(e, c).start())

    xb = h1b_ref[...]
    gs = jnp.dot(xb, wsg_ref[...], preferred_element_type=jnp.float32)
    us = jnp.dot(xb, wsu_ref[...], preferred_element_type=jnp.float32)
    hs = (gs / (1.0 + jnp.exp(-gs)) * us).astype(jnp.bfloat16)
    shared = jnp.dot(hs, wsd_ref[...], preferred_element_type=jnp.float32)

    _for_each_chunk(tcnt_ref, lambda e, c: chunk_copy(e, c).wait())

    def weigh(j, carry):
        acc = gate_ref[0, j] * buf_ref[off_ref[idx_ref[0, j]] + rin_ref[0, j]]
        for k in range(1, TOP_K):
            acc = acc + gate_ref[k, j] * buf_ref[off_ref[idx_ref[k, j]] + rin_ref[k, j]]
        comb_ref[pl.ds(pl.multiple_of(j * SLABS, SLABS), SLABS), :] = acc
        return carry

    lax.fori_loop(0, tc, weigh, 0)
    routed = jnp.concatenate([comb_ref[pl.ds(s, tc, stride=SLABS), :] for s in range(SLABS)], axis=1)
    h2 = _layer_norm(DN_ALPHA * h1_ref[...] + (routed + shared), g_ref[...], b_ref[...])
    h2_ref[...] = h2
    h2b_ref[...] = h2.astype(h2b_ref.dtype)


def _combine(top_idx, rin, gates, tstart, tcnt, y3, h1, h1b, wsg, wsu, wsd, ln_g, ln_b, tc, cap):
    n = h1.shape[0]
    row = lambda i: (i, 0)
    c2 = lambda i: (0, 0)
    smem_col = pl.BlockSpec((TOP_K, tc), lambda i: (0, i), memory_space=pltpu.SMEM)
    per_tile = pl.BlockSpec((1, 1, N_EXPERTS), lambda i: (i, 0, 0), memory_space=pltpu.SMEM)
    return pl.pallas_call(
        functools.partial(_combine_kernel, tc=tc, cap=cap),
        out_shape=(jax.ShapeDtypeStruct((n, D_MODEL), jnp.float32),
                   jax.ShapeDtypeStruct((n, D_MODEL), jnp.bfloat16)),
        grid=(n // tc,),
        in_specs=[smem_col, smem_col, smem_col, per_tile, per_tile,
                  pl.BlockSpec(memory_space=pl.ANY),
                  pl.BlockSpec((tc, D_MODEL), row),
                  pl.BlockSpec((tc, D_MODEL), row),
                  pl.BlockSpec((D_MODEL, D_SHARED), c2),
                  pl.BlockSpec((D_MODEL, D_SHARED), c2),
                  pl.BlockSpec((D_SHARED, D_MODEL), c2),
                  pl.BlockSpec((1, D_MODEL), c2),
                  pl.BlockSpec((1, D_MODEL), c2)],
        out_specs=(pl.BlockSpec((tc, D_MODEL), row),
                   pl.BlockSpec((tc, D_MODEL), row)),
        scratch_shapes=[pltpu.VMEM((_stage_rows(tc), SLABS, LANES), jnp.float32),
                        pltpu.SMEM((N_EXPERTS,), jnp.int32),
                        pltpu.VMEM((tc * SLABS, LANES), jnp.float32),
                        pltpu.SemaphoreType.DMA(())],
        compiler_params=_cparams(("arbitrary",)),
        name="moe_combine",
    )(top_idx, rin, gates, tstart, tcnt, y3, h1, h1b, wsg, wsu, wsd, ln_g, ln_b)


def _routing_plan(counts, n_blocks, cap):
    blocks_per_e = (counts + ROW_BLOCK - 1) // ROW_BLOCK
    blk_end = jnp.cumsum(blocks_per_e)
    blk_start = blk_end - blocks_per_e
    block_ids = jnp.arange(n_blocks, dtype=jnp.int32)
    block_expert = jnp.minimum(jnp.sum((blk_end[None, :] <= block_ids[:, None]).astype(jnp.int32), axis=1),
                               N_EXPERTS - 1)
    start_of = jnp.sum(jnp.where(block_expert[:, None] == jnp.arange(N_EXPERTS, dtype=jnp.int32)[None, :],
                                 blk_start[None, :], 0), axis=1)
    block_row = block_expert * (cap // ROW_BLOCK) + (block_ids - start_of)
    expert_row0 = jnp.arange(N_EXPERTS, dtype=jnp.int32) * cap
    pad_lo = expert_row0 + counts
    pad_hi = expert_row0 + blocks_per_e * ROW_BLOCK
    return block_expert, block_row.astype(jnp.int32), blk_end[-1:], pad_lo, pad_hi


def _prepare_weights(w_in, w_uk, w_uv, w_o):
    L = w_in.shape[0]
    o = IN_OFFS
    bf = jnp.bfloat16
    w_qa = w_in[:, :, o[0]:o[1]].reshape(L, D_MODEL, A_HEADS, A_HEAD_DIM).transpose(0, 2, 1, 3)
    w_qlat = _fold(w_qa, w_uk, A_HEAD_DIM ** -0.5)
    w_qlat = w_qlat.transpose(0, 2, 1, 3).reshape(L, D_MODEL, A_HEADS * KV_LATENT)
    w_oa = w_o[:, :A_HEADS * A_HEAD_DIM].reshape(L, A_HEADS, A_HEAD_DIM, D_MODEL)
    w_uvo = _fold(w_uv, w_oa, 1.0)
    w_ob = w_o[:, A_HEADS * A_HEAD_DIM:].astype(bf)

    def pad_cols(w, width):
        return jnp.pad(w, ((0, 0), (0, 0), (0, width - w.shape[-1])))

    half = R_QK_DIM // 2
    perm = np.concatenate([np.arange(half) + R_QK_DIM * h for h in range(R_HEADS)]
                          + [np.arange(half) + R_QK_DIM * h + half for h in range(R_HEADS)])
    w_cat = jnp.concatenate([
        w_qlat,
        w_in[:, :, o[1]:o[2]].astype(bf),
        w_in[:, :, o[2]:o[3]].astype(bf),
        pad_cols(w_in[:, :, o[3]:o[4]], LANES).astype(bf),
        pad_cols(w_in[:, :, o[4]:o[5]], LANES).astype(bf),
        w_in[:, :, o[5]:o[6]][:, :, perm].astype(bf),
        w_in[:, :, o[6]:o[7]][:, :, perm].astype(bf),
        w_in[:, :, o[7]:o[8]].astype(bf),
        w_in[:, :, o[8]:o[9]].astype(bf)], axis=-1)
    return w_cat, w_uvo, w_ob


def _rotary_tables(t_pad):
    half = R_QK_DIM // 2
    inv = ROPE_BASE ** (-jnp.arange(half, dtype=jnp.float32) / half)
    ang = jnp.arange(t_pad, dtype=jnp.float32)[:, None] * inv
    return jnp.tile(jnp.cos(ang), (1, R_HEADS)), jnp.tile(jnp.sin(ang), (1, R_HEADS))


def _pick_tile(n, prefer):
    for t in prefer:
        if n % t == 0:
            return t
    raise ValueError(f"no tile for {n}")


def kernel(x, meta_tokens, w_in, w_uk, w_uv, kv_norm_w, ret_norm_w, w_o, ln1_g, ln1_b, w_router, router_bias,
           w_gate, w_up, w_down, ws_gate, ws_up, ws_down, ln2_g, ln2_b):
    b, s, d = x.shape
    assert d == D_MODEL
    L = w_in.shape[0]
    t = s + N_META
    n_sel = min(TOPK_MAX, s // 4)
    t_pad = -(-t // LANES) * LANES
    n = b * t_pad
    bf = jnp.bfloat16

    meta = jnp.broadcast_to(meta_tokens.astype(x.dtype)[None], (b, N_META, d))
    h = jnp.concatenate([meta, x, jnp.zeros((b, t_pad - t, d), x.dtype)], axis=1).reshape(n, d)
    hb = h.astype(bf)

    w_cat, w_uvo, w_ob = _prepare_weights(w_in, w_uk, w_uv, w_o)
    cos_t, sin_t = _rotary_tables(t_pad)
    wgu = jnp.concatenate([w_gate.astype(bf), w_up.astype(bf)], axis=-1)
    wd = w_down.astype(bf)
    wsg, wsu, wsd = ws_gate.astype(bf), ws_up.astype(bf), ws_down.astype(bf)
    w_rt = jnp.swapaxes(w_router, 1, 2)

    tm = _pick_tile(n, (MOE_TILE, LANES))
    n_tiles = n // tm
    n_blocks = -(-(n * TOP_K) // ROW_BLOCK) + N_EXPERTS
    cap = (-(-n // ROW_BLOCK) + 1) * ROW_BLOCK
    n_rows = N_EXPERTS * cap

    for l in range(L):
        qlat, ckv, qi, ki, wi, qr, kr, vr, sg = _inproj(hb, w_cat[l], kv_norm_w[l][None], cos_t, sin_t, t_pad)
        olat = _attention(qlat, qi, wi, ckv, ki, b, t_pad, n_sel)
        ob = _retention(qr, kr, vr, sg, ret_norm_w[l][None], b, t_pad)
        h1, h1b, x3, gates, top_idx, rin, counts, tstart, tcnt = _mix_router(
            olat, ob, h, w_uvo[l], w_ob[l], ln1_g[l][None], ln1_b[l][None], w_rt[l], router_bias[l][:, None], tm)
        tstart = tstart[:, :n_tiles].T[:, None, :]
        tcnt = tcnt[:, :n_tiles].T[:, None, :]
        block_expert, block_row, n_used, pad_lo, pad_hi = _routing_plan(counts[:, 0], n_blocks, cap)
        x_rows = _dispatch(pad_lo, pad_hi, top_idx, rin, tstart, tcnt, x3.reshape(n, SLABS, LANES), n_rows, tm, cap)
        n_all = x_rows.shape[0]
        y_rows = _experts(block_expert, block_row, n_used, x_rows.reshape(n_all * SLABS, LANES), wgu[l], wd[l])
        h, hb = _combine(top_idx, rin, gates, tstart, tcnt, y_rows.reshape(n_all, SLABS, LANES), h1, h1b,
                         wsg[l], wsu[l], wsd[l], ln2_g[l][None], ln2_b[l][None], tm, cap)
    return h.reshape(b, t_pad, d)[:, N_META:t]
```

```python
import functools
import math

import numpy as np
import jax
import jax.numpy as jnp
from jax import lax
from jax.experimental import pallas as pl
from jax.experimental.pallas import tpu as pltpu

D_MODEL = 1024
N_META = 16
A_HEADS = 8
A_HEAD_DIM = 64
KV_LATENT = 128
IDX_HEADS = 8
IDX_DIM = 64
TOPK_MAX = 256
R_HEADS = 4
R_QK_DIM = 64
R_V_DIM = 128
ROPE_BASE = 10000.0
N_EXPERTS = 64
N_GROUPS = 8
GROUP_SIZE = N_EXPERTS // N_GROUPS
TOP_GROUPS = 4
TOP_K = 8
D_EXPERT = 256
D_SHARED = 256
ROUTE_SCALE = 2.5
DEPTH = 4
DN_ALPHA = (2 * DEPTH) ** 0.25
EPS = 1e-6

IN_COLS = (A_HEADS * A_HEAD_DIM, KV_LATENT, IDX_HEADS * IDX_DIM, IDX_DIM, IDX_HEADS,
           R_HEADS * R_QK_DIM, R_HEADS * R_QK_DIM, R_HEADS * R_V_DIM, R_HEADS * R_V_DIM)
IN_OFFS = tuple(int(v) for v in np.cumsum((0,) + IN_COLS))

LANES = 128
SUBLANES = 8
Q_BLOCK = 128
KEY_CHUNK = 256
R_CHUNK = 128
ROW_BLOCK = 256
MOE_TILE = 512
SEG_ROWS = 32
SLABS = D_MODEL // LANES
VMEM_LIMIT = 56 * 1024 * 1024

C_QLAT = 0
C_CKV = C_QLAT + A_HEADS * KV_LATENT
C_QI = C_CKV + KV_LATENT
C_KI = C_QI + IDX_HEADS * IDX_DIM
C_WI = C_KI + LANES
C_QR = C_WI + LANES
C_KR = C_QR + R_HEADS * R_QK_DIM
C_VR = C_KR + R_HEADS * R_QK_DIM
C_GR = C_VR + R_HEADS * R_V_DIM
C_END = C_GR + R_HEADS * R_V_DIM

INT_MIN = -2 ** 31
NEG_BIG = -1e30

_NT = (((1,), (1,)), ((), ()))
_TN = (((0,), (0,)), ((), ()))


def _cparams(sem):
    return pltpu.CompilerParams(dimension_semantics=sem, vmem_limit_bytes=VMEM_LIMIT)


def _fold_kernel(a_ref, b_ref, o_ref, *, scale):
    o_ref[0, 0] = (jnp.dot(a_ref[0, 0], b_ref[0, 0], preferred_element_type=jnp.float32,
                           precision=lax.Precision.HIGHEST) * scale).astype(o_ref.dtype)


def _fold(a, b, scale):
    L, H, M, K = a.shape
    N = b.shape[-1]
    return pl.pallas_call(
        functools.partial(_fold_kernel, scale=scale),
        out_shape=jax.ShapeDtypeStruct((L, H, M, N), jnp.bfloat16),
        grid=(L, H),
        in_specs=[pl.BlockSpec((1, 1, M, K), lambda l, h: (l, h, 0, 0)),
                  pl.BlockSpec((1, 1, K, N), lambda l, h: (l, h, 0, 0))],
        out_specs=pl.BlockSpec((1, 1, M, N), lambda l, h: (l, h, 0, 0)),
        compiler_params=_cparams(("parallel", "parallel")),
        name="weight_fold",
    )(a, b)


def _inproj_kernel(x_ref, w_ref, kvg_ref, cos_ref, sin_ref,
                   qlat_ref, ckv_ref, qi_ref, ki_ref, wi_ref, qr_ref, kr_ref, vr_ref, sg_ref):
    x = x_ref[...]

    def proj(lo, hi):
        return jnp.dot(x, w_ref[:, lo:hi], preferred_element_type=jnp.float32)

    r = proj(C_QLAT, C_CKV)
    for h in range(A_HEADS):
        qlat_ref[h] = r[:, h * KV_LATENT:(h + 1) * KV_LATENT].astype(qlat_ref.dtype)

    r = proj(C_CKV, C_QI)
    r = r * lax.rsqrt(jnp.mean(r * r, axis=-1, keepdims=True) + EPS) * kvg_ref[...]
    ckv_ref[...] = r.astype(ckv_ref.dtype)

    r = proj(C_QI, C_KI)
    for h in range(IDX_HEADS):
        qi_ref[h] = r[:, h * IDX_DIM:(h + 1) * IDX_DIM].astype(qi_ref.dtype)

    r = proj(C_KI, C_WI)
    ki_ref[...] = r[:, :IDX_DIM].astype(ki_ref.dtype)
    r = proj(C_WI, C_QR)
    wi_ref[...] = r[:, :IDX_HEADS] * (IDX_HEADS ** -0.5)

    cos = cos_ref[...]
    sin = sin_ref[...]
    half = R_HEADS * R_QK_DIM // 2

    def rot(r, out_ref, scale):
        x1, x2 = r[:, :half], r[:, half:]
        out_ref[:, :half] = ((x1 * cos - x2 * sin) * scale).astype(out_ref.dtype)
        out_ref[:, half:] = ((x1 * sin + x2 * cos) * scale).astype(out_ref.dtype)

    rot(proj(C_QR, C_KR), qr_ref, 1.0)
    rot(proj(C_KR, C_VR), kr_ref, R_QK_DIM ** -0.5)
    vr_ref[...] = proj(C_VR, C_GR).astype(vr_ref.dtype)
    g = proj(C_GR, C_END)
    sg_ref[...] = (g / (1.0 + jnp.exp(-g))).astype(sg_ref.dtype)


def _inproj(xb, w_cat, kv_g, cos_t, sin_t, t_pad):
    n = xb.shape[0]
    tm = t_pad // 4
    per_b = t_pad // tm
    bf = jnp.bfloat16
    row = lambda i: (i, 0)
    head = lambda i: (0, i, 0)
    const = lambda i: (0, 0)
    pos = lambda i: (i % per_b, 0)
    return pl.pallas_call(
        _inproj_kernel,
        out_shape=(jax.ShapeDtypeStruct((A_HEADS, n, KV_LATENT), bf),
                   jax.ShapeDtypeStruct((n, KV_LATENT), bf),
                   jax.ShapeDtypeStruct((IDX_HEADS, n, IDX_DIM), bf),
                   jax.ShapeDtypeStruct((n, IDX_DIM), bf),
                   jax.ShapeDtypeStruct((n, IDX_HEADS), jnp.float32),
                   jax.ShapeDtypeStruct((n, R_HEADS * R_QK_DIM), bf),
                   jax.ShapeDtypeStruct((n, R_HEADS * R_QK_DIM), bf),
                   jax.ShapeDtypeStruct((n, R_HEADS * R_V_DIM), bf),
                   jax.ShapeDtypeStruct((n, R_HEADS * R_V_DIM), bf)),
        grid=(n // tm,),
        in_specs=[pl.BlockSpec((tm, D_MODEL), row),
                  pl.BlockSpec((D_MODEL, C_END), const),
                  pl.BlockSpec((1, KV_LATENT), const),
                  pl.BlockSpec((tm, LANES), pos),
                  pl.BlockSpec((tm, LANES), pos)],
        out_specs=(pl.BlockSpec((A_HEADS, tm, KV_LATENT), head),
                   pl.BlockSpec((tm, KV_LATENT), row),
                   pl.BlockSpec((IDX_HEADS, tm, IDX_DIM), head),
                   pl.BlockSpec((tm, IDX_DIM), row),
                   pl.BlockSpec((tm, IDX_HEADS), row),
                   pl.BlockSpec((tm, R_HEADS * R_QK_DIM), row),
                   pl.BlockSpec((tm, R_HEADS * R_QK_DIM), row),
                   pl.BlockSpec((tm, R_HEADS * R_V_DIM), row),
                   pl.BlockSpec((tm, R_HEADS * R_V_DIM), row)),
        compiler_params=_cparams(("parallel",)),
        name="in_proj",
    )(xb, w_cat, kv_g, cos_t, sin_t)


def _fold_rows(x, op):
    out = x[:SUBLANES]
    for r in range(1, x.shape[0] // SUBLANES):
        out = op(out, x[r * SUBLANES:(r + 1) * SUBLANES])
    return out


def _attn_kernel(qlat_ref, qi_ref, wit_ref, ckv_ref, ckvt_ref, ki_ref, o_ref, key_scr, lg_scr, acc_scr,
                 *, n_sel, t_pad):
    i = pl.program_id(1)
    nk = ((i + 1) * Q_BLOCK + KEY_CHUNK - 1) // KEY_CHUNK
    tile = (KEY_CHUNK, Q_BLOCK)
    cols_all = A_HEADS * Q_BLOCK
    q_all = qlat_ref[...].reshape(cols_all, KV_LATENT)
    qi_all = qi_ref[...].reshape(IDX_HEADS * Q_BLOCK, IDX_DIM)
    wt = wit_ref[...]
    k_off = lax.broadcasted_iota(jnp.int32, tile, 0)
    q_off = lax.broadcasted_iota(jnp.int32, tile, 1)

    def key_rows(kt):
        return pl.ds(pl.multiple_of(kt * KEY_CHUNK, KEY_CHUNK), KEY_CHUNK)

    def key_pos(kt):
        return kt * KEY_CHUNK + k_off

    def index_tile(kt, carry):
        qk = lax.dot_general(ki_ref[key_rows(kt), :], qi_all, _NT, preferred_element_type=jnp.float32)
        score = wt[0:1] * jnp.maximum(qk[:, :Q_BLOCK], 0.0)
        for h in range(1, IDX_HEADS):
            score = score + wt[h:h + 1] * jnp.maximum(qk[:, h * Q_BLOCK:(h + 1) * Q_BLOCK], 0.0)
        bits = lax.bitcast_convert_type(score, jnp.int32)
        key = bits ^ ((bits >> 31) & jnp.int32(0x7FFFFFFF))
        key = jnp.where(score == 0.0, 0, key)
        causal = key_pos(kt) <= (i * Q_BLOCK + q_off)
        key_scr[kt] = jnp.where(causal, key, INT_MIN)
        return carry

    lax.fori_loop(0, nk, index_tile, 0)

    def count(pred):
        def body(kt, acc):
            return acc + jnp.where(pred(key_scr[kt], kt), 1.0, 0.0)
        acc = lax.fori_loop(0, nk, body, jnp.zeros(tile, jnp.float32))
        return jnp.sum(acc, axis=0, keepdims=True)

    def thr_step(it, t):
        cand = t ^ lax.shift_left(jnp.int32(1), 31 - it)
        return jnp.where(count(lambda k, kt: k >= cand) >= n_sel, cand, t)

    t = lax.fori_loop(0, 32, thr_step, jnp.full((1, Q_BLOCK), INT_MIN, jnp.int32))

    need = n_sel - count(lambda k, kt: k > t)
    n_eq = count(lambda k, kt: k == t)
    idx_bits = max(1, (t_pad - 1).bit_length())
    surplus = jnp.max(jnp.where((t > INT_MIN) & (n_eq > need), 1.0, 0.0)) > 0.0

    def tie_search():
        def tie_step(it, c):
            cand = c | lax.shift_left(jnp.int32(1), idx_bits - 1 - it)
            below = count(lambda k, kt: (k == t) & (key_pos(kt) < cand))
            return jnp.where(below <= need, cand, c)
        return lax.fori_loop(0, idx_bits, tie_step, jnp.zeros((1, Q_BLOCK), jnp.int32))

    c = lax.cond(surplus, tie_search, lambda: jnp.full((1, Q_BLOCK), 1 << idx_bits, jnp.int32))

    def logit_tile(kt, m8):
        key = key_scr[kt]
        sel = ((key > t) | ((key == t) & (key_pos(kt) < c))) & (key != INT_MIN)
        lg = lax.dot_general(ckv_ref[key_rows(kt), :], q_all, _NT, preferred_element_type=jnp.float32)
        lg = jnp.where(jnp.concatenate([sel] * A_HEADS, axis=1), lg, NEG_BIG)
        lg_scr[kt] = lg
        return jnp.maximum(m8, _fold_rows(lg, jnp.maximum))

    m8 = lax.fori_loop(0, nk, logit_tile, jnp.full((SUBLANES, cols_all), NEG_BIG, jnp.float32))
    m = jnp.max(m8, axis=0, keepdims=True)
    acc_scr[...] = jnp.zeros_like(acc_scr)

    def pv_tile(kt, l8):
        p = jnp.exp(lg_scr[kt] - m)
        acc_scr[...] += jnp.dot(ckvt_ref[kt], p.astype(ckvt_ref.dtype), preferred_element_type=jnp.float32)
        return l8 + _fold_rows(p, jnp.add)

    l8 = lax.fori_loop(0, nk, pv_tile, jnp.zeros((SUBLANES, cols_all), jnp.float32))
    o_t = acc_scr[...] / jnp.sum(l8, axis=0, keepdims=True)
    for h in range(A_HEADS):
        o_ref[h] = o_t[:, h * Q_BLOCK:(h + 1) * Q_BLOCK].T.astype(o_ref.dtype)


def _attention(qlat, qi, wi, ckv, ki, batch, t_pad, n_sel):
    n = ckv.shape[0]
    nq = t_pad // Q_BLOCK
    nkc = -(-t_pad // KEY_CHUNK)
    t_keys = nkc * KEY_CHUNK

    def pad_keys(a):
        a = a.reshape(batch, t_pad, a.shape[-1])
        return jnp.pad(a, ((0, 0), (0, t_keys - t_pad), (0, 0))).reshape(batch * t_keys, a.shape[-1])

    ckv_p, ki_p = pad_keys(ckv), pad_keys(ki)
    ckv_t = jnp.swapaxes(ckv_p.reshape(batch * nkc, KEY_CHUNK, KV_LATENT), 1, 2)
    qmap = lambda b, i: (0, b * nq + i, 0)
    cols_all = A_HEADS * Q_BLOCK
    scratch = [pltpu.VMEM((nkc, KEY_CHUNK, Q_BLOCK), jnp.int32),
               pltpu.VMEM((nkc, KEY_CHUNK, cols_all), jnp.float32),
               pltpu.VMEM((KV_LATENT, cols_all), jnp.float32)]
    return pl.pallas_call(
        functools.partial(_attn_kernel, n_sel=n_sel, t_pad=t_keys),
        out_shape=jax.ShapeDtypeStruct((A_HEADS, n, KV_LATENT), jnp.bfloat16),
        grid=(batch, nq),
        in_specs=[pl.BlockSpec((A_HEADS, Q_BLOCK, KV_LATENT), qmap),
                  pl.BlockSpec((IDX_HEADS, Q_BLOCK, IDX_DIM), qmap),
                  pl.BlockSpec((IDX_HEADS, Q_BLOCK), lambda b, i: (0, b * nq + i)),
                  pl.BlockSpec((t_keys, KV_LATENT), lambda b, i: (b, 0)),
                  pl.BlockSpec((nkc, KV_LATENT, KEY_CHUNK), lambda b, i: (b, 0, 0)),
                  pl.BlockSpec((t_keys, IDX_DIM), lambda b, i: (b, 0))],
        out_specs=pl.BlockSpec((A_HEADS, Q_BLOCK, KV_LATENT), qmap),
        scratch_shapes=scratch,
        compiler_params=_cparams(("parallel", "parallel")),
        name="sparse_attention",
    )(qlat, qi, wi.T, ckv_p, ckv_t, ki_p)


def _retention_tables():
    log_g = np.log1p(-np.exp(np.linspace(math.log(1.0 / 32), math.log(1.0 / 512), R_HEADS))).astype(np.float32)
    idx = np.arange(R_CHUNK, dtype=np.float32)
    diff = idx[:, None] - idx[None, :]
    decay = np.where(diff[None] >= 0, np.exp(diff[None] * log_g[:, None, None]), 0.0).astype(np.float32)
    q_decay = np.exp((idx + 1.0)[None, :] * log_g[:, None])[:, :, None].astype(np.float32)
    k_decay = np.exp((R_CHUNK - 1.0 - idx)[None, :] * log_g[:, None])[:, :, None].astype(np.float32)
    chunk_decay = np.exp(R_CHUNK * log_g).astype(np.float32)
    return decay, q_decay, k_decay, chunk_decay


def _retention_kernel(q_ref, k_ref, v_ref, sg_ref, rw_ref, dec_ref, qd_ref, kd_ref, o_ref, state_ref,
                      *, t_pad, chunk_decay):
    state_ref[...] = jnp.zeros_like(state_ref)
    qk_w = R_HEADS * R_QK_DIM
    lane = lax.broadcasted_iota(jnp.int32, (1, qk_w), 1)
    half_w = R_QK_DIM // 2
    head_mask = [((lane % (qk_w // 2)) // half_w) == h for h in range(R_HEADS)]

    def chunk(c, carry):
        r0 = pl.multiple_of(c * R_CHUNK, R_CHUNK)
        rows = pl.ds(r0, R_CHUNK)
        qc = q_ref[rows, :]
        kc = k_ref[rows, :]
        for h in range(R_HEADS):
            cols = slice(h * R_V_DIM, (h + 1) * R_V_DIM)
            qm = jnp.where(head_mask[h], qc, jnp.zeros_like(qc))
            km = jnp.where(head_mask[h], kc, jnp.zeros_like(kc))
            vh = v_ref[rows, cols]
            inner = lax.dot_general(qm, kc, _NT, preferred_element_type=jnp.float32) * dec_ref[h]
            st = state_ref[h]
            out = (jnp.dot(inner.astype(vh.dtype), vh, preferred_element_type=jnp.float32)
                   + jnp.dot(qm, st.astype(qm.dtype), preferred_element_type=jnp.float32) * qd_ref[h])
            kdec = (km.astype(jnp.float32) * kd_ref[h]).astype(km.dtype)
            state_ref[h] = chunk_decay[h] * st + lax.dot_general(kdec, vh, _TN,
                                                                  preferred_element_type=jnp.float32)
            mu = jnp.mean(out, axis=-1, keepdims=True)
            d = out - mu
            var = jnp.mean(d * d, axis=-1, keepdims=True)
            normed = d * lax.rsqrt(var + EPS) * rw_ref[:, cols]
            o_ref[rows, cols] = (sg_ref[rows, cols].astype(jnp.float32) * normed).astype(o_ref.dtype)
        return carry

    lax.fori_loop(0, t_pad // R_CHUNK, chunk, 0)


def _retention(qr, kr, vr, sg, ret_w, batch, t_pad):
    n = qr.shape[0]
    decay, q_decay, k_decay, chunk_decay = _retention_tables()
    qk_w = R_HEADS * R_QK_DIM
    v_w = R_HEADS * R_V_DIM
    per_b = lambda b: (b, 0)
    c2 = lambda b: (0, 0)
    c3 = lambda b: (0, 0, 0)
    return pl.pallas_call(
        functools.partial(_retention_kernel, t_pad=t_pad, chunk_decay=[float(v) for v in chunk_decay]),
        out_shape=jax.ShapeDtypeStruct((n, v_w), jnp.bfloat16),
        grid=(batch,),
        in_specs=[pl.BlockSpec((t_pad, qk_w), per_b),
                  pl.BlockSpec((t_pad, qk_w), per_b),
                  pl.BlockSpec((t_pad, v_w), per_b),
                  pl.BlockSpec((t_pad, v_w), per_b),
                  pl.BlockSpec((1, v_w), c2),
                  pl.BlockSpec((R_HEADS, R_CHUNK, R_CHUNK), c3),
                  pl.BlockSpec((R_HEADS, R_CHUNK, 1), c3),
                  pl.BlockSpec((R_HEADS, R_CHUNK, 1), c3)],
        out_specs=pl.BlockSpec((t_pad, v_w), per_b),
        scratch_shapes=[pltpu.VMEM((R_HEADS, qk_w, R_V_DIM), jnp.float32)],
        compiler_params=_cparams(("parallel",)),
        name="retention",
    )(qr, kr, vr, sg, ret_w, jnp.asarray(decay), jnp.asarray(q_decay), jnp.asarray(k_decay))


def _layer_norm(y, g, b):
    mu = jnp.mean(y, axis=-1, keepdims=True)
    d = y - mu
    var = jnp.mean(d * d, axis=-1, keepdims=True)
    return d * lax.rsqrt(var + EPS) * g + b


def _mix_router_kernel(olat_ref, ob_ref, h_ref, wuvo_ref, wob_ref, g_ref, b_ref, wrt_ref, rb_ref, tri_ref, etri_ref,
                       h1_ref, h1b_ref, x3_ref, gate_ref, loc_ref, cnt_ref, tstart_ref, tcnt_ref, run_ref,
                       *, tm):
    mix = jnp.dot(ob_ref[...], wob_ref[...], preferred_element_type=jnp.float32)
    for h in range(A_HEADS):
        mix = mix + jnp.dot(olat_ref[h], wuvo_ref[h], preferred_element_type=jnp.float32)
    h1 = _layer_norm(DN_ALPHA * h_ref[...] + mix, g_ref[...], b_ref[...])
    h1_ref[...] = h1
    h1b_ref[...] = h1.astype(h1b_ref.dtype)
    for s in range(SLABS):
        x3_ref[pl.ds(s, tm, stride=SLABS), :] = h1[:, s * LANES:(s + 1) * LANES]

    logits = lax.dot_general(wrt_ref[...], h1, _NT, preferred_element_type=jnp.float32,
                             precision=lax.Precision.HIGHEST)
    scores = 1.0 / (1.0 + jnp.exp(-logits))
    sel = scores + rb_ref[...]
    neg = -jnp.inf
    iota_g = lax.broadcasted_iota(jnp.int32, (GROUP_SIZE, tm), 0)
    iota_n = lax.broadcasted_iota(jnp.int32, (N_GROUPS, tm), 0)

    def first_argmax(v, iota, big):
        m = jnp.max(v, axis=0, keepdims=True)
        return m, jnp.min(jnp.where(v == m, iota, big), axis=0, keepdims=True)

    grp_score = jnp.zeros((N_GROUPS, tm), jnp.float32)
    for g in range(N_GROUPS):
        blk = sel[g * GROUP_SIZE:(g + 1) * GROUP_SIZE]
        m1, i1 = first_argmax(blk, iota_g, GROUP_SIZE)
        m2 = jnp.max(jnp.where(iota_g == i1, neg, blk), axis=0, keepdims=True)
        grp_score = jnp.where(iota_n == g, m1 + m2, grp_score)

    grp_on = jnp.zeros((N_GROUPS, tm), jnp.float32)
    work = grp_score
    for _ in range(TOP_GROUPS):
        _, gi = first_argmax(work, iota_n, N_GROUPS)
        hit = iota_n == gi
        grp_on = jnp.where(hit, 1.0, grp_on)
        work = jnp.where(hit, neg, work)

    masked = jnp.concatenate(
        [jnp.where(grp_on[g:g + 1] > 0.0, sel[g * GROUP_SIZE:(g + 1) * GROUP_SIZE], neg)
         for g in range(N_GROUPS)], axis=0)
    iota_e = lax.broadcasted_iota(jnp.int32, (N_EXPERTS, tm), 0)
    iota_k = lax.broadcasted_iota(jnp.int32, (TOP_K, tm), 0)
    top_idx = jnp.zeros((TOP_K, tm), jnp.int32)
    top_gate = jnp.zeros((TOP_K, tm), jnp.float32)
    hits = []
    for k in range(TOP_K):
        _, ei = first_argmax(masked, iota_e, N_EXPERTS)
        hit = iota_e == ei
        hits.append(hit)
        gk = jnp.sum(jnp.where(hit, scores, 0.0), axis=0, keepdims=True)
        masked = jnp.where(hit, neg, masked)
        top_idx = jnp.where(iota_k == k, ei, top_idx)
        top_gate = jnp.where(iota_k == k, gk, top_gate)
    gate_ref[...] = top_gate / jnp.sum(top_gate, axis=0, keepdims=True) * ROUTE_SCALE

    @pl.when(pl.program_id(0) == 0)
    def _():
        run_ref[...] = jnp.zeros_like(run_ref)

    onehot = jnp.zeros((N_EXPERTS, tm), jnp.float32)
    for hit in hits:
        onehot = jnp.where(hit, 1.0, onehot)
    before = jnp.dot(onehot.astype(jnp.bfloat16), tri_ref[...], preferred_element_type=jnp.float32)
    tile_cnt = jnp.sum(onehot, axis=1, keepdims=True)
    seg_chunks = jnp.floor((tile_cnt + (SEG_ROWS - 1)) * (1.0 / SEG_ROWS))
    seg_off = SEG_ROWS * jnp.dot(etri_ref[...], jnp.broadcast_to(seg_chunks, (N_EXPERTS, tm)).astype(jnp.bfloat16),
                                 preferred_element_type=jnp.float32)
    slot = before + seg_off
    loc = jnp.zeros((TOP_K, tm), jnp.float32)
    for k in range(TOP_K):
        loc = jnp.where(iota_k == k, jnp.sum(jnp.where(hits[k], slot, 0.0), axis=0, keepdims=True), loc)
    loc_ref[...] = loc.astype(jnp.int32)

    @pl.when(pl.program_id(0) == 0)
    def _():
        tstart_ref[...] = jnp.zeros_like(tstart_ref)
        tcnt_ref[...] = jnp.zeros_like(tcnt_ref)

    this_tile = lax.broadcasted_iota(jnp.int32, tstart_ref.shape, 1) == pl.program_id(0)
    tstart_ref[...] = jnp.where(this_tile, run_ref[...].astype(jnp.int32), tstart_ref[...])
    tcnt_ref[...] = jnp.where(this_tile, tile_cnt.astype(jnp.int32), tcnt_ref[...])
    run_ref[...] += tile_cnt
    cnt_ref[...] = run_ref[...].astype(jnp.int32)


def _mix_router(olat, ob, h, wuvo, wob, ln_g, ln_b, wrt, rbias, tm):
    n = h.shape[0]
    row = lambda i: (i, 0)
    c2 = lambda i: (0, 0)
    c3 = lambda i: (0, 0, 0)
    col = lambda i: (0, i)
    v_w = R_HEADS * R_V_DIM
    tiles_pad = -(-(n // tm) // LANES) * LANES
    tri = jnp.triu(jnp.ones((tm, tm), jnp.bfloat16), k=1)
    etri = jnp.tril(jnp.ones((N_EXPERTS, N_EXPERTS), jnp.bfloat16), k=-1)
    return pl.pallas_call(
        functools.partial(_mix_router_kernel, tm=tm),
        out_shape=(jax.ShapeDtypeStruct((n, D_MODEL), jnp.float32),
                   jax.ShapeDtypeStruct((n, D_MODEL), jnp.bfloat16),
                   jax.ShapeDtypeStruct((n * SLABS, LANES), jnp.float32),
                   jax.ShapeDtypeStruct((TOP_K, n), jnp.float32),
                   jax.ShapeDtypeStruct((TOP_K, n), jnp.int32),
                   jax.ShapeDtypeStruct((N_EXPERTS, 1), jnp.int32),
                   jax.ShapeDtypeStruct((N_EXPERTS, tiles_pad), jnp.int32),
                   jax.ShapeDtypeStruct((N_EXPERTS, tiles_pad), jnp.int32)),
        grid=(n // tm,),
        in_specs=[pl.BlockSpec((A_HEADS, tm, KV_LATENT), lambda i: (0, i, 0)),
                  pl.BlockSpec((tm, v_w), row),
                  pl.BlockSpec((tm, D_MODEL), row),
                  pl.BlockSpec((A_HEADS, KV_LATENT, D_MODEL), c3),
                  pl.BlockSpec((v_w, D_MODEL), c2),
                  pl.BlockSpec((1, D_MODEL), c2),
                  pl.BlockSpec((1, D_MODEL), c2),
                  pl.BlockSpec((N_EXPERTS, D_MODEL), c2),
                  pl.BlockSpec((N_EXPERTS, 1), c2),
                  pl.BlockSpec((tm, tm), c2),
                  pl.BlockSpec((N_EXPERTS, N_EXPERTS), c2)],
        out_specs=(pl.BlockSpec((tm, D_MODEL), row),
                   pl.BlockSpec((tm, D_MODEL), row),
                   pl.BlockSpec((tm * SLABS, LANES), row),
                   pl.BlockSpec((TOP_K, tm), col),
                   pl.BlockSpec((TOP_K, tm), col),
                   pl.BlockSpec((N_EXPERTS, 1), c2),
                   pl.BlockSpec((N_EXPERTS, tiles_pad), c2),
                   pl.BlockSpec((N_EXPERTS, tiles_pad), c2)),
        scratch_shapes=[pltpu.VMEM((N_EXPERTS, 1), jnp.float32)],
        compiler_params=_cparams(("arbitrary",)),
        name="mix_ln_router",
    )(olat, ob, h, wuvo, wob, ln_g, ln_b, wrt, rbias, tri, etri)


def _row_copy(src, dst, sem):
    return pltpu.make_async_copy(src, dst, sem)


def _chunk_tables(tstart, tcnt, cap, max_chunks):
    chunks = (tcnt + SEG_ROWS - 1) // SEG_ROWS
    cum = jnp.cumsum(chunks, axis=1)
    first = cum - chunks
    c = jnp.arange(max_chunks, dtype=jnp.int32)
    expert = jnp.minimum(jnp.sum((cum[:, None, :] <= c[None, :, None]).astype(jnp.int32), axis=2), N_EXPERTS - 1)
    pick = lambda a: jnp.take_along_axis(a, expert, axis=1)
    local = c[None, :] - pick(first)
    stage_row = (pick(first) + local) * SEG_ROWS
    buffer_row = expert * cap + pick(tstart) + local * SEG_ROWS
    as_smem = lambda a: a.astype(jnp.int32)[:, None, :]
    return cum[:, -1].astype(jnp.int32), as_smem(stage_row), as_smem(buffer_row)


def _dispatch_kernel(nchunk_ref, pad_lo_ref, pad_hi_ref, loc_ref, cstage_ref, crow_ref, x3_ref, rows_ref,
                     stage_ref, zero_ref, sem, *, td):
    @pl.when(pl.program_id(0) == 0)
    def _():
        stage_ref[...] = jnp.zeros_like(stage_ref)

    def place(j, carry):
        row = x3_ref[j]
        for k in range(TOP_K):
            stage_ref[loc_ref[k, j]] = row
        return carry

    lax.fori_loop(0, td, place, 0)
    n_chunks = nchunk_ref[pl.program_id(0)]

    def send(c, carry):
        _row_copy(stage_ref.at[pl.ds(cstage_ref[0, 0, c], SEG_ROWS)],
                  rows_ref.at[pl.ds(crow_ref[0, 0, c], SEG_ROWS)], sem).start()
        return carry

    def drain_chunk(c, carry):
        _row_copy(stage_ref.at[pl.ds(0, SEG_ROWS)], rows_ref.at[pl.ds(0, SEG_ROWS)], sem).wait()
        return carry

    lax.fori_loop(0, n_chunks, send, 0)
    lax.fori_loop(0, n_chunks, drain_chunk, 0)

    @pl.when(pl.program_id(0) == pl.num_programs(0) - 1)
    def _():
        zero_ref[...] = jnp.zeros_like(zero_ref)

        def fill(e, carry):
            def one(r, c):
                _row_copy(zero_ref, rows_ref.at[r], sem).start()
                return c
            return lax.fori_loop(pad_lo_ref[e], pad_hi_ref[e], one, carry)

        def drain(e, carry):
            def one(r, c):
                _row_copy(zero_ref, rows_ref.at[r], sem).wait()
                return c
            return lax.fori_loop(pad_lo_ref[e], pad_hi_ref[e], one, carry)

        lax.fori_loop(0, N_EXPERTS, fill, 0)
        lax.fori_loop(0, N_EXPERTS, drain, 0)


def _stage_rows(td):
    return -(-(td * TOP_K + N_EXPERTS * (SEG_ROWS - 1)) // SEG_ROWS) * SEG_ROWS


def _max_chunks(td):
    return td * TOP_K // SEG_ROWS + N_EXPERTS


def _dispatch(n_chunks, pad_lo, pad_hi, loc, chunk_stage, chunk_row, x3, n_rows, td):
    n = loc.shape[1]
    col = pl.BlockSpec((TOP_K, td), lambda i, *_: (0, i), memory_space=pltpu.SMEM)
    per_tile = pl.BlockSpec((1, 1, _max_chunks(td)), lambda i, *_: (i, 0, 0), memory_space=pltpu.SMEM)
    return pl.pallas_call(
        functools.partial(_dispatch_kernel, td=td),
        out_shape=jax.ShapeDtypeStruct((n_rows, SLABS, LANES), jnp.float32),
        grid_spec=pltpu.PrefetchScalarGridSpec(
            num_scalar_prefetch=3,
            grid=(n // td,),
            in_specs=[col, per_tile, per_tile,
                      pl.BlockSpec((td, SLABS, LANES), lambda i, *_: (i, 0, 0))],
            out_specs=pl.BlockSpec(memory_space=pl.ANY),
            scratch_shapes=[pltpu.VMEM((_stage_rows(td), SLABS, LANES), jnp.float32),
                            pltpu.VMEM((SLABS, LANES), jnp.float32),
                            pltpu.SemaphoreType.DMA(())]),
        compiler_params=pltpu.CompilerParams(dimension_semantics=("arbitrary",), has_side_effects=True,
                                             vmem_limit_bytes=VMEM_LIMIT),
        name="moe_dispatch",
    )(n_chunks, pad_lo, pad_hi, loc, chunk_stage, chunk_row, x3)


def _expert_kernel(be_ref, br_ref, nu_ref, x_ref, wgu_ref, wd_ref, y_ref):
    @pl.when(pl.program_id(0) < nu_ref[0])
    def _():
        pair = 2 * LANES
        gu = jnp.zeros((ROW_BLOCK, 2 * D_EXPERT), jnp.float32)
        for s in range(0, SLABS, 2):
            xs = jnp.concatenate([x_ref[pl.ds(s, ROW_BLOCK, stride=SLABS), :],
                                  x_ref[pl.ds(s + 1, ROW_BLOCK, stride=SLABS), :]], axis=1).astype(jnp.bfloat16)
            gu = gu + jnp.dot(xs, wgu_ref[0, s * LANES:s * LANES + pair, :], preferred_element_type=jnp.float32)
        g, u = gu[:, :D_EXPERT], gu[:, D_EXPERT:]
        hdn = (g / (1.0 + jnp.exp(-g)) * u).astype(jnp.bfloat16)
        y = jnp.dot(hdn, wd_ref[0], preferred_element_type=jnp.float32)
        for s in range(SLABS):
            y_ref[pl.ds(s, ROW_BLOCK, stride=SLABS), :] = y[:, s * LANES:(s + 1) * LANES]


def _experts(block_expert, block_row, n_used, x_rows2, wgu, wd):
    n_blocks = block_expert.shape[0]
    blk = lambda i, be, br, nu: (br[jnp.minimum(i, nu[0] - 1)], 0)
    wsel = lambda i, be, br, nu: (be[jnp.minimum(i, nu[0] - 1)], 0, 0)
    return pl.pallas_call(
        _expert_kernel,
        out_shape=jax.ShapeDtypeStruct(x_rows2.shape, jnp.float32),
        grid_spec=pltpu.PrefetchScalarGridSpec(
            num_scalar_prefetch=3,
            grid=(n_blocks,),
            in_specs=[pl.BlockSpec((ROW_BLOCK * SLABS, LANES), blk),
                      pl.BlockSpec((1, D_MODEL, 2 * D_EXPERT), wsel),
                      pl.BlockSpec((1, D_EXPERT, D_MODEL), wsel)],
            out_specs=pl.BlockSpec((ROW_BLOCK * SLABS, LANES), blk)),
        compiler_params=_cparams(("arbitrary",)),
        name="moe_experts",
    )(block_expert, block_row, n_used, x_rows2, wgu, wd)


def _combine_kernel(nchunk_ref, loc_ref, gate_ref, cstage_ref, crow_ref, y3_ref, h1_ref, h1b_ref,
                    wsg_ref, wsu_ref, wsd_ref, g_ref, b_ref, h2_ref, h2b_ref, buf_ref, comb_ref, sem, *, tc):
    n_chunks = nchunk_ref[pl.program_id(0)]

    def fetch(c, carry):
        _row_copy(y3_ref.at[pl.ds(crow_ref[0, 0, c], SEG_ROWS)],
                  buf_ref.at[pl.ds(cstage_ref[0, 0, c], SEG_ROWS)], sem).start()
        return carry

    def drain_chunk(c, carry):
        _row_copy(y3_ref.at[pl.ds(0, SEG_ROWS)], buf_ref.at[pl.ds(0, SEG_ROWS)], sem).wait()
        return carry

    lax.fori_loop(0, n_chunks, fetch, 0)

    xb = h1b_ref[...]
    gs = jnp.dot(xb, wsg_ref[...], preferred_element_type=jnp.float32)
    us = jnp.dot(xb, wsu_ref[...], preferred_element_type=jnp.float32)
    hs = (gs / (1.0 + jnp.exp(-gs)) * us).astype(jnp.bfloat16)
    shared = jnp.dot(hs, wsd_ref[...], preferred_element_type=jnp.float32)

    lax.fori_loop(0, n_chunks, drain_chunk, 0)

    def weigh(j, carry):
        acc = gate_ref[0, j] * buf_ref[loc_ref[0, j]]
        for k in range(1, TOP_K):
            acc = acc + gate_ref[k, j] * buf_ref[loc_ref[k, j]]
        comb_ref[pl.ds(pl.multiple_of(j * SLABS, SLABS), SLABS), :] = acc
        return carry

    lax.fori_loop(0, tc, weigh, 0)
    routed = jnp.concatenate([comb_ref[pl.ds(s, tc, stride=SLABS), :] for s in range(SLABS)], axis=1)
    h2 = _layer_norm(DN_ALPHA * h1_ref[...] + (routed + shared), g_ref[...], b_ref[...])
    h2_ref[...] = h2
    h2b_ref[...] = h2.astype(h2b_ref.dtype)


def _combine(n_chunks, loc, gates, chunk_stage, chunk_row, y3, h1, h1b, wsg, wsu, wsd, ln_g, ln_b, tc):
    n = h1.shape[0]
    row = lambda i, *_: (i, 0)
    c2 = lambda i, *_: (0, 0)
    smem_col = pl.BlockSpec((TOP_K, tc), lambda i, *_: (0, i), memory_space=pltpu.SMEM)
    per_tile = pl.BlockSpec((1, 1, _max_chunks(tc)), lambda i, *_: (i, 0, 0), memory_space=pltpu.SMEM)
    return pl.pallas_call(
        functools.partial(_combine_kernel, tc=tc),
        out_shape=(jax.ShapeDtypeStruct((n, D_MODEL), jnp.float32),
                   jax.ShapeDtypeStruct((n, D_MODEL), jnp.bfloat16)),
        grid_spec=pltpu.PrefetchScalarGridSpec(
            num_scalar_prefetch=1,
            grid=(n // tc,),
            in_specs=[smem_col, smem_col, per_tile, per_tile,
                      pl.BlockSpec(memory_space=pl.ANY),
                      pl.BlockSpec((tc, D_MODEL), row),
                      pl.BlockSpec((tc, D_MODEL), row),
                      pl.BlockSpec((D_MODEL, D_SHARED), c2),
                      pl.BlockSpec((D_MODEL, D_SHARED), c2),
                      pl.BlockSpec((D_SHARED, D_MODEL), c2),
                      pl.BlockSpec((1, D_MODEL), c2),
                      pl.BlockSpec((1, D_MODEL), c2)],
            out_specs=(pl.BlockSpec((tc, D_MODEL), row),
                       pl.BlockSpec((tc, D_MODEL), row)),
            scratch_shapes=[pltpu.VMEM((_stage_rows(tc), SLABS, LANES), jnp.float32),
                            pltpu.VMEM((tc * SLABS, LANES), jnp.float32),
                            pltpu.SemaphoreType.DMA(())]),
        compiler_params=_cparams(("arbitrary",)),
        name="moe_combine",
    )(n_chunks, loc, gates, chunk_stage, chunk_row, y3, h1, h1b, wsg, wsu, wsd, ln_g, ln_b)


def _routing_plan(counts, n_blocks, cap):
    blocks_per_e = (counts + ROW_BLOCK - 1) // ROW_BLOCK
    blk_end = jnp.cumsum(blocks_per_e)
    blk_start = blk_end - blocks_per_e
    block_ids = jnp.arange(n_blocks, dtype=jnp.int32)
    block_expert = jnp.minimum(jnp.sum((blk_end[None, :] <= block_ids[:, None]).astype(jnp.int32), axis=1),
                               N_EXPERTS - 1)
    start_of = jnp.sum(jnp.where(block_expert[:, None] == jnp.arange(N_EXPERTS, dtype=jnp.int32)[None, :],
                                 blk_start[None, :], 0), axis=1)
    block_row = block_expert * (cap // ROW_BLOCK) + (block_ids - start_of)
    expert_row0 = jnp.arange(N_EXPERTS, dtype=jnp.int32) * cap
    pad_lo = expert_row0 + counts
    pad_hi = expert_row0 + blocks_per_e * ROW_BLOCK
    return block_expert, block_row.astype(jnp.int32), blk_end[-1:], pad_lo, pad_hi


def _prepare_weights(w_in, w_uk, w_uv, w_o):
    L = w_in.shape[0]
    o = IN_OFFS
    bf = jnp.bfloat16
    w_qa = w_in[:, :, o[0]:o[1]].reshape(L, D_MODEL, A_HEADS, A_HEAD_DIM).transpose(0, 2, 1, 3)
    w_qlat = _fold(w_qa, w_uk, A_HEAD_DIM ** -0.5)
    w_qlat = w_qlat.transpose(0, 2, 1, 3).reshape(L, D_MODEL, A_HEADS * KV_LATENT)
    w_oa = w_o[:, :A_HEADS * A_HEAD_DIM].reshape(L, A_HEADS, A_HEAD_DIM, D_MODEL)
    w_uvo = _fold(w_uv, w_oa, 1.0)
    w_ob = w_o[:, A_HEADS * A_HEAD_DIM:].astype(bf)

    def pad_cols(w, width):
        return jnp.pad(w, ((0, 0), (0, 0), (0, width - w.shape[-1])))

    half = R_QK_DIM // 2
    perm = np.concatenate([np.arange(half) + R_QK_DIM * h for h in range(R_HEADS)]
                          + [np.arange(half) + R_QK_DIM * h + half for h in range(R_HEADS)])
    w_cat = jnp.concatenate([
        w_qlat,
        w_in[:, :, o[1]:o[2]].astype(bf),
        w_in[:, :, o[2]:o[3]].astype(bf),
        pad_cols(w_in[:, :, o[3]:o[4]], LANES).astype(bf),
        pad_cols(w_in[:, :, o[4]:o[5]], LANES).astype(bf),
        w_in[:, :, o[5]:o[6]][:, :, perm].astype(bf),
        w_in[:, :, o[6]:o[7]][:, :, perm].astype(bf),
        w_in[:, :, o[7]:o[8]].astype(bf),
        w_in[:, :, o[8]:o[9]].astype(bf)], axis=-1)
    return w_cat, w_uvo, w_ob


def _rotary_tables(t_pad):
    half = R_QK_DIM // 2
    inv = ROPE_BASE ** (-jnp.arange(half, dtype=jnp.float32) / half)
    ang = jnp.arange(t_pad, dtype=jnp.float32)[:, None] * inv
    return jnp.tile(jnp.cos(ang), (1, R_HEADS)), jnp.tile(jnp.sin(ang), (1, R_HEADS))


def _pick_tile(n, prefer):
    for t in prefer:
        if n % t == 0:
            return t
    raise ValueError(f"no tile for {n}")


def kernel(x, meta_tokens, w_in, w_uk, w_uv, kv_norm_w, ret_norm_w, w_o, ln1_g, ln1_b, w_router, router_bias,
           w_gate, w_up, w_down, ws_gate, ws_up, ws_down, ln2_g, ln2_b):
    b, s, d = x.shape
    assert d == D_MODEL
    L = w_in.shape[0]
    t = s + N_META
    n_sel = min(TOPK_MAX, s // 4)
    t_pad = -(-t // LANES) * LANES
    n = b * t_pad
    bf = jnp.bfloat16

    meta = jnp.broadcast_to(meta_tokens.astype(x.dtype)[None], (b, N_META, d))
    h = jnp.concatenate([meta, x, jnp.zeros((b, t_pad - t, d), x.dtype)], axis=1).reshape(n, d)
    hb = h.astype(bf)

    w_cat, w_uvo, w_ob = _prepare_weights(w_in, w_uk, w_uv, w_o)
    cos_t, sin_t = _rotary_tables(t_pad)
    wgu = jnp.concatenate([w_gate.astype(bf), w_up.astype(bf)], axis=-1)
    wd = w_down.astype(bf)
    wsg, wsu, wsd = ws_gate.astype(bf), ws_up.astype(bf), ws_down.astype(bf)
    w_rt = jnp.swapaxes(w_router, 1, 2)

    tm = _pick_tile(n, (MOE_TILE, 2 * LANES, LANES))
    n_tiles = n // tm
    n_blocks = -(-(n * TOP_K) // ROW_BLOCK) + N_EXPERTS
    cap = (-(-n // ROW_BLOCK) + 1) * ROW_BLOCK
    n_rows = N_EXPERTS * cap

    for l in range(L):
        qlat, ckv, qi, ki, wi, qr, kr, vr, sg = _inproj(hb, w_cat[l], kv_norm_w[l][None], cos_t, sin_t, t_pad)
        olat = _attention(qlat, qi, wi, ckv, ki, b, t_pad, n_sel)
        ob = _retention(qr, kr, vr, sg, ret_norm_w[l][None], b, t_pad)
        h1, h1b, x3, gates, loc, counts, tstart, tcnt = _mix_router(
            olat, ob, h, w_uvo[l], w_ob[l], ln1_g[l][None], ln1_b[l][None], w_rt[l], router_bias[l][:, None], tm)
        n_chunks, chunk_stage, chunk_row = _chunk_tables(tstart[:, :n_tiles].T, tcnt[:, :n_tiles].T, cap,
                                                         _max_chunks(tm))
        block_expert, block_row, n_used, pad_lo, pad_hi = _routing_plan(counts[:, 0], n_blocks, cap)
        x_rows = _dispatch(n_chunks, pad_lo, pad_hi, loc, chunk_stage, chunk_row, x3.reshape(n, SLABS, LANES),
                           n_rows, tm)
        y_rows = _experts(block_expert, block_row, n_used, x_rows.reshape(n_rows * SLABS, LANES), wgu[l], wd[l])
        h, hb = _combine(n_chunks, loc, gates, chunk_stage, chunk_row, y_rows.reshape(n_rows, SLABS, LANES), h1, h1b,
                         wsg[l], wsu[l], wsd[l], ln2_g[l][None], ln2_b[l][None], tm)
    return h.reshape(b, t_pad, d)[:, N_META:t]
```

```python
import functools
import math

import numpy as np
import jax
import jax.numpy as jnp
from jax import lax
from jax.experimental import pallas as pl
from jax.experimental.pallas import tpu as pltpu

D_MODEL = 1024
N_META = 16
A_HEADS = 8
A_HEAD_DIM = 64
KV_LATENT = 128
IDX_HEADS = 8
IDX_DIM = 64
TOPK_MAX = 256
R_HEADS = 4
R_QK_DIM = 64
R_V_DIM = 128
ROPE_BASE = 10000.0
N_EXPERTS = 64
N_GROUPS = 8
GROUP_SIZE = N_EXPERTS // N_GROUPS
TOP_GROUPS = 4
TOP_K = 8
D_EXPERT = 256
D_SHARED = 256
ROUTE_SCALE = 2.5
DEPTH = 4
DN_ALPHA = (2 * DEPTH) ** 0.25
EPS = 1e-6

IN_COLS = (A_HEADS * A_HEAD_DIM, KV_LATENT, IDX_HEADS * IDX_DIM, IDX_DIM, IDX_HEADS,
           R_HEADS * R_QK_DIM, R_HEADS * R_QK_DIM, R_HEADS * R_V_DIM, R_HEADS * R_V_DIM)
IN_OFFS = tuple(int(v) for v in np.cumsum((0,) + IN_COLS))

LANES = 128
SUBLANES = 8
Q_BLOCK = 128
KEY_CHUNK = 256
R_CHUNK = 128
ROW_BLOCK = 256
MOE_TILE = 256
SEG_ROWS = 16
SLABS = D_MODEL // LANES
VMEM_LIMIT = 56 * 1024 * 1024

C_QLAT = 0
C_CKV = C_QLAT + A_HEADS * KV_LATENT
C_QI = C_CKV + KV_LATENT
C_KI = C_QI + IDX_HEADS * IDX_DIM
C_WI = C_KI + LANES
C_QR = C_WI + LANES
C_KR = C_QR + R_HEADS * R_QK_DIM
C_VR = C_KR + R_HEADS * R_QK_DIM
C_GR = C_VR + R_HEADS * R_V_DIM
C_END = C_GR + R_HEADS * R_V_DIM

INT_MIN = -2 ** 31
NEG_BIG = -1e30

_NT = (((1,), (1,)), ((), ()))
_TN = (((0,), (0,)), ((), ()))


def _cparams(sem):
    return pltpu.CompilerParams(dimension_semantics=sem, vmem_limit_bytes=VMEM_LIMIT)


def _fold_kernel(a_ref, b_ref, o_ref, *, scale):
    o_ref[0, 0] = (jnp.dot(a_ref[0, 0], b_ref[0, 0], preferred_element_type=jnp.float32,
                           precision=lax.Precision.HIGHEST) * scale).astype(o_ref.dtype)


def _fold(a, b, scale):
    L, H, M, K = a.shape
    N = b.shape[-1]
    return pl.pallas_call(
        functools.partial(_fold_kernel, scale=scale),
        out_shape=jax.ShapeDtypeStruct((L, H, M, N), jnp.bfloat16),
        grid=(L, H),
        in_specs=[pl.BlockSpec((1, 1, M, K), lambda l, h: (l, h, 0, 0)),
                  pl.BlockSpec((1, 1, K, N), lambda l, h: (l, h, 0, 0))],
        out_specs=pl.BlockSpec((1, 1, M, N), lambda l, h: (l, h, 0, 0)),
        compiler_params=_cparams(("parallel", "parallel")),
        name="weight_fold",
    )(a, b)


def _inproj_kernel(x_ref, w_ref, kvg_ref, cos_ref, sin_ref,
                   qlat_ref, ckv_ref, qi_ref, ki_ref, wi_ref, qr_ref, kr_ref, vr_ref, sg_ref):
    x = x_ref[...]

    def proj(lo, hi):
        return jnp.dot(x, w_ref[:, lo:hi], preferred_element_type=jnp.float32)

    r = proj(C_QLAT, C_CKV)
    for h in range(A_HEADS):
        qlat_ref[h] = r[:, h * KV_LATENT:(h + 1) * KV_LATENT].astype(qlat_ref.dtype)

    r = proj(C_CKV, C_QI)
    r = r * lax.rsqrt(jnp.mean(r * r, axis=-1, keepdims=True) + EPS) * kvg_ref[...]
    ckv_ref[...] = r.astype(ckv_ref.dtype)

    r = proj(C_QI, C_KI)
    for h in range(IDX_HEADS):
        qi_ref[h] = r[:, h * IDX_DIM:(h + 1) * IDX_DIM].astype(qi_ref.dtype)

    r = proj(C_KI, C_WI)
    ki_ref[...] = r[:, :IDX_DIM].astype(ki_ref.dtype)
    r = proj(C_WI, C_QR)
    wi_ref[...] = r[:, :IDX_HEADS] * (IDX_HEADS ** -0.5)

    cos = cos_ref[...]
    sin = sin_ref[...]
    half = R_HEADS * R_QK_DIM // 2

    def rot(r, out_ref, scale):
        x1, x2 = r[:, :half], r[:, half:]
        out_ref[:, :half] = ((x1 * cos - x2 * sin) * scale).astype(out_ref.dtype)
        out_ref[:, half:] = ((x1 * sin + x2 * cos) * scale).astype(out_ref.dtype)

    rot(proj(C_QR, C_KR), qr_ref, 1.0)
    rot(proj(C_KR, C_VR), kr_ref, R_QK_DIM ** -0.5)
    vr_ref[...] = proj(C_VR, C_GR).astype(vr_ref.dtype)
    g = proj(C_GR, C_END)
    sg_ref[...] = (g / (1.0 + jnp.exp(-g))).astype(sg_ref.dtype)


def _inproj(xb, w_cat, kv_g, cos_t, sin_t, t_pad):
    n = xb.shape[0]
    tm = t_pad // 4
    per_b = t_pad // tm
    bf = jnp.bfloat16
    row = lambda i: (i, 0)
    head = lambda i: (0, i, 0)
    const = lambda i: (0, 0)
    pos = lambda i: (i % per_b, 0)
    return pl.pallas_call(
        _inproj_kernel,
        out_shape=(jax.ShapeDtypeStruct((A_HEADS, n, KV_LATENT), bf),
                   jax.ShapeDtypeStruct((n, KV_LATENT), bf),
                   jax.ShapeDtypeStruct((IDX_HEADS, n, IDX_DIM), bf),
                   jax.ShapeDtypeStruct((n, IDX_DIM), bf),
                   jax.ShapeDtypeStruct((n, IDX_HEADS), jnp.float32),
                   jax.ShapeDtypeStruct((n, R_HEADS * R_QK_DIM), bf),
                   jax.ShapeDtypeStruct((n, R_HEADS * R_QK_DIM), bf),
                   jax.ShapeDtypeStruct((n, R_HEADS * R_V_DIM), bf),
                   jax.ShapeDtypeStruct((n, R_HEADS * R_V_DIM), bf)),
        grid=(n // tm,),
        in_specs=[pl.BlockSpec((tm, D_MODEL), row),
                  pl.BlockSpec((D_MODEL, C_END), const),
                  pl.BlockSpec((1, KV_LATENT), const),
                  pl.BlockSpec((tm, LANES), pos),
                  pl.BlockSpec((tm, LANES), pos)],
        out_specs=(pl.BlockSpec((A_HEADS, tm, KV_LATENT), head),
                   pl.BlockSpec((tm, KV_LATENT), row),
                   pl.BlockSpec((IDX_HEADS, tm, IDX_DIM), head),
                   pl.BlockSpec((tm, IDX_DIM), row),
                   pl.BlockSpec((tm, IDX_HEADS), row),
                   pl.BlockSpec((tm, R_HEADS * R_QK_DIM), row),
                   pl.BlockSpec((tm, R_HEADS * R_QK_DIM), row),
                   pl.BlockSpec((tm, R_HEADS * R_V_DIM), row),
                   pl.BlockSpec((tm, R_HEADS * R_V_DIM), row)),
        compiler_params=_cparams(("parallel",)),
        name="in_proj",
    )(xb, w_cat, kv_g, cos_t, sin_t)


def _fold_rows(x, op):
    out = x[:SUBLANES]
    for r in range(1, x.shape[0] // SUBLANES):
        out = op(out, x[r * SUBLANES:(r + 1) * SUBLANES])
    return out


def _attn_kernel(qlat_ref, qi_ref, wit_ref, ckv_ref, ckvt_ref, ki_ref, o_ref, key_scr, lg_scr, acc_scr,
                 *, n_sel, t_pad):
    i = pl.program_id(1)
    nk = ((i + 1) * Q_BLOCK + KEY_CHUNK - 1) // KEY_CHUNK
    tile = (KEY_CHUNK, Q_BLOCK)
    cols_all = A_HEADS * Q_BLOCK
    q_all = qlat_ref[...].reshape(cols_all, KV_LATENT)
    qi_all = qi_ref[...].reshape(IDX_HEADS * Q_BLOCK, IDX_DIM)
    wt = wit_ref[...]
    k_off = lax.broadcasted_iota(jnp.int32, tile, 0)
    q_off = lax.broadcasted_iota(jnp.int32, tile, 1)

    def key_rows(kt):
        return pl.ds(pl.multiple_of(kt * KEY_CHUNK, KEY_CHUNK), KEY_CHUNK)

    def key_pos(kt):
        return kt * KEY_CHUNK + k_off

    def index_tile(kt, carry):
        qk = lax.dot_general(ki_ref[key_rows(kt), :], qi_all, _NT, preferred_element_type=jnp.float32)
        score = wt[0:1] * jnp.maximum(qk[:, :Q_BLOCK], 0.0)
        for h in range(1, IDX_HEADS):
            score = score + wt[h:h + 1] * jnp.maximum(qk[:, h * Q_BLOCK:(h + 1) * Q_BLOCK], 0.0)
        bits = lax.bitcast_convert_type(score, jnp.int32)
        key = bits ^ ((bits >> 31) & jnp.int32(0x7FFFFFFF))
        key = jnp.where(score == 0.0, 0, key)
        causal = key_pos(kt) <= (i * Q_BLOCK + q_off)
        key_scr[kt] = jnp.where(causal, key, INT_MIN)
        return carry

    lax.fori_loop(0, nk, index_tile, 0)

    def count(pred):
        def body(kt, acc):
            return acc + jnp.where(pred(key_scr[kt], kt), 1.0, 0.0)
        acc = lax.fori_loop(0, nk, body, jnp.zeros(tile, jnp.float32))
        return jnp.sum(acc, axis=0, keepdims=True)

    def thr_step(it, t):
        cand = t ^ lax.shift_left(jnp.int32(1), 31 - it)
        return jnp.where(count(lambda k, kt: k >= cand) >= n_sel, cand, t)

    t = lax.fori_loop(0, 32, thr_step, jnp.full((1, Q_BLOCK), INT_MIN, jnp.int32))

    need = n_sel - count(lambda k, kt: k > t)
    n_eq = count(lambda k, kt: k == t)
    idx_bits = max(1, (t_pad - 1).bit_length())
    surplus = jnp.max(jnp.where((t > INT_MIN) & (n_eq > need), 1.0, 0.0)) > 0.0

    def tie_search():
        def tie_step(it, c):
            cand = c | lax.shift_left(jnp.int32(1), idx_bits - 1 - it)
            below = count(lambda k, kt: (k == t) & (key_pos(kt) < cand))
            return jnp.where(below <= need, cand, c)
        return lax.fori_loop(0, idx_bits, tie_step, jnp.zeros((1, Q_BLOCK), jnp.int32))

    c = lax.cond(surplus, tie_search, lambda: jnp.full((1, Q_BLOCK), 1 << idx_bits, jnp.int32))

    def logit_tile(kt, m8):
        key = key_scr[kt]
        sel = ((key > t) | ((key == t) & (key_pos(kt) < c))) & (key != INT_MIN)
        lg = lax.dot_general(ckv_ref[key_rows(kt), :], q_all, _NT, preferred_element_type=jnp.float32)
        lg = jnp.where(jnp.concatenate([sel] * A_HEADS, axis=1), lg, NEG_BIG)
        lg_scr[kt] = lg
        return jnp.maximum(m8, _fold_rows(lg, jnp.maximum))

    m8 = lax.fori_loop(0, nk, logit_tile, jnp.full((SUBLANES, cols_all), NEG_BIG, jnp.float32))
    m = jnp.max(m8, axis=0, keepdims=True)
    acc_scr[...] = jnp.zeros_like(acc_scr)

    def pv_tile(kt, l8):
        p = jnp.exp(lg_scr[kt] - m)
        acc_scr[...] += jnp.dot(ckvt_ref[kt], p.astype(ckvt_ref.dtype), preferred_element_type=jnp.float32)
        return l8 + _fold_rows(p, jnp.add)

    l8 = lax.fori_loop(0, nk, pv_tile, jnp.zeros((SUBLANES, cols_all), jnp.float32))
    o_t = acc_scr[...] / jnp.sum(l8, axis=0, keepdims=True)
    for h in range(A_HEADS):
        o_ref[h] = o_t[:, h * Q_BLOCK:(h + 1) * Q_BLOCK].T.astype(o_ref.dtype)


def _attention(qlat, qi, wi, ckv, ki, batch, t_pad, n_sel):
    n = ckv.shape[0]
    nq = t_pad // Q_BLOCK
    nkc = -(-t_pad // KEY_CHUNK)
    t_keys = nkc * KEY_CHUNK

    def pad_keys(a):
        a = a.reshape(batch, t_pad, a.shape[-1])
        return jnp.pad(a, ((0, 0), (0, t_keys - t_pad), (0, 0))).reshape(batch * t_keys, a.shape[-1])

    ckv_p, ki_p = pad_keys(ckv), pad_keys(ki)
    ckv_t = jnp.swapaxes(ckv_p.reshape(batch * nkc, KEY_CHUNK, KV_LATENT), 1, 2)
    qmap = lambda b, i: (0, b * nq + i, 0)
    cols_all = A_HEADS * Q_BLOCK
    scratch = [pltpu.VMEM((nkc, KEY_CHUNK, Q_BLOCK), jnp.int32),
               pltpu.VMEM((nkc, KEY_CHUNK, cols_all), jnp.float32),
               pltpu.VMEM((KV_LATENT, cols_all), jnp.float32)]
    return pl.pallas_call(
        functools.partial(_attn_kernel, n_sel=n_sel, t_pad=t_keys),
        out_shape=jax.ShapeDtypeStruct((A_HEADS, n, KV_LATENT), jnp.bfloat16),
        grid=(batch, nq),
        in_specs=[pl.BlockSpec((A_HEADS, Q_BLOCK, KV_LATENT), qmap),
                  pl.BlockSpec((IDX_HEADS, Q_BLOCK, IDX_DIM), qmap),
                  pl.BlockSpec((IDX_HEADS, Q_BLOCK), lambda b, i: (0, b * nq + i)),
                  pl.BlockSpec((t_keys, KV_LATENT), lambda b, i: (b, 0)),
                  pl.BlockSpec((nkc, KV_LATENT, KEY_CHUNK), lambda b, i: (b, 0, 0)),
                  pl.BlockSpec((t_keys, IDX_DIM), lambda b, i: (b, 0))],
        out_specs=pl.BlockSpec((A_HEADS, Q_BLOCK, KV_LATENT), qmap),
        scratch_shapes=scratch,
        compiler_params=_cparams(("parallel", "parallel")),
        name="sparse_attention",
    )(qlat, qi, wi.T, ckv_p, ckv_t, ki_p)


def _retention_tables():
    log_g = np.log1p(-np.exp(np.linspace(math.log(1.0 / 32), math.log(1.0 / 512), R_HEADS))).astype(np.float32)
    idx = np.arange(R_CHUNK, dtype=np.float32)
    diff = idx[:, None] - idx[None, :]
    decay = np.where(diff[None] >= 0, np.exp(diff[None] * log_g[:, None, None]), 0.0).astype(np.float32)
    q_decay = np.exp((idx + 1.0)[None, :] * log_g[:, None])[:, :, None].astype(np.float32)
    k_decay = np.exp((R_CHUNK - 1.0 - idx)[None, :] * log_g[:, None])[:, :, None].astype(np.float32)
    chunk_decay = np.exp(R_CHUNK * log_g).astype(np.float32)
    return decay, q_decay, k_decay, chunk_decay


def _retention_kernel(q_ref, k_ref, v_ref, sg_ref, rw_ref, dec_ref, qd_ref, kd_ref, o_ref, state_ref,
                      *, t_pad, chunk_decay):
    state_ref[...] = jnp.zeros_like(state_ref)
    qk_w = R_HEADS * R_QK_DIM
    lane = lax.broadcasted_iota(jnp.int32, (1, qk_w), 1)
    half_w = R_QK_DIM // 2
    head_mask = [((lane % (qk_w // 2)) // half_w) == h for h in range(R_HEADS)]

    def chunk(c, carry):
        r0 = pl.multiple_of(c * R_CHUNK, R_CHUNK)
        rows = pl.ds(r0, R_CHUNK)
        qc = q_ref[rows, :]
        kc = k_ref[rows, :]
        for h in range(R_HEADS):
            cols = slice(h * R_V_DIM, (h + 1) * R_V_DIM)
            qm = jnp.where(head_mask[h], qc, jnp.zeros_like(qc))
            km = jnp.where(head_mask[h], kc, jnp.zeros_like(kc))
            vh = v_ref[rows, cols]
            inner = lax.dot_general(qm, kc, _NT, preferred_element_type=jnp.float32) * dec_ref[h]
            st = state_ref[h]
            out = (jnp.dot(inner.astype(vh.dtype), vh, preferred_element_type=jnp.float32)
                   + jnp.dot(qm, st.astype(qm.dtype), preferred_element_type=jnp.float32) * qd_ref[h])
            kdec = (km.astype(jnp.float32) * kd_ref[h]).astype(km.dtype)
            state_ref[h] = chunk_decay[h] * st + lax.dot_general(kdec, vh, _TN,
                                                                  preferred_element_type=jnp.float32)
            mu = jnp.mean(out, axis=-1, keepdims=True)
            d = out - mu
            var = jnp.mean(d * d, axis=-1, keepdims=True)
            normed = d * lax.rsqrt(var + EPS) * rw_ref[:, cols]
            o_ref[rows, cols] = (sg_ref[rows, cols].astype(jnp.float32) * normed).astype(o_ref.dtype)
        return carry

    lax.fori_loop(0, t_pad // R_CHUNK, chunk, 0)


def _retention(qr, kr, vr, sg, ret_w, batch, t_pad):
    n = qr.shape[0]
    decay, q_decay, k_decay, chunk_decay = _retention_tables()
    qk_w = R_HEADS * R_QK_DIM
    v_w = R_HEADS * R_V_DIM
    per_b = lambda b: (b, 0)
    c2 = lambda b: (0, 0)
    c3 = lambda b: (0, 0, 0)
    return pl.pallas_call(
        functools.partial(_retention_kernel, t_pad=t_pad, chunk_decay=[float(v) for v in chunk_decay]),
        out_shape=jax.ShapeDtypeStruct((n, v_w), jnp.bfloat16),
        grid=(batch,),
        in_specs=[pl.BlockSpec((t_pad, qk_w), per_b),
                  pl.BlockSpec((t_pad, qk_w), per_b),
                  pl.BlockSpec((t_pad, v_w), per_b),
                  pl.BlockSpec((t_pad, v_w), per_b),
                  pl.BlockSpec((1, v_w), c2),
                  pl.BlockSpec((R_HEADS, R_CHUNK, R_CHUNK), c3),
                  pl.BlockSpec((R_HEADS, R_CHUNK, 1), c3),
                  pl.BlockSpec((R_HEADS, R_CHUNK, 1), c3)],
        out_specs=pl.BlockSpec((t_pad, v_w), per_b),
        scratch_shapes=[pltpu.VMEM((R_HEADS, qk_w, R_V_DIM), jnp.float32)],
        compiler_params=_cparams(("parallel",)),
        name="retention",
    )(qr, kr, vr, sg, ret_w, jnp.asarray(decay), jnp.asarray(q_decay), jnp.asarray(k_decay))


def _layer_norm(y, g, b):
    mu = jnp.mean(y, axis=-1, keepdims=True)
    d = y - mu
    var = jnp.mean(d * d, axis=-1, keepdims=True)
    return d * lax.rsqrt(var + EPS) * g + b


def _mix_router_kernel(olat_ref, ob_ref, h_ref, wuvo_ref, wob_ref, g_ref, b_ref, wrt_ref, rb_ref, tri_ref, etri_ref,
                       h1_ref, h1b_ref, x3_ref, gate_ref, loc_ref, cnt_ref, tstart_ref, tcnt_ref, run_ref,
                       *, tm):
    mix = jnp.dot(ob_ref[...], wob_ref[...], preferred_element_type=jnp.float32)
    for h in range(A_HEADS):
        mix = mix + jnp.dot(olat_ref[h], wuvo_ref[h], preferred_element_type=jnp.float32)
    h1 = _layer_norm(DN_ALPHA * h_ref[...] + mix, g_ref[...], b_ref[...])
    h1_ref[...] = h1
    h1b_ref[...] = h1.astype(h1b_ref.dtype)
    for s in range(SLABS):
        x3_ref[pl.ds(s, tm, stride=SLABS), :] = h1[:, s * LANES:(s + 1) * LANES]

    logits = lax.dot_general(wrt_ref[...], h1, _NT, preferred_element_type=jnp.float32,
                             precision=lax.Precision.HIGHEST)
    scores = 1.0 / (1.0 + jnp.exp(-logits))
    sel = scores + rb_ref[...]
    neg = -jnp.inf
    iota_g = lax.broadcasted_iota(jnp.int32, (GROUP_SIZE, tm), 0)
    iota_n = lax.broadcasted_iota(jnp.int32, (N_GROUPS, tm), 0)

    def first_argmax(v, iota, big):
        m = jnp.max(v, axis=0, keepdims=True)
        return m, jnp.min(jnp.where(v == m, iota, big), axis=0, keepdims=True)

    grp_score = jnp.zeros((N_GROUPS, tm), jnp.float32)
    for g in range(N_GROUPS):
        blk = sel[g * GROUP_SIZE:(g + 1) * GROUP_SIZE]
        m1, i1 = first_argmax(blk, iota_g, GROUP_SIZE)
        m2 = jnp.max(jnp.where(iota_g == i1, neg, blk), axis=0, keepdims=True)
        grp_score = jnp.where(iota_n == g, m1 + m2, grp_score)

    grp_on = jnp.zeros((N_GROUPS, tm), jnp.float32)
    work = grp_score
    for _ in range(TOP_GROUPS):
        _, gi = first_argmax(work, iota_n, N_GROUPS)
        hit = iota_n == gi
        grp_on = jnp.where(hit, 1.0, grp_on)
        work = jnp.where(hit, neg, work)

    masked = jnp.concatenate(
        [jnp.where(grp_on[g:g + 1] > 0.0, sel[g * GROUP_SIZE:(g + 1) * GROUP_SIZE], neg)
         for g in range(N_GROUPS)], axis=0)
    iota_e = lax.broadcasted_iota(jnp.int32, (N_EXPERTS, tm), 0)
    iota_k = lax.broadcasted_iota(jnp.int32, (TOP_K, tm), 0)
    top_idx = jnp.zeros((TOP_K, tm), jnp.int32)
    top_gate = jnp.zeros((TOP_K, tm), jnp.float32)
    hits = []
    for k in range(TOP_K):
        _, ei = first_argmax(masked, iota_e, N_EXPERTS)
        hit = iota_e == ei
        hits.append(hit)
        gk = jnp.sum(jnp.where(hit, scores, 0.0), axis=0, keepdims=True)
        masked = jnp.where(hit, neg, masked)
        top_idx = jnp.where(iota_k == k, ei, top_idx)
        top_gate = jnp.where(iota_k == k, gk, top_gate)
    gate_ref[...] = top_gate / jnp.sum(top_gate, axis=0, keepdims=True) * ROUTE_SCALE

    @pl.when(pl.program_id(0) == 0)
    def _():
        run_ref[...] = jnp.zeros_like(run_ref)

    onehot = jnp.zeros((N_EXPERTS, tm), jnp.float32)
    for hit in hits:
        onehot = jnp.where(hit, 1.0, onehot)
    before = jnp.dot(onehot.astype(jnp.bfloat16), tri_ref[...], preferred_element_type=jnp.float32)

    @pl.when(pl.program_id(0) == 0)
    def _():
        tstart_ref[...] = jnp.zeros_like(tstart_ref)
        tcnt_ref[...] = jnp.zeros_like(tcnt_ref)

    lane = lax.broadcasted_iota(jnp.int32, (1, tm), 1)
    tile_col = lax.broadcasted_iota(jnp.int32, tstart_ref.shape, 1)
    subs = tm // MOE_TILE
    slot = before
    seen = jnp.zeros((N_EXPERTS, 1), jnp.float32)
    for s in range(subs):
        in_sub = (lane >= s * MOE_TILE) & (lane < (s + 1) * MOE_TILE)
        sub_cnt = jnp.sum(jnp.where(in_sub, onehot, 0.0), axis=1, keepdims=True)
        seg_chunks = jnp.floor((sub_cnt + (SEG_ROWS - 1)) * (1.0 / SEG_ROWS))
        seg_off = SEG_ROWS * jnp.dot(etri_ref[...],
                                     jnp.broadcast_to(seg_chunks, (N_EXPERTS, LANES)).astype(jnp.bfloat16),
                                     preferred_element_type=jnp.float32)[:, :1]
        slot = jnp.where(in_sub, slot + (seg_off - seen), slot)
        this_tile = tile_col == pl.program_id(0) * subs + s
        tstart_ref[...] = jnp.where(this_tile, (run_ref[...] + seen).astype(jnp.int32), tstart_ref[...])
        tcnt_ref[...] = jnp.where(this_tile, sub_cnt.astype(jnp.int32), tcnt_ref[...])
        seen = seen + sub_cnt
    loc = jnp.zeros((TOP_K, tm), jnp.float32)
    for k in range(TOP_K):
        loc = jnp.where(iota_k == k, jnp.sum(jnp.where(hits[k], slot, 0.0), axis=0, keepdims=True), loc)
    loc_ref[...] = loc.astype(jnp.int32)
    run_ref[...] += seen
    cnt_ref[...] = run_ref[...].astype(jnp.int32)


def _mix_router(olat, ob, h, wuvo, wob, ln_g, ln_b, wrt, rbias, tm):
    n = h.shape[0]
    row = lambda i: (i, 0)
    c2 = lambda i: (0, 0)
    c3 = lambda i: (0, 0, 0)
    col = lambda i: (0, i)
    v_w = R_HEADS * R_V_DIM
    tiles_pad = -(-(n // MOE_TILE) // LANES) * LANES
    tri = jnp.triu(jnp.ones((tm, tm), jnp.bfloat16), k=1)
    etri = jnp.tril(jnp.ones((N_EXPERTS, N_EXPERTS), jnp.bfloat16), k=-1)
    return pl.pallas_call(
        functools.partial(_mix_router_kernel, tm=tm),
        out_shape=(jax.ShapeDtypeStruct((n, D_MODEL), jnp.float32),
                   jax.ShapeDtypeStruct((n, D_MODEL), jnp.bfloat16),
                   jax.ShapeDtypeStruct((n * SLABS, LANES), jnp.float32),
                   jax.ShapeDtypeStruct((TOP_K, n), jnp.float32),
                   jax.ShapeDtypeStruct((TOP_K, n), jnp.int32),
                   jax.ShapeDtypeStruct((N_EXPERTS, 1), jnp.int32),
                   jax.ShapeDtypeStruct((N_EXPERTS, tiles_pad), jnp.int32),
                   jax.ShapeDtypeStruct((N_EXPERTS, tiles_pad), jnp.int32)),
        grid=(n // tm,),
        in_specs=[pl.BlockSpec((A_HEADS, tm, KV_LATENT), lambda i: (0, i, 0)),
                  pl.BlockSpec((tm, v_w), row),
                  pl.BlockSpec((tm, D_MODEL), row),
                  pl.BlockSpec((A_HEADS, KV_LATENT, D_MODEL), c3),
                  pl.BlockSpec((v_w, D_MODEL), c2),
                  pl.BlockSpec((1, D_MODEL), c2),
                  pl.BlockSpec((1, D_MODEL), c2),
                  pl.BlockSpec((N_EXPERTS, D_MODEL), c2),
                  pl.BlockSpec((N_EXPERTS, 1), c2),
                  pl.BlockSpec((tm, tm), c2),
                  pl.BlockSpec((N_EXPERTS, N_EXPERTS), c2)],
        out_specs=(pl.BlockSpec((tm, D_MODEL), row),
                   pl.BlockSpec((tm, D_MODEL), row),
                   pl.BlockSpec((tm * SLABS, LANES), row),
                   pl.BlockSpec((TOP_K, tm), col),
                   pl.BlockSpec((TOP_K, tm), col),
                   pl.BlockSpec((N_EXPERTS, 1), c2),
                   pl.BlockSpec((N_EXPERTS, tiles_pad), c2),
                   pl.BlockSpec((N_EXPERTS, tiles_pad), c2)),
        scratch_shapes=[pltpu.VMEM((N_EXPERTS, 1), jnp.float32)],
        compiler_params=_cparams(("arbitrary",)),
        name="mix_ln_router",
    )(olat, ob, h, wuvo, wob, ln_g, ln_b, wrt, rbias, tri, etri)


def _row_copy(src, dst, sem):
    return pltpu.make_async_copy(src, dst, sem)


def _chunk_tables(tstart, tcnt, cap, max_chunks):
    chunks = (tcnt + SEG_ROWS - 1) // SEG_ROWS
    cum = jnp.cumsum(chunks, axis=1)
    first = cum - chunks
    c = jnp.arange(max_chunks, dtype=jnp.int32)
    expert = jnp.minimum(jnp.sum((cum[:, None, :] <= c[None, :, None]).astype(jnp.int32), axis=2), N_EXPERTS - 1)
    is_e = expert[:, :, None] == jnp.arange(N_EXPERTS, dtype=jnp.int32)[None, None, :]
    pick = lambda a: jnp.sum(jnp.where(is_e, a[:, None, :], 0), axis=2)
    local = c[None, :] - pick(first)
    stage_row = (pick(first) + local) * SEG_ROWS
    buffer_row = expert * cap + pick(tstart) + local * SEG_ROWS
    as_smem = lambda a: a.astype(jnp.int32)[:, None, :]
    return cum[:, -1].astype(jnp.int32), as_smem(stage_row), as_smem(buffer_row)


def _dispatch_kernel(nchunk_ref, pad_lo_ref, pad_hi_ref, loc_ref, cstage_ref, crow_ref, x3_ref, rows_ref,
                     stage_ref, zero_ref, sem, *, td):
    i = pl.program_id(0)
    last = pl.num_programs(0) - 1
    slot = i % 2

    @pl.when(i == 0)
    def _():
        stage_ref[...] = jnp.zeros_like(stage_ref)

    def place(j, carry):
        row = x3_ref[j]
        for k in range(TOP_K):
            stage_ref[slot, loc_ref[k, j]] = row
        return carry

    lax.fori_loop(0, td, place, 0)

    def drain(n_chunks, which):
        def one(c, carry):
            _row_copy(stage_ref.at[which, pl.ds(0, SEG_ROWS)], rows_ref.at[pl.ds(0, SEG_ROWS)], sem.at[which]).wait()
            return carry
        lax.fori_loop(0, n_chunks, one, 0)

    @pl.when(i > 0)
    def _():
        drain(nchunk_ref[jnp.maximum(i - 1, 0)], 1 - slot)

    def send(c, carry):
        _row_copy(stage_ref.at[slot, pl.ds(cstage_ref[0, 0, c], SEG_ROWS)],
                  rows_ref.at[pl.ds(crow_ref[0, 0, c], SEG_ROWS)], sem.at[slot]).start()
        return carry

    lax.fori_loop(0, nchunk_ref[i], send, 0)

    @pl.when(i == last)
    def _():
        drain(nchunk_ref[i], slot)
        zero_ref[...] = jnp.zeros_like(zero_ref)

        def fill(e, carry):
            def one(r, c):
                _row_copy(zero_ref, rows_ref.at[r], sem.at[0]).start()
                return c
            return lax.fori_loop(pad_lo_ref[e], pad_hi_ref[e], one, carry)

        def fill_wait(e, carry):
            def one(r, c):
                _row_copy(zero_ref, rows_ref.at[r], sem.at[0]).wait()
                return c
            return lax.fori_loop(pad_lo_ref[e], pad_hi_ref[e], one, carry)

        lax.fori_loop(0, N_EXPERTS, fill, 0)
        lax.fori_loop(0, N_EXPERTS, fill_wait, 0)


def _stage_rows(td):
    return -(-(td * TOP_K + N_EXPERTS * (SEG_ROWS - 1)) // SEG_ROWS) * SEG_ROWS


def _max_chunks(td):
    return td * TOP_K // SEG_ROWS + N_EXPERTS


def _dispatch(n_chunks, pad_lo, pad_hi, loc, chunk_stage, chunk_row, x3, n_rows, td):
    n = loc.shape[1]
    col = pl.BlockSpec((TOP_K, td), lambda i, *_: (0, i), memory_space=pltpu.SMEM)
    per_tile = pl.BlockSpec((1, 1, _max_chunks(td)), lambda i, *_: (i, 0, 0), memory_space=pltpu.SMEM)
    return pl.pallas_call(
        functools.partial(_dispatch_kernel, td=td),
        out_shape=jax.ShapeDtypeStruct((n_rows, SLABS, LANES), jnp.float32),
        grid_spec=pltpu.PrefetchScalarGridSpec(
            num_scalar_prefetch=3,
            grid=(n // td,),
            in_specs=[col, per_tile, per_tile,
                      pl.BlockSpec((td, SLABS, LANES), lambda i, *_: (i, 0, 0))],
            out_specs=pl.BlockSpec(memory_space=pl.ANY),
            scratch_shapes=[pltpu.VMEM((2, _stage_rows(td), SLABS, LANES), jnp.float32),
                            pltpu.VMEM((SLABS, LANES), jnp.float32),
                            pltpu.SemaphoreType.DMA((2,))]),
        compiler_params=pltpu.CompilerParams(dimension_semantics=("arbitrary",), has_side_effects=True,
                                             vmem_limit_bytes=VMEM_LIMIT),
        name="moe_dispatch",
    )(n_chunks, pad_lo, pad_hi, loc, chunk_stage, chunk_row, x3)


def _expert_kernel(be_ref, br_ref, nu_ref, x_ref, wg_ref, wu_ref, wd_ref, y_ref, wgu_scr, wd_scr):
    i = pl.program_id(0)

    @pl.when(i < nu_ref[0])
    def _():
        @pl.when((i == 0) | (be_ref[i] != be_ref[jnp.maximum(i - 1, 0)]))
        def _():
            wgu_scr[:, :D_EXPERT] = wg_ref[0, 0].astype(wgu_scr.dtype)
            wgu_scr[:, D_EXPERT:] = wu_ref[0, 0].astype(wgu_scr.dtype)
            wd_scr[...] = wd_ref[0, 0].astype(wd_scr.dtype)

        pair = 2 * LANES
        gu = jnp.zeros((ROW_BLOCK, 2 * D_EXPERT), jnp.float32)
        for s in range(0, SLABS, 2):
            xs = jnp.concatenate([x_ref[pl.ds(s, ROW_BLOCK, stride=SLABS), :],
                                  x_ref[pl.ds(s + 1, ROW_BLOCK, stride=SLABS), :]], axis=1).astype(jnp.bfloat16)
            gu = gu + jnp.dot(xs, wgu_scr[s * LANES:s * LANES + pair, :], preferred_element_type=jnp.float32)
        g, u = gu[:, :D_EXPERT], gu[:, D_EXPERT:]
        hdn = (g / (1.0 + jnp.exp(-g)) * u).astype(jnp.bfloat16)
        y = jnp.dot(hdn, wd_scr[...], preferred_element_type=jnp.float32)
        for s in range(SLABS):
            y_ref[pl.ds(s, ROW_BLOCK, stride=SLABS), :] = y[:, s * LANES:(s + 1) * LANES]


def _experts(block_expert, block_row, n_used, x_rows2, wg, wu, wd, layer):
    n_blocks = block_expert.shape[0]
    blk = lambda i, be, br, nu: (br[jnp.minimum(i, nu[0] - 1)], 0)
    wsel = lambda i, be, br, nu: (layer, be[jnp.minimum(i, nu[0] - 1)], 0, 0)
    return pl.pallas_call(
        _expert_kernel,
        out_shape=jax.ShapeDtypeStruct(x_rows2.shape, jnp.float32),
        grid_spec=pltpu.PrefetchScalarGridSpec(
            num_scalar_prefetch=3,
            grid=(n_blocks,),
            in_specs=[pl.BlockSpec((ROW_BLOCK * SLABS, LANES), blk),
                      pl.BlockSpec((1, 1, D_MODEL, D_EXPERT), wsel),
                      pl.BlockSpec((1, 1, D_MODEL, D_EXPERT), wsel),
                      pl.BlockSpec((1, 1, D_EXPERT, D_MODEL), wsel)],
            out_specs=pl.BlockSpec((ROW_BLOCK * SLABS, LANES), blk),
            scratch_shapes=[pltpu.VMEM((D_MODEL, 2 * D_EXPERT), jnp.bfloat16),
                            pltpu.VMEM((D_EXPERT, D_MODEL), jnp.bfloat16)]),
        compiler_params=_cparams(("arbitrary",)),
        name="moe_experts",
    )(block_expert, block_row, n_used, x_rows2, wg, wu, wd)


def _combine_kernel(nchunk_ref, loc_ref, gate_ref, cstage_ref, crow_ref, cstage_next_ref, crow_next_ref,
                    y3_ref, h1_ref, h1b_ref, wsg_ref, wsu_ref, wsd_ref, g_ref, b_ref, h2_ref, h2b_ref,
                    buf_ref, comb_ref, sem, *, tc):
    i = pl.program_id(0)
    last = pl.num_programs(0) - 1
    slot = i % 2

    def fetch(stage_tbl, row_tbl, n_chunks, which):
        def one(c, carry):
            _row_copy(y3_ref.at[pl.ds(row_tbl[0, 0, c], SEG_ROWS)],
                      buf_ref.at[which, pl.ds(stage_tbl[0, 0, c], SEG_ROWS)], sem.at[which]).start()
            return carry
        lax.fori_loop(0, n_chunks, one, 0)

    @pl.when(i == 0)
    def _():
        fetch(cstage_ref, crow_ref, nchunk_ref[0], 0)

    @pl.when(i < last)
    def _():
        fetch(cstage_next_ref, crow_next_ref, nchunk_ref[jnp.minimum(i + 1, last)], 1 - slot)

    xb = h1b_ref[...]
    gs = jnp.dot(xb, wsg_ref[...], preferred_element_type=jnp.float32)
    us = jnp.dot(xb, wsu_ref[...], preferred_element_type=jnp.float32)
    hs = (gs / (1.0 + jnp.exp(-gs)) * us).astype(jnp.bfloat16)
    shared = jnp.dot(hs, wsd_ref[...], preferred_element_type=jnp.float32)

    def drain_chunk(c, carry):
        _row_copy(y3_ref.at[pl.ds(0, SEG_ROWS)], buf_ref.at[slot, pl.ds(0, SEG_ROWS)], sem.at[slot]).wait()
        return carry

    lax.fori_loop(0, nchunk_ref[i], drain_chunk, 0)

    def weigh(j, carry):
        acc = gate_ref[0, j] * buf_ref[slot, loc_ref[0, j]]
        for k in range(1, TOP_K):
            acc = acc + gate_ref[k, j] * buf_ref[slot, loc_ref[k, j]]
        comb_ref[pl.ds(pl.multiple_of(j * SLABS, SLABS), SLABS), :] = acc
        return carry

    lax.fori_loop(0, tc, weigh, 0)
    routed = jnp.concatenate([comb_ref[pl.ds(s, tc, stride=SLABS), :] for s in range(SLABS)], axis=1)
    h2 = _layer_norm(DN_ALPHA * h1_ref[...] + (routed + shared), g_ref[...], b_ref[...])
    h2_ref[...] = h2
    h2b_ref[...] = h2.astype(h2b_ref.dtype)


def _combine(n_chunks, loc, gates, chunk_stage, chunk_row, y3, h1, h1b, wsg, wsu, wsd, ln_g, ln_b, tc):
    n = h1.shape[0]
    row = lambda i, *_: (i, 0)
    c2 = lambda i, *_: (0, 0)
    smem_col = pl.BlockSpec((TOP_K, tc), lambda i, *_: (0, i), memory_space=pltpu.SMEM)
    per_tile = pl.BlockSpec((1, 1, _max_chunks(tc)), lambda i, *_: (i, 0, 0), memory_space=pltpu.SMEM)
    next_tile = pl.BlockSpec((1, 1, _max_chunks(tc)), lambda i, *_: (jnp.minimum(i + 1, n // tc - 1), 0, 0),
                             memory_space=pltpu.SMEM)
    return pl.pallas_call(
        functools.partial(_combine_kernel, tc=tc),
        out_shape=(jax.ShapeDtypeStruct((n, D_MODEL), jnp.float32),
                   jax.ShapeDtypeStruct((n, D_MODEL), jnp.bfloat16)),
        grid_spec=pltpu.PrefetchScalarGridSpec(
            num_scalar_prefetch=1,
            grid=(n // tc,),
            in_specs=[smem_col, smem_col, per_tile, per_tile, next_tile, next_tile,
                      pl.BlockSpec(memory_space=pl.ANY),
                      pl.BlockSpec((tc, D_MODEL), row),
                      pl.BlockSpec((tc, D_MODEL), row),
                      pl.BlockSpec((D_MODEL, D_SHARED), c2),
                      pl.BlockSpec((D_MODEL, D_SHARED), c2),
                      pl.BlockSpec((D_SHARED, D_MODEL), c2),
                      pl.BlockSpec((1, D_MODEL), c2),
                      pl.BlockSpec((1, D_MODEL), c2)],
            out_specs=(pl.BlockSpec((tc, D_MODEL), row),
                       pl.BlockSpec((tc, D_MODEL), row)),
            scratch_shapes=[pltpu.VMEM((2, _stage_rows(tc), SLABS, LANES), jnp.float32),
                            pltpu.VMEM((tc * SLABS, LANES), jnp.float32),
                            pltpu.SemaphoreType.DMA((2,))]),
        compiler_params=_cparams(("arbitrary",)),
        name="moe_combine",
    )(n_chunks, loc, gates, chunk_stage, chunk_row, chunk_stage, chunk_row, y3, h1, h1b, wsg, wsu, wsd, ln_g, ln_b)


def _routing_plan(counts, n_blocks, cap):
    blocks_per_e = (counts + ROW_BLOCK - 1) // ROW_BLOCK
    blk_end = jnp.cumsum(blocks_per_e)
    blk_start = blk_end - blocks_per_e
    block_ids = jnp.arange(n_blocks, dtype=jnp.int32)
    block_expert = jnp.minimum(jnp.sum((blk_end[None, :] <= block_ids[:, None]).astype(jnp.int32), axis=1),
                               N_EXPERTS - 1)
    start_of = jnp.sum(jnp.where(block_expert[:, None] == jnp.arange(N_EXPERTS, dtype=jnp.int32)[None, :],
                                 blk_start[None, :], 0), axis=1)
    block_row = block_expert * (cap // ROW_BLOCK) + (block_ids - start_of)
    expert_row0 = jnp.arange(N_EXPERTS, dtype=jnp.int32) * cap
    pad_lo = expert_row0 + counts
    pad_hi = expert_row0 + blocks_per_e * ROW_BLOCK
    return block_expert, block_row.astype(jnp.int32), blk_end[-1:], pad_lo, pad_hi


def _prepare_weights(w_in, w_uk, w_uv, w_o):
    L = w_in.shape[0]
    o = IN_OFFS
    bf = jnp.bfloat16
    w_qa = w_in[:, :, o[0]:o[1]].reshape(L, D_MODEL, A_HEADS, A_HEAD_DIM).transpose(0, 2, 1, 3)
    w_qlat = _fold(w_qa, w_uk, A_HEAD_DIM ** -0.5)
    w_qlat = w_qlat.transpose(0, 2, 1, 3).reshape(L, D_MODEL, A_HEADS * KV_LATENT)
    w_oa = w_o[:, :A_HEADS * A_HEAD_DIM].reshape(L, A_HEADS, A_HEAD_DIM, D_MODEL)
    w_uvo = _fold(w_uv, w_oa, 1.0)
    w_ob = w_o[:, A_HEADS * A_HEAD_DIM:].astype(bf)

    def pad_cols(w, width):
        return jnp.pad(w, ((0, 0), (0, 0), (0, width - w.shape[-1])))

    half = R_QK_DIM // 2
    perm = np.concatenate([np.arange(half) + R_QK_DIM * h for h in range(R_HEADS)]
                          + [np.arange(half) + R_QK_DIM * h + half for h in range(R_HEADS)])
    w_cat = jnp.concatenate([
        w_qlat,
        w_in[:, :, o[1]:o[2]].astype(bf),
        w_in[:, :, o[2]:o[3]].astype(bf),
        pad_cols(w_in[:, :, o[3]:o[4]], LANES).astype(bf),
        pad_cols(w_in[:, :, o[4]:o[5]], LANES).astype(bf),
        w_in[:, :, o[5]:o[6]][:, :, perm].astype(bf),
        w_in[:, :, o[6]:o[7]][:, :, perm].astype(bf),
        w_in[:, :, o[7]:o[8]].astype(bf),
        w_in[:, :, o[8]:o[9]].astype(bf)], axis=-1)
    return w_cat, w_uvo, w_ob


def _rotary_tables(t_pad):
    half = R_QK_DIM // 2
    inv = ROPE_BASE ** (-jnp.arange(half, dtype=jnp.float32) / half)
    ang = jnp.arange(t_pad, dtype=jnp.float32)[:, None] * inv
    return jnp.tile(jnp.cos(ang), (1, R_HEADS)), jnp.tile(jnp.sin(ang), (1, R_HEADS))


def _pick_tile(n, prefer):
    for t in prefer:
        if n % t == 0:
            return t
    raise ValueError(f"no tile for {n}")


def kernel(x, meta_tokens, w_in, w_uk, w_uv, kv_norm_w, ret_norm_w, w_o, ln1_g, ln1_b, w_router, router_bias,
           w_gate, w_up, w_down, ws_gate, ws_up, ws_down, ln2_g, ln2_b):
    b, s, d = x.shape
    assert d == D_MODEL
    L = w_in.shape[0]
    t = s + N_META
    n_sel = min(TOPK_MAX, s // 4)
    t_pad = -(-t // LANES) * LANES
    n = b * t_pad
    bf = jnp.bfloat16

    meta = jnp.broadcast_to(meta_tokens.astype(x.dtype)[None], (b, N_META, d))
    h = jnp.concatenate([meta, x, jnp.zeros((b, t_pad - t, d), x.dtype)], axis=1).reshape(n, d)
    hb = h.astype(bf)

    w_cat, w_uvo, w_ob = _prepare_weights(w_in, w_uk, w_uv, w_o)
    cos_t, sin_t = _rotary_tables(t_pad)
    wsg, wsu, wsd = ws_gate.astype(bf), ws_up.astype(bf), ws_down.astype(bf)
    w_rt = jnp.swapaxes(w_router, 1, 2)

    tm = _pick_tile(n, (2 * MOE_TILE, MOE_TILE))
    n_tiles = n // MOE_TILE
    n_blocks = -(-(n * TOP_K) // ROW_BLOCK) + N_EXPERTS
    cap = (-(-n // ROW_BLOCK) + 1) * ROW_BLOCK
    n_rows = N_EXPERTS * cap

    for l in range(L):
        qlat, ckv, qi, ki, wi, qr, kr, vr, sg = _inproj(hb, w_cat[l], kv_norm_w[l][None], cos_t, sin_t, t_pad)
        olat = _attention(qlat, qi, wi, ckv, ki, b, t_pad, n_sel)
        ob = _retention(qr, kr, vr, sg, ret_norm_w[l][None], b, t_pad)
        h1, h1b, x3, gates, loc, counts, tstart, tcnt = _mix_router(
            olat, ob, h, w_uvo[l], w_ob[l], ln1_g[l][None], ln1_b[l][None], w_rt[l], router_bias[l][:, None], tm)
        n_chunks, chunk_stage, chunk_row = _chunk_tables(tstart[:, :n_tiles].T, tcnt[:, :n_tiles].T, cap,
                                                         _max_chunks(MOE_TILE))
        block_expert, block_row, n_used, pad_lo, pad_hi = _routing_plan(counts[:, 0], n_blocks, cap)
        x_rows = _dispatch(n_chunks, pad_lo, pad_hi, loc, chunk_stage, chunk_row, x3.reshape(n, SLABS, LANES),
                           n_rows, MOE_TILE)
        y_rows = _experts(block_expert, block_row, n_used, x_rows.reshape(n_rows * SLABS, LANES),
                          w_gate, w_up, w_down, l)
        h, hb = _combine(n_chunks, loc, gates, chunk_stage, chunk_row, y_rows.reshape(n_rows, SLABS, LANES), h1, h1b,
                         wsg[l], wsu[l], wsd[l], ln2_g[l][None], ln2_b[l][None], MOE_TILE)
    return h.reshape(b, t_pad, d)[:, N_META:t]
```

```python
import functools
import math

import numpy as np
import jax
import jax.numpy as jnp
from jax import lax
from jax.experimental import pallas as pl
from jax.experimental.pallas import tpu as pltpu

D_MODEL = 1024
N_META = 16
A_HEADS = 8
A_HEAD_DIM = 64
KV_LATENT = 128
IDX_HEADS = 8
IDX_DIM = 64
TOPK_MAX = 256
R_HEADS = 4
R_QK_DIM = 64
R_V_DIM = 128
ROPE_BASE = 10000.0
N_EXPERTS = 64
N_GROUPS = 8
GROUP_SIZE = N_EXPERTS // N_GROUPS
TOP_GROUPS = 4
TOP_K = 8
D_EXPERT = 256
D_SHARED = 256
ROUTE_SCALE = 2.5
DEPTH = 4
DN_ALPHA = (2 * DEPTH) ** 0.25
EPS = 1e-6

IN_COLS = (A_HEADS * A_HEAD_DIM, KV_LATENT, IDX_HEADS * IDX_DIM, IDX_DIM, IDX_HEADS,
           R_HEADS * R_QK_DIM, R_HEADS * R_QK_DIM, R_HEADS * R_V_DIM, R_HEADS * R_V_DIM)
IN_OFFS = tuple(int(v) for v in np.cumsum((0,) + IN_COLS))

LANES = 128
SUBLANES = 8
Q_BLOCK = 128
KEY_CHUNK = 256
SEARCH_ALWAYS = 16
SEARCH_GROUP = 4
R_CHUNK = 128
ROW_BLOCK = 256
MOE_TILE = 256
SEG_ROWS = 32
SLABS = D_MODEL // LANES
VMEM_LIMIT = 56 * 1024 * 1024

C_QLAT = 0
C_CKV = C_QLAT + A_HEADS * KV_LATENT
C_QI = C_CKV + KV_LATENT
C_KI = C_QI + IDX_HEADS * IDX_DIM
C_WI = C_KI + LANES
C_QR = C_WI + LANES
C_KR = C_QR + R_HEADS * R_QK_DIM
C_VR = C_KR + R_HEADS * R_QK_DIM
C_GR = C_VR + R_HEADS * R_V_DIM
C_END = C_GR + R_HEADS * R_V_DIM

INT_MIN = -2 ** 31
NEG_BIG = -1e30

_NT = (((1,), (1,)), ((), ()))
_TN = (((0,), (0,)), ((), ()))


def _cparams(sem):
    return pltpu.CompilerParams(dimension_semantics=sem, vmem_limit_bytes=VMEM_LIMIT)


def _fold_kernel(a_ref, b_ref, o_ref, *, scale):
    o_ref[0, 0] = (jnp.dot(a_ref[0, 0], b_ref[0, 0], preferred_element_type=jnp.float32,
                           precision=lax.Precision.HIGHEST) * scale).astype(o_ref.dtype)


def _fold(a, b, scale):
    L, H, M, K = a.shape
    N = b.shape[-1]
    return pl.pallas_call(
        functools.partial(_fold_kernel, scale=scale),
        out_shape=jax.ShapeDtypeStruct((L, H, M, N), jnp.bfloat16),
        grid=(L, H),
        in_specs=[pl.BlockSpec((1, 1, M, K), lambda l, h: (l, h, 0, 0)),
                  pl.BlockSpec((1, 1, K, N), lambda l, h: (l, h, 0, 0))],
        out_specs=pl.BlockSpec((1, 1, M, N), lambda l, h: (l, h, 0, 0)),
        compiler_params=_cparams(("parallel", "parallel")),
        name="weight_fold",
    )(a, b)


def _inproj_kernel(x_ref, w_ref, kvg_ref, cos_ref, sin_ref,
                   qlat_ref, ckv_ref, qi_ref, ki_ref, wi_ref, qr_ref, kr_ref, vr_ref, sg_ref):
    x = x_ref[...]

    def proj(lo, hi):
        return jnp.dot(x, w_ref[:, lo:hi], preferred_element_type=jnp.float32)

    r = proj(C_QLAT, C_CKV)
    for h in range(A_HEADS):
        qlat_ref[h] = r[:, h * KV_LATENT:(h + 1) * KV_LATENT].astype(qlat_ref.dtype)

    r = proj(C_CKV, C_QI)
    r = r * lax.rsqrt(jnp.mean(r * r, axis=-1, keepdims=True) + EPS) * kvg_ref[...]
    ckv_ref[...] = r.astype(ckv_ref.dtype)

    r = proj(C_QI, C_KI)
    for h in range(IDX_HEADS):
        qi_ref[h] = r[:, h * IDX_DIM:(h + 1) * IDX_DIM].astype(qi_ref.dtype)

    r = proj(C_KI, C_WI)
    ki_ref[...] = r[:, :IDX_DIM].astype(ki_ref.dtype)
    r = proj(C_WI, C_QR)
    wi_ref[...] = r[:, :IDX_HEADS] * (IDX_HEADS ** -0.5)

    cos = cos_ref[...]
    sin = sin_ref[...]
    half = R_HEADS * R_QK_DIM // 2

    def rot(r, out_ref, scale):
        x1, x2 = r[:, :half], r[:, half:]
        out_ref[:, :half] = ((x1 * cos - x2 * sin) * scale).astype(out_ref.dtype)
        out_ref[:, half:] = ((x1 * sin + x2 * cos) * scale).astype(out_ref.dtype)

    rot(proj(C_QR, C_KR), qr_ref, 1.0)
    rot(proj(C_KR, C_VR), kr_ref, R_QK_DIM ** -0.5)
    vr_ref[...] = proj(C_VR, C_GR).astype(vr_ref.dtype)
    g = proj(C_GR, C_END)
    sg_ref[...] = (g / (1.0 + jnp.exp(-g))).astype(sg_ref.dtype)


def _inproj(xb, w_cat, kv_g, cos_t, sin_t, t_pad):
    n = xb.shape[0]
    tm = t_pad // 4
    per_b = t_pad // tm
    bf = jnp.bfloat16
    row = lambda i: (i, 0)
    head = lambda i: (0, i, 0)
    const = lambda i: (0, 0)
    pos = lambda i: (i % per_b, 0)
    return pl.pallas_call(
        _inproj_kernel,
        out_shape=(jax.ShapeDtypeStruct((A_HEADS, n, KV_LATENT), bf),
                   jax.ShapeDtypeStruct((n, KV_LATENT), bf),
                   jax.ShapeDtypeStruct((IDX_HEADS, n, IDX_DIM), bf),
                   jax.ShapeDtypeStruct((n, IDX_DIM), bf),
                   jax.ShapeDtypeStruct((n, IDX_HEADS), jnp.float32),
                   jax.ShapeDtypeStruct((n, R_HEADS * R_QK_DIM), bf),
                   jax.ShapeDtypeStruct((n, R_HEADS * R_QK_DIM), bf),
                   jax.ShapeDtypeStruct((n, R_HEADS * R_V_DIM), bf),
                   jax.ShapeDtypeStruct((n, R_HEADS * R_V_DIM), bf)),
        grid=(n // tm,),
        in_specs=[pl.BlockSpec((tm, D_MODEL), row),
                  pl.BlockSpec((D_MODEL, C_END), const),
                  pl.BlockSpec((1, KV_LATENT), const),
                  pl.BlockSpec((tm, LANES), pos),
                  pl.BlockSpec((tm, LANES), pos)],
        out_specs=(pl.BlockSpec((A_HEADS, tm, KV_LATENT), head),
                   pl.BlockSpec((tm, KV_LATENT), row),
                   pl.BlockSpec((IDX_HEADS, tm, IDX_DIM), head),
                   pl.BlockSpec((tm, IDX_DIM), row),
                   pl.BlockSpec((tm, IDX_HEADS), row),
                   pl.BlockSpec((tm, R_HEADS * R_QK_DIM), row),
                   pl.BlockSpec((tm, R_HEADS * R_QK_DIM), row),
                   pl.BlockSpec((tm, R_HEADS * R_V_DIM), row),
                   pl.BlockSpec((tm, R_HEADS * R_V_DIM), row)),
        compiler_params=_cparams(("parallel",)),
        name="in_proj",
    )(xb, w_cat, kv_g, cos_t, sin_t)


def _fold_rows(x, op):
    out = x[:SUBLANES]
    for r in range(1, x.shape[0] // SUBLANES):
        out = op(out, x[r * SUBLANES:(r + 1) * SUBLANES])
    return out


def _attn_kernel(qlat_ref, qi_ref, wit_ref, ckv_ref, ckvt_ref, ki_ref, o_ref, key_scr, lg_scr, acc_scr,
                 *, n_sel, t_pad):
    i = pl.program_id(1)
    nk = ((i + 1) * Q_BLOCK + KEY_CHUNK - 1) // KEY_CHUNK
    tile = (KEY_CHUNK, Q_BLOCK)
    cols_all = A_HEADS * Q_BLOCK
    q_all = qlat_ref[...].reshape(cols_all, KV_LATENT)
    qi_all = qi_ref[...].reshape(IDX_HEADS * Q_BLOCK, IDX_DIM)
    wt = wit_ref[...]
    k_off = lax.broadcasted_iota(jnp.int32, tile, 0)
    q_off = lax.broadcasted_iota(jnp.int32, tile, 1)

    def key_rows(kt):
        return pl.ds(pl.multiple_of(kt * KEY_CHUNK, KEY_CHUNK), KEY_CHUNK)

    def key_pos(kt):
        return kt * KEY_CHUNK + k_off

    def index_tile(kt, carry):
        qk = lax.dot_general(ki_ref[key_rows(kt), :], qi_all, _NT, preferred_element_type=jnp.float32)
        score = wt[0:1] * jnp.maximum(qk[:, :Q_BLOCK], 0.0)
        for h in range(1, IDX_HEADS):
            score = score + wt[h:h + 1] * jnp.maximum(qk[:, h * Q_BLOCK:(h + 1) * Q_BLOCK], 0.0)
        bits = lax.bitcast_convert_type(score, jnp.int32)
        key = bits ^ ((bits >> 31) & jnp.int32(0x7FFFFFFF))
        key = jnp.where(score == 0.0, 0, key)
        causal = key_pos(kt) <= (i * Q_BLOCK + q_off)
        key_scr[kt] = jnp.where(causal, key, INT_MIN)
        return carry

    lax.fori_loop(0, nk, index_tile, 0)

    def count(pred):
        def body(kt, acc):
            return acc + jnp.where(pred(key_scr[kt], kt), 1.0, 0.0)
        acc = lax.fori_loop(0, nk, body, jnp.zeros(tile, jnp.float32))
        return jnp.sum(acc, axis=0, keepdims=True)

    def thr_step(it, state):
        t, n_ge = state
        cand = t ^ lax.shift_left(jnp.int32(1), 31 - it)
        n_cand = count(lambda k, kt: k >= cand)
        keep = n_cand >= n_sel
        return jnp.where(keep, cand, t), jnp.where(keep, n_cand, n_ge)

    n_real = count(lambda k, kt: k != INT_MIN)
    state = (jnp.full((1, Q_BLOCK), INT_MIN, jnp.int32),
             jnp.zeros((1, Q_BLOCK), jnp.float32) + (nk * KEY_CHUNK).astype(jnp.float32))
    state = lax.fori_loop(0, SEARCH_ALWAYS, thr_step, state)

    def unsettled(state):
        t, n_ge = state
        return jnp.max(jnp.where((n_ge == n_sel) | (n_real < n_sel), 0.0, 1.0)) > 0.0

    def more_bits(carry):
        it, state = carry
        for _ in range(SEARCH_GROUP):
            state = thr_step(it, state)
            it = it + 1
        return it, state

    _, (t, _) = lax.while_loop(lambda c: (c[0] < 32) & unsettled(c[1]), more_bits, (jnp.int32(SEARCH_ALWAYS), state))

    need = n_sel - count(lambda k, kt: k > t)
    n_eq = count(lambda k, kt: k == t)
    idx_bits = max(1, (t_pad - 1).bit_length())
    surplus = jnp.max(jnp.where((t > INT_MIN) & (n_eq > need), 1.0, 0.0)) > 0.0

    def tie_search():
        def tie_step(it, c):
            cand = c | lax.shift_left(jnp.int32(1), idx_bits - 1 - it)
            below = count(lambda k, kt: (k == t) & (key_pos(kt) < cand))
            return jnp.where(below <= need, cand, c)
        return lax.fori_loop(0, idx_bits, tie_step, jnp.zeros((1, Q_BLOCK), jnp.int32))

    c = lax.cond(surplus, tie_search, lambda: jnp.full((1, Q_BLOCK), 1 << idx_bits, jnp.int32))

    def logit_tile(kt, m8):
        key = key_scr[kt]
        sel = ((key > t) | ((key == t) & (key_pos(kt) < c))) & (key != INT_MIN)
        lg = lax.dot_general(ckv_ref[key_rows(kt), :], q_all, _NT, preferred_element_type=jnp.float32)
        lg = jnp.where(jnp.concatenate([sel] * A_HEADS, axis=1), lg, NEG_BIG)
        lg_scr[kt] = lg
        return jnp.maximum(m8, _fold_rows(lg, jnp.maximum))

    m8 = lax.fori_loop(0, nk, logit_tile, jnp.full((SUBLANES, cols_all), NEG_BIG, jnp.float32))
    m = jnp.max(m8, axis=0, keepdims=True)
    acc_scr[...] = jnp.zeros_like(acc_scr)

    def pv_tile(kt, l8):
        p = jnp.exp(lg_scr[kt] - m)
        acc_scr[...] += jnp.dot(ckvt_ref[kt], p.astype(ckvt_ref.dtype), preferred_element_type=jnp.float32)
        return l8 + _fold_rows(p, jnp.add)

    l8 = lax.fori_loop(0, nk, pv_tile, jnp.zeros((SUBLANES, cols_all), jnp.float32))
    o_t = acc_scr[...] / jnp.sum(l8, axis=0, keepdims=True)
    for h in range(A_HEADS):
        o_ref[h] = o_t[:, h * Q_BLOCK:(h + 1) * Q_BLOCK].T.astype(o_ref.dtype)


def _attention(qlat, qi, wi, ckv, ki, batch, t_pad, n_sel):
    n = ckv.shape[0]
    nq = t_pad // Q_BLOCK
    nkc = -(-t_pad // KEY_CHUNK)
    t_keys = nkc * KEY_CHUNK

    def pad_keys(a):
        a = a.reshape(batch, t_pad, a.shape[-1])
        return jnp.pad(a, ((0, 0), (0, t_keys - t_pad), (0, 0))).reshape(batch * t_keys, a.shape[-1])

    ckv_p, ki_p = pad_keys(ckv), pad_keys(ki)
    ckv_t = jnp.swapaxes(ckv_p.reshape(batch * nkc, KEY_CHUNK, KV_LATENT), 1, 2)
    qmap = lambda b, i: (0, b * nq + i, 0)
    cols_all = A_HEADS * Q_BLOCK
    scratch = [pltpu.VMEM((nkc, KEY_CHUNK, Q_BLOCK), jnp.int32),
               pltpu.VMEM((nkc, KEY_CHUNK, cols_all), jnp.float32),
               pltpu.VMEM((KV_LATENT, cols_all), jnp.float32)]
    return pl.pallas_call(
        functools.partial(_attn_kernel, n_sel=n_sel, t_pad=t_keys),
        out_shape=jax.ShapeDtypeStruct((A_HEADS, n, KV_LATENT), jnp.bfloat16),
        grid=(batch, nq),
        in_specs=[pl.BlockSpec((A_HEADS, Q_BLOCK, KV_LATENT), qmap),
                  pl.BlockSpec((IDX_HEADS, Q_BLOCK, IDX_DIM), qmap),
                  pl.BlockSpec((IDX_HEADS, Q_BLOCK), lambda b, i: (0, b * nq + i)),
                  pl.BlockSpec((t_keys, KV_LATENT), lambda b, i: (b, 0)),
                  pl.BlockSpec((nkc, KV_LATENT, KEY_CHUNK), lambda b, i: (b, 0, 0)),
                  pl.BlockSpec((t_keys, IDX_DIM), lambda b, i: (b, 0))],
        out_specs=pl.BlockSpec((A_HEADS, Q_BLOCK, KV_LATENT), qmap),
        scratch_shapes=scratch,
        compiler_params=_cparams(("parallel", "parallel")),
        name="sparse_attention",
    )(qlat, qi, wi.T, ckv_p, ckv_t, ki_p)


def _retention_tables():
    log_g = np.log1p(-np.exp(np.linspace(math.log(1.0 / 32), math.log(1.0 / 512), R_HEADS))).astype(np.float32)
    idx = np.arange(R_CHUNK, dtype=np.float32)
    diff = idx[:, None] - idx[None, :]
    decay = np.where(diff[None] >= 0, np.exp(diff[None] * log_g[:, None, None]), 0.0).astype(np.float32)
    q_decay = np.exp((idx + 1.0)[None, :] * log_g[:, None])[:, :, None].astype(np.float32)
    k_decay = np.exp((R_CHUNK - 1.0 - idx)[None, :] * log_g[:, None])[:, :, None].astype(np.float32)
    chunk_decay = np.exp(R_CHUNK * log_g).astype(np.float32)
    return decay, q_decay, k_decay, chunk_decay


def _retention_kernel(q_ref, k_ref, v_ref, sg_ref, rw_ref, dec_ref, qd_ref, kd_ref, o_ref, state_ref,
                      *, t_pad, chunk_decay):
    state_ref[...] = jnp.zeros_like(state_ref)
    qk_w = R_HEADS * R_QK_DIM
    lane = lax.broadcasted_iota(jnp.int32, (1, qk_w), 1)
    half_w = R_QK_DIM // 2
    head_mask = [((lane % (qk_w // 2)) // half_w) == h for h in range(R_HEADS)]

    def chunk(c, carry):
        r0 = pl.multiple_of(c * R_CHUNK, R_CHUNK)
        rows = pl.ds(r0, R_CHUNK)
        qc = q_ref[rows, :]
        kc = k_ref[rows, :]
        for h in range(R_HEADS):
            cols = slice(h * R_V_DIM, (h + 1) * R_V_DIM)
            qm = jnp.where(head_mask[h], qc, jnp.zeros_like(qc))
            km = jnp.where(head_mask[h], kc, jnp.zeros_like(kc))
            vh = v_ref[rows, cols]
            inner = lax.dot_general(qm, kc, _NT, preferred_element_type=jnp.float32) * dec_ref[h]
            st = state_ref[h]
            out = (jnp.dot(inner.astype(vh.dtype), vh, preferred_element_type=jnp.float32)
                   + jnp.dot(qm, st.astype(qm.dtype), preferred_element_type=jnp.float32) * qd_ref[h])
            kdec = (km.astype(jnp.float32) * kd_ref[h]).astype(km.dtype)
            state_ref[h] = chunk_decay[h] * st + lax.dot_general(kdec, vh, _TN,
                                                                  preferred_element_type=jnp.float32)
            mu = jnp.mean(out, axis=-1, keepdims=True)
            d = out - mu
            var = jnp.mean(d * d, axis=-1, keepdims=True)
            normed = d * lax.rsqrt(var + EPS) * rw_ref[:, cols]
            o_ref[rows, cols] = (sg_ref[rows, cols].astype(jnp.float32) * normed).astype(o_ref.dtype)
        return carry

    lax.fori_loop(0, t_pad // R_CHUNK, chunk, 0)


def _retention(qr, kr, vr, sg, ret_w, batch, t_pad):
    n = qr.shape[0]
    decay, q_decay, k_decay, chunk_decay = _retention_tables()
    qk_w = R_HEADS * R_QK_DIM
    v_w = R_HEADS * R_V_DIM
    per_b = lambda b: (b, 0)
    c2 = lambda b: (0, 0)
    c3 = lambda b: (0, 0, 0)
    return pl.pallas_call(
        functools.partial(_retention_kernel, t_pad=t_pad, chunk_decay=[float(v) for v in chunk_decay]),
        out_shape=jax.ShapeDtypeStruct((n, v_w), jnp.bfloat16),
        grid=(batch,),
        in_specs=[pl.BlockSpec((t_pad, qk_w), per_b),
                  pl.BlockSpec((t_pad, qk_w), per_b),
                  pl.BlockSpec((t_pad, v_w), per_b),
                  pl.BlockSpec((t_pad, v_w), per_b),
                  pl.BlockSpec((1, v_w), c2),
                  pl.BlockSpec((R_HEADS, R_CHUNK, R_CHUNK), c3),
                  pl.BlockSpec((R_HEADS, R_CHUNK, 1), c3),
                  pl.BlockSpec((R_HEADS, R_CHUNK, 1), c3)],
        out_specs=pl.BlockSpec((t_pad, v_w), per_b),
        scratch_shapes=[pltpu.VMEM((R_HEADS, qk_w, R_V_DIM), jnp.float32)],
        compiler_params=_cparams(("parallel",)),
        name="retention",
    )(qr, kr, vr, sg, ret_w, jnp.asarray(decay), jnp.asarray(q_decay), jnp.asarray(k_decay))


def _layer_norm(y, g, b):
    mu = jnp.mean(y, axis=-1, keepdims=True)
    d = y - mu
    var = jnp.mean(d * d, axis=-1, keepdims=True)
    return d * lax.rsqrt(var + EPS) * g + b


def _mix_router_kernel(olat_ref, ob_ref, h_ref, wuvo_ref, wob_ref, g_ref, b_ref, wrt_ref, rb_ref, tri_ref, etri_ref,
                       h1_ref, h1b_ref, x3_ref, gate_ref, loc_ref, cnt_ref, tstart_ref, tcnt_ref, run_ref,
                       *, tm):
    mix = jnp.dot(ob_ref[...], wob_ref[...], preferred_element_type=jnp.float32)
    for h in range(A_HEADS):
        mix = mix + jnp.dot(olat_ref[h], wuvo_ref[h], preferred_element_type=jnp.float32)
    h1 = _layer_norm(DN_ALPHA * h_ref[...] + mix, g_ref[...], b_ref[...])
    h1_ref[...] = h1
    h1b_ref[...] = h1.astype(h1b_ref.dtype)
    for s in range(SLABS):
        x3_ref[pl.ds(s, tm, stride=SLABS), :] = h1[:, s * LANES:(s + 1) * LANES]

    logits = lax.dot_general(wrt_ref[...], h1, _NT, preferred_element_type=jnp.float32,
                             precision=lax.Precision.HIGHEST)
    scores = 1.0 / (1.0 + jnp.exp(-logits))
    sel = scores + rb_ref[...]
    neg = -jnp.inf
    iota_g = lax.broadcasted_iota(jnp.int32, (GROUP_SIZE, tm), 0)
    iota_n = lax.broadcasted_iota(jnp.int32, (N_GROUPS, tm), 0)

    def first_argmax(v, iota, big):
        m = jnp.max(v, axis=0, keepdims=True)
        return m, jnp.min(jnp.where(v == m, iota, big), axis=0, keepdims=True)

    grp_score = jnp.zeros((N_GROUPS, tm), jnp.float32)
    for g in range(N_GROUPS):
        blk = sel[g * GROUP_SIZE:(g + 1) * GROUP_SIZE]
        m1, i1 = first_argmax(blk, iota_g, GROUP_SIZE)
        m2 = jnp.max(jnp.where(iota_g == i1, neg, blk), axis=0, keepdims=True)
        grp_score = jnp.where(iota_n == g, m1 + m2, grp_score)

    grp_on = jnp.zeros((N_GROUPS, tm), jnp.float32)
    work = grp_score
    for _ in range(TOP_GROUPS):
        _, gi = first_argmax(work, iota_n, N_GROUPS)
        hit = iota_n == gi
        grp_on = jnp.where(hit, 1.0, grp_on)
        work = jnp.where(hit, neg, work)

    masked = jnp.concatenate(
        [jnp.where(grp_on[g:g + 1] > 0.0, sel[g * GROUP_SIZE:(g + 1) * GROUP_SIZE], neg)
         for g in range(N_GROUPS)], axis=0)
    iota_e = lax.broadcasted_iota(jnp.int32, (N_EXPERTS, tm), 0)
    iota_k = lax.broadcasted_iota(jnp.int32, (TOP_K, tm), 0)
    top_idx = jnp.zeros((TOP_K, tm), jnp.int32)
    top_gate = jnp.zeros((TOP_K, tm), jnp.float32)
    hits = []
    for k in range(TOP_K):
        _, ei = first_argmax(masked, iota_e, N_EXPERTS)
        hit = iota_e == ei
        hits.append(hit)
        gk = jnp.sum(jnp.where(hit, scores, 0.0), axis=0, keepdims=True)
        masked = jnp.where(hit, neg, masked)
        top_idx = jnp.where(iota_k == k, ei, top_idx)
        top_gate = jnp.where(iota_k == k, gk, top_gate)
    gate_ref[...] = top_gate / jnp.sum(top_gate, axis=0, keepdims=True) * ROUTE_SCALE

    @pl.when(pl.program_id(0) == 0)
    def _():
        run_ref[...] = jnp.zeros_like(run_ref)

    onehot = jnp.zeros((N_EXPERTS, tm), jnp.float32)
    for hit in hits:
        onehot = jnp.where(hit, 1.0, onehot)
    before = jnp.dot(onehot.astype(jnp.bfloat16), tri_ref[...], preferred_element_type=jnp.float32)

    @pl.when(pl.program_id(0) == 0)
    def _():
        tstart_ref[...] = jnp.zeros_like(tstart_ref)
        tcnt_ref[...] = jnp.zeros_like(tcnt_ref)

    lane = lax.broadcasted_iota(jnp.int32, (1, tm), 1)
    tile_col = lax.broadcasted_iota(jnp.int32, tstart_ref.shape, 1)
    subs = tm // MOE_TILE
    slot = before
    seen = jnp.zeros((N_EXPERTS, 1), jnp.float32)
    for s in range(subs):
        in_sub = (lane >= s * MOE_TILE) & (lane < (s + 1) * MOE_TILE)
        sub_cnt = jnp.sum(jnp.where(in_sub, onehot, 0.0), axis=1, keepdims=True)
        seg_chunks = jnp.floor((sub_cnt + (SEG_ROWS - 1)) * (1.0 / SEG_ROWS))
        seg_off = SEG_ROWS * jnp.dot(etri_ref[...],
                                     jnp.broadcast_to(seg_chunks, (N_EXPERTS, LANES)).astype(jnp.bfloat16),
                                     preferred_element_type=jnp.float32)[:, :1]
        slot = jnp.where(in_sub, slot + (seg_off - seen), slot)
        this_tile = tile_col == pl.program_id(0) * subs + s
        tstart_ref[...] = jnp.where(this_tile, (run_ref[...] + seen).astype(jnp.int32), tstart_ref[...])
        tcnt_ref[...] = jnp.where(this_tile, sub_cnt.astype(jnp.int32), tcnt_ref[...])
        seen = seen + sub_cnt
    loc = jnp.zeros((TOP_K, tm), jnp.float32)
    for k in range(TOP_K):
        loc = jnp.where(iota_k == k, jnp.sum(jnp.where(hits[k], slot, 0.0), axis=0, keepdims=True), loc)
    loc_ref[...] = loc.astype(jnp.int32)
    run_ref[...] += seen
    cnt_ref[...] = run_ref[...].astype(jnp.int32)


def _mix_router(olat, ob, h, wuvo, wob, ln_g, ln_b, wrt, rbias, tm):
    n = h.shape[0]
    row = lambda i: (i, 0)
    c2 = lambda i: (0, 0)
    c3 = lambda i: (0, 0, 0)
    col = lambda i: (0, i)
    v_w = R_HEADS * R_V_DIM
    tiles_pad = -(-(n // MOE_TILE) // LANES) * LANES
    tri = jnp.triu(jnp.ones((tm, tm), jnp.bfloat16), k=1)
    etri = jnp.tril(jnp.ones((N_EXPERTS, N_EXPERTS), jnp.bfloat16), k=-1)
    return pl.pallas_call(
        functools.partial(_mix_router_kernel, tm=tm),
        out_shape=(jax.ShapeDtypeStruct((n, D_MODEL), jnp.float32),
                   jax.ShapeDtypeStruct((n, D_MODEL), jnp.bfloat16),
                   jax.ShapeDtypeStruct((n * SLABS, LANES), jnp.float32),
                   jax.ShapeDtypeStruct((TOP_K, n), jnp.float32),
                   jax.ShapeDtypeStruct((TOP_K, n), jnp.int32),
                   jax.ShapeDtypeStruct((N_EXPERTS, 1), jnp.int32),
                   jax.ShapeDtypeStruct((N_EXPERTS, tiles_pad), jnp.int32),
                   jax.ShapeDtypeStruct((N_EXPERTS, tiles_pad), jnp.int32)),
        grid=(n // tm,),
        in_specs=[pl.BlockSpec((A_HEADS, tm, KV_LATENT), lambda i: (0, i, 0)),
                  pl.BlockSpec((tm, v_w), row),
                  pl.BlockSpec((tm, D_MODEL), row),
                  pl.BlockSpec((A_HEADS, KV_LATENT, D_MODEL), c3),
                  pl.BlockSpec((v_w, D_MODEL), c2),
                  pl.BlockSpec((1, D_MODEL), c2),
                  pl.BlockSpec((1, D_MODEL), c2),
                  pl.BlockSpec((N_EXPERTS, D_MODEL), c2),
                  pl.BlockSpec((N_EXPERTS, 1), c2),
                  pl.BlockSpec((tm, tm), c2),
                  pl.BlockSpec((N_EXPERTS, N_EXPERTS), c2)],
        out_specs=(pl.BlockSpec((tm, D_MODEL), row),
                   pl.BlockSpec((tm, D_MODEL), row),
                   pl.BlockSpec((tm * SLABS, LANES), row),
                   pl.BlockSpec((TOP_K, tm), col),
                   pl.BlockSpec((TOP_K, tm), col),
                   pl.BlockSpec((N_EXPERTS, 1), c2),
                   pl.BlockSpec((N_EXPERTS, tiles_pad), c2),
                   pl.BlockSpec((N_EXPERTS, tiles_pad), c2)),
        scratch_shapes=[pltpu.VMEM((N_EXPERTS, 1), jnp.float32)],
        compiler_params=_cparams(("arbitrary",)),
        name="mix_ln_router",
    )(olat, ob, h, wuvo, wob, ln_g, ln_b, wrt, rbias, tri, etri)


def _row_copy(src, dst, sem):
    return pltpu.make_async_copy(src, dst, sem)


def _chunk_tables(tstart, tcnt, cap, max_chunks):
    chunks = (tcnt + SEG_ROWS - 1) // SEG_ROWS
    cum = jnp.cumsum(chunks, axis=1)
    first = cum - chunks
    c = jnp.arange(max_chunks, dtype=jnp.int32)
    expert = jnp.minimum(jnp.sum((cum[:, None, :] <= c[None, :, None]).astype(jnp.int32), axis=2), N_EXPERTS - 1)
    is_e = expert[:, :, None] == jnp.arange(N_EXPERTS, dtype=jnp.int32)[None, None, :]
    pick = lambda a: jnp.sum(jnp.where(is_e, a[:, None, :], 0), axis=2)
    local = c[None, :] - pick(first)
    stage_row = (pick(first) + local) * SEG_ROWS
    buffer_row = expert * cap + pick(tstart) + local * SEG_ROWS
    as_smem = lambda a: a.astype(jnp.int32)[:, None, :]
    return cum[:, -1].astype(jnp.int32), as_smem(stage_row), as_smem(buffer_row)


def _dispatch_kernel(nchunk_ref, pad_lo_ref, pad_hi_ref, loc_ref, cstage_ref, crow_ref, x3_ref, rows_ref,
                     stage_ref, zero_ref, sem, *, td):
    i = pl.program_id(0)
    last = pl.num_programs(0) - 1
    slot = i % 2

    @pl.when(i == 0)
    def _():
        stage_ref[...] = jnp.zeros_like(stage_ref)

    def place(j, carry):
        row = x3_ref[j]
        for k in range(TOP_K):
            stage_ref[slot, loc_ref[k, j]] = row
        return carry

    lax.fori_loop(0, td, place, 0)

    def drain(n_chunks, which):
        def one(c, carry):
            _row_copy(stage_ref.at[which, pl.ds(0, SEG_ROWS)], rows_ref.at[pl.ds(0, SEG_ROWS)], sem.at[which]).wait()
            return carry
        lax.fori_loop(0, n_chunks, one, 0)

    @pl.when(i > 0)
    def _():
        drain(nchunk_ref[jnp.maximum(i - 1, 0)], 1 - slot)

    def send(c, carry):
        _row_copy(stage_ref.at[slot, pl.ds(cstage_ref[0, 0, c], SEG_ROWS)],
                  rows_ref.at[pl.ds(crow_ref[0, 0, c], SEG_ROWS)], sem.at[slot]).start()
        return carry

    lax.fori_loop(0, nchunk_ref[i], send, 0)

    @pl.when(i == last)
    def _():
        drain(nchunk_ref[i], slot)
        zero_ref[...] = jnp.zeros_like(zero_ref)

        def fill(e, carry):
            def one(r, c):
                _row_copy(zero_ref, rows_ref.at[r], sem.at[0]).start()
                return c
            return lax.fori_loop(pad_lo_ref[e], pad_hi_ref[e], one, carry)

        def fill_wait(e, carry):
            def one(r, c):
                _row_copy(zero_ref, rows_ref.at[r], sem.at[0]).wait()
                return c
            return lax.fori_loop(pad_lo_ref[e], pad_hi_ref[e], one, carry)

        lax.fori_loop(0, N_EXPERTS, fill, 0)
        lax.fori_loop(0, N_EXPERTS, fill_wait, 0)


def _stage_rows(td):
    return -(-(td * TOP_K + N_EXPERTS * (SEG_ROWS - 1)) // SEG_ROWS) * SEG_ROWS


def _max_chunks(td):
    return td * TOP_K // SEG_ROWS + N_EXPERTS


def _dispatch(n_chunks, pad_lo, pad_hi, loc, chunk_stage, chunk_row, x3, n_rows, td):
    n = loc.shape[1]
    col = pl.BlockSpec((TOP_K, td), lambda i, *_: (0, i), memory_space=pltpu.SMEM)
    per_tile = pl.BlockSpec((1, 1, _max_chunks(td)), lambda i, *_: (i, 0, 0), memory_space=pltpu.SMEM)
    return pl.pallas_call(
        functools.partial(_dispatch_kernel, td=td),
        out_shape=jax.ShapeDtypeStruct((n_rows, SLABS, LANES), jnp.float32),
        grid_spec=pltpu.PrefetchScalarGridSpec(
            num_scalar_prefetch=3,
            grid=(n // td,),
            in_specs=[col, per_tile, per_tile,
                      pl.BlockSpec((td, SLABS, LANES), lambda i, *_: (i, 0, 0))],
            out_specs=pl.BlockSpec(memory_space=pl.ANY),
            scratch_shapes=[pltpu.VMEM((2, _stage_rows(td), SLABS, LANES), jnp.float32),
                            pltpu.VMEM((SLABS, LANES), jnp.float32),
                            pltpu.SemaphoreType.DMA((2,))]),
        compiler_params=pltpu.CompilerParams(dimension_semantics=("arbitrary",), has_side_effects=True,
                                             vmem_limit_bytes=VMEM_LIMIT),
        name="moe_dispatch",
    )(n_chunks, pad_lo, pad_hi, loc, chunk_stage, chunk_row, x3)


def _expert_kernel(be_ref, br_ref, nu_ref, x_ref, wg_ref, wu_ref, wd_ref, y_ref, wgu_scr, wd_scr):
    i = pl.program_id(0)

    @pl.when(i < nu_ref[0])
    def _():
        @pl.when((i == 0) | (be_ref[i] != be_ref[jnp.maximum(i - 1, 0)]))
        def _():
            wgu_scr[:, :D_EXPERT] = wg_ref[0, 0].astype(wgu_scr.dtype)
            wgu_scr[:, D_EXPERT:] = wu_ref[0, 0].astype(wgu_scr.dtype)
            wd_scr[...] = wd_ref[0, 0].astype(wd_scr.dtype)

        pair = 2 * LANES
        gu = jnp.zeros((ROW_BLOCK, 2 * D_EXPERT), jnp.float32)
        for s in range(0, SLABS, 2):
            xs = jnp.concatenate([x_ref[pl.ds(s, ROW_BLOCK, stride=SLABS), :],
                                  x_ref[pl.ds(s + 1, ROW_BLOCK, stride=SLABS), :]], axis=1).astype(jnp.bfloat16)
            gu = gu + jnp.dot(xs, wgu_scr[s * LANES:s * LANES + pair, :], preferred_element_type=jnp.float32)
        g, u = gu[:, :D_EXPERT], gu[:, D_EXPERT:]
        hdn = (g / (1.0 + jnp.exp(-g)) * u).astype(jnp.bfloat16)
        y = jnp.dot(hdn, wd_scr[...], preferred_element_type=jnp.float32)
        for s in range(SLABS):
            y_ref[pl.ds(s, ROW_BLOCK, stride=SLABS), :] = y[:, s * LANES:(s + 1) * LANES]


def _experts(block_expert, block_row, n_used, x_rows2, wg, wu, wd, layer):
    n_blocks = block_expert.shape[0]
    blk = lambda i, be, br, nu: (br[jnp.minimum(i, nu[0] - 1)], 0)
    wsel = lambda i, be, br, nu: (layer, be[jnp.minimum(i, nu[0] - 1)], 0, 0)
    return pl.pallas_call(
        _expert_kernel,
        out_shape=jax.ShapeDtypeStruct(x_rows2.shape, jnp.float32),
        grid_spec=pltpu.PrefetchScalarGridSpec(
            num_scalar_prefetch=3,
            grid=(n_blocks,),
            in_specs=[pl.BlockSpec((ROW_BLOCK * SLABS, LANES), blk),
                      pl.BlockSpec((1, 1, D_MODEL, D_EXPERT), wsel),
                      pl.BlockSpec((1, 1, D_MODEL, D_EXPERT), wsel),
                      pl.BlockSpec((1, 1, D_EXPERT, D_MODEL), wsel)],
            out_specs=pl.BlockSpec((ROW_BLOCK * SLABS, LANES), blk),
            scratch_shapes=[pltpu.VMEM((D_MODEL, 2 * D_EXPERT), jnp.bfloat16),
                            pltpu.VMEM((D_EXPERT, D_MODEL), jnp.bfloat16)]),
        compiler_params=_cparams(("arbitrary",)),
        name="moe_experts",
    )(block_expert, block_row, n_used, x_rows2, wg, wu, wd)


def _combine_kernel(nchunk_ref, loc_ref, gate_ref, cstage_ref, crow_ref, cstage_next_ref, crow_next_ref,
                    y3_ref, h1_ref, h1b_ref, wsg_ref, wsu_ref, wsd_ref, g_ref, b_ref, h2_ref, h2b_ref,
                    buf_ref, comb_ref, sem, *, tc):
    i = pl.program_id(0)
    last = pl.num_programs(0) - 1
    slot = i % 2

    def fetch(stage_tbl, row_tbl, n_chunks, which):
        def one(c, carry):
            _row_copy(y3_ref.at[pl.ds(row_tbl[0, 0, c], SEG_ROWS)],
                      buf_ref.at[which, pl.ds(stage_tbl[0, 0, c], SEG_ROWS)], sem.at[which]).start()
            return carry
        lax.fori_loop(0, n_chunks, one, 0)

    @pl.when(i == 0)
    def _():
        fetch(cstage_ref, crow_ref, nchunk_ref[0], 0)

    @pl.when(i < last)
    def _():
        fetch(cstage_next_ref, crow_next_ref, nchunk_ref[jnp.minimum(i + 1, last)], 1 - slot)

    xb = h1b_ref[...]
    gs = jnp.dot(xb, wsg_ref[...], preferred_element_type=jnp.float32)
    us = jnp.dot(xb, wsu_ref[...], preferred_element_type=jnp.float32)
    hs = (gs / (1.0 + jnp.exp(-gs)) * us).astype(jnp.bfloat16)
    shared = jnp.dot(hs, wsd_ref[...], preferred_element_type=jnp.float32)

    def drain_chunk(c, carry):
        _row_copy(y3_ref.at[pl.ds(0, SEG_ROWS)], buf_ref.at[slot, pl.ds(0, SEG_ROWS)], sem.at[slot]).wait()
        return carry

    lax.fori_loop(0, nchunk_ref[i], drain_chunk, 0)

    def weigh(j, carry):
        acc = gate_ref[0, j] * buf_ref[slot, loc_ref[0, j]]
        for k in range(1, TOP_K):
            acc = acc + gate_ref[k, j] * buf_ref[slot, loc_ref[k, j]]
        comb_ref[pl.ds(pl.multiple_of(j * SLABS, SLABS), SLABS), :] = acc
        return carry

    lax.fori_loop(0, tc, weigh, 0)
    routed = jnp.concatenate([comb_ref[pl.ds(s, tc, stride=SLABS), :] for s in range(SLABS)], axis=1)
    h2 = _layer_norm(DN_ALPHA * h1_ref[...] + (routed + shared), g_ref[...], b_ref[...])
    h2_ref[...] = h2
    h2b_ref[...] = h2.astype(h2b_ref.dtype)


def _combine(n_chunks, loc, gates, chunk_stage, chunk_row, y3, h1, h1b, wsg, wsu, wsd, ln_g, ln_b, tc):
    n = h1.shape[0]
    row = lambda i, *_: (i, 0)
    c2 = lambda i, *_: (0, 0)
    smem_col = pl.BlockSpec((TOP_K, tc), lambda i, *_: (0, i), memory_space=pltpu.SMEM)
    per_tile = pl.BlockSpec((1, 1, _max_chunks(tc)), lambda i, *_: (i, 0, 0), memory_space=pltpu.SMEM)
    next_tile = pl.BlockSpec((1, 1, _max_chunks(tc)), lambda i, *_: (jnp.minimum(i + 1, n // tc - 1), 0, 0),
                             memory_space=pltpu.SMEM)
    return pl.pallas_call(
        functools.partial(_combine_kernel, tc=tc),
        out_shape=(jax.ShapeDtypeStruct((n, D_MODEL), jnp.float32),
                   jax.ShapeDtypeStruct((n, D_MODEL), jnp.bfloat16)),
        grid_spec=pltpu.PrefetchScalarGridSpec(
            num_scalar_prefetch=1,
            grid=(n // tc,),
            in_specs=[smem_col, smem_col, per_tile, per_tile, next_tile, next_tile,
                      pl.BlockSpec(memory_space=pl.ANY),
                      pl.BlockSpec((tc, D_MODEL), row),
                      pl.BlockSpec((tc, D_MODEL), row),
                      pl.BlockSpec((D_MODEL, D_SHARED), c2),
                      pl.BlockSpec((D_MODEL, D_SHARED), c2),
                      pl.BlockSpec((D_SHARED, D_MODEL), c2),
                      pl.BlockSpec((1, D_MODEL), c2),
                      pl.BlockSpec((1, D_MODEL), c2)],
            out_specs=(pl.BlockSpec((tc, D_MODEL), row),
                       pl.BlockSpec((tc, D_MODEL), row)),
            scratch_shapes=[pltpu.VMEM((2, _stage_rows(tc), SLABS, LANES), jnp.float32),
                            pltpu.VMEM((tc * SLABS, LANES), jnp.float32),
                            pltpu.SemaphoreType.DMA((2,))]),
        compiler_params=_cparams(("arbitrary",)),
        name="moe_combine",
    )(n_chunks, loc, gates, chunk_stage, chunk_row, chunk_stage, chunk_row, y3, h1, h1b, wsg, wsu, wsd, ln_g, ln_b)


def _routing_plan(counts, n_blocks, cap):
    blocks_per_e = (counts + ROW_BLOCK - 1) // ROW_BLOCK
    blk_end = jnp.cumsum(blocks_per_e)
    blk_start = blk_end - blocks_per_e
    block_ids = jnp.arange(n_blocks, dtype=jnp.int32)
    block_expert = jnp.minimum(jnp.sum((blk_end[None, :] <= block_ids[:, None]).astype(jnp.int32), axis=1),
                               N_EXPERTS - 1)
    start_of = jnp.sum(jnp.where(block_expert[:, None] == jnp.arange(N_EXPERTS, dtype=jnp.int32)[None, :],
                                 blk_start[None, :], 0), axis=1)
    block_row = block_expert * (cap // ROW_BLOCK) + (block_ids - start_of)
    expert_row0 = jnp.arange(N_EXPERTS, dtype=jnp.int32) * cap
    pad_lo = expert_row0 + counts
    pad_hi = expert_row0 + blocks_per_e * ROW_BLOCK
    return block_expert, block_row.astype(jnp.int32), blk_end[-1:], pad_lo, pad_hi


def _prepare_weights(w_in, w_uk, w_uv, w_o):
    L = w_in.shape[0]
    o = IN_OFFS
    bf = jnp.bfloat16
    w_qa = w_in[:, :, o[0]:o[1]].reshape(L, D_MODEL, A_HEADS, A_HEAD_DIM).transpose(0, 2, 1, 3)
    w_qlat = _fold(w_qa, w_uk, A_HEAD_DIM ** -0.5)
    w_qlat = w_qlat.transpose(0, 2, 1, 3).reshape(L, D_MODEL, A_HEADS * KV_LATENT)
    w_oa = w_o[:, :A_HEADS * A_HEAD_DIM].reshape(L, A_HEADS, A_HEAD_DIM, D_MODEL)
    w_uvo = _fold(w_uv, w_oa, 1.0)
    w_ob = w_o[:, A_HEADS * A_HEAD_DIM:].astype(bf)

    def pad_cols(w, width):
        return jnp.pad(w, ((0, 0), (0, 0), (0, width - w.shape[-1])))

    half = R_QK_DIM // 2
    perm = np.concatenate([np.arange(half) + R_QK_DIM * h for h in range(R_HEADS)]
                          + [np.arange(half) + R_QK_DIM * h + half for h in range(R_HEADS)])
    w_cat = jnp.concatenate([
        w_qlat,
        w_in[:, :, o[1]:o[2]].astype(bf),
        w_in[:, :, o[2]:o[3]].astype(bf),
        pad_cols(w_in[:, :, o[3]:o[4]], LANES).astype(bf),
        pad_cols(w_in[:, :, o[4]:o[5]], LANES).astype(bf),
        w_in[:, :, o[5]:o[6]][:, :, perm].astype(bf),
        w_in[:, :, o[6]:o[7]][:, :, perm].astype(bf),
        w_in[:, :, o[7]:o[8]].astype(bf),
        w_in[:, :, o[8]:o[9]].astype(bf)], axis=-1)
    return w_cat, w_uvo, w_ob


def _rotary_tables(t_pad):
    half = R_QK_DIM // 2
    inv = ROPE_BASE ** (-jnp.arange(half, dtype=jnp.float32) / half)
    ang = jnp.arange(t_pad, dtype=jnp.float32)[:, None] * inv
    return jnp.tile(jnp.cos(ang), (1, R_HEADS)), jnp.tile(jnp.sin(ang), (1, R_HEADS))


def _pick_tile(n, prefer):
    for t in prefer:
        if n % t == 0:
            return t
    raise ValueError(f"no tile for {n}")


def kernel(x, meta_tokens, w_in, w_uk, w_uv, kv_norm_w, ret_norm_w, w_o, ln1_g, ln1_b, w_router, router_bias,
           w_gate, w_up, w_down, ws_gate, ws_up, ws_down, ln2_g, ln2_b):
    b, s, d = x.shape
    assert d == D_MODEL
    L = w_in.shape[0]
    t = s + N_META
    n_sel = min(TOPK_MAX, s // 4)
    t_pad = -(-t // LANES) * LANES
    n = b * t_pad
    bf = jnp.bfloat16

    meta = jnp.broadcast_to(meta_tokens.astype(x.dtype)[None], (b, N_META, d))
    h = jnp.concatenate([meta, x, jnp.zeros((b, t_pad - t, d), x.dtype)], axis=1).reshape(n, d)
    hb = h.astype(bf)

    w_cat, w_uvo, w_ob = _prepare_weights(w_in, w_uk, w_uv, w_o)
    cos_t, sin_t = _rotary_tables(t_pad)
    wsg, wsu, wsd = ws_gate.astype(bf), ws_up.astype(bf), ws_down.astype(bf)
    w_rt = jnp.swapaxes(w_router, 1, 2)

    tm = _pick_tile(n, (2 * MOE_TILE, MOE_TILE))
    n_tiles = n // MOE_TILE
    n_blocks = -(-(n * TOP_K) // ROW_BLOCK) + N_EXPERTS
    cap = (-(-n // ROW_BLOCK) + 1) * ROW_BLOCK
    n_rows = N_EXPERTS * cap

    for l in range(L):
        qlat, ckv, qi, ki, wi, qr, kr, vr, sg = _inproj(hb, w_cat[l], kv_norm_w[l][None], cos_t, sin_t, t_pad)
        olat = _attention(qlat, qi, wi, ckv, ki, b, t_pad, n_sel)
        ob = _retention(qr, kr, vr, sg, ret_norm_w[l][None], b, t_pad)
        h1, h1b, x3, gates, loc, counts, tstart, tcnt = _mix_router(
            olat, ob, h, w_uvo[l], w_ob[l], ln1_g[l][None], ln1_b[l][None], w_rt[l], router_bias[l][:, None], tm)
        n_chunks, chunk_stage, chunk_row = _chunk_tables(tstart[:, :n_tiles].T, tcnt[:, :n_tiles].T, cap,
                                                         _max_chunks(MOE_TILE))
        block_expert, block_row, n_used, pad_lo, pad_hi = _routing_plan(counts[:, 0], n_blocks, cap)
        x_rows = _dispatch(n_chunks, pad_lo, pad_hi, loc, chunk_stage, chunk_row, x3.reshape(n, SLABS, LANES),
                           n_rows, MOE_TILE)
        y_rows = _experts(block_expert, block_row, n_used, x_rows.reshape(n_rows * SLABS, LANES),
                          w_gate, w_up, w_down, l)
        h, hb = _combine(n_chunks, loc, gates, chunk_stage, chunk_row, y_rows.reshape(n_rows, SLABS, LANES), h1, h1b,
                         wsg[l], wsu[l], wsd[l], ln2_g[l][None], ln2_b[l][None], MOE_TILE)
    return h.reshape(b, t_pad, d)[:, N_META:t]
```

```python
import functools
import math

import numpy as np
import jax
import jax.numpy as jnp
from jax import lax
from jax.experimental import pallas as pl
from jax.experimental.pallas import tpu as pltpu

D_MODEL = 1024
N_META = 16
A_HEADS = 8
A_HEAD_DIM = 64
KV_LATENT = 128
IDX_HEADS = 8
IDX_DIM = 64
TOPK_MAX = 256
R_HEADS = 4
R_QK_DIM = 64
R_V_DIM = 128
ROPE_BASE = 10000.0
N_EXPERTS = 64
N_GROUPS = 8
GROUP_SIZE = N_EXPERTS // N_GROUPS
TOP_GROUPS = 4
TOP_K = 8
D_EXPERT = 256
D_SHARED = 256
ROUTE_SCALE = 2.5
DEPTH = 4
DN_ALPHA = (2 * DEPTH) ** 0.25
EPS = 1e-6

IN_COLS = (A_HEADS * A_HEAD_DIM, KV_LATENT, IDX_HEADS * IDX_DIM, IDX_DIM, IDX_HEADS,
           R_HEADS * R_QK_DIM, R_HEADS * R_QK_DIM, R_HEADS * R_V_DIM, R_HEADS * R_V_DIM)
IN_OFFS = tuple(int(v) for v in np.cumsum((0,) + IN_COLS))

LANES = 128
SUBLANES = 8
Q_BLOCK = 128
KEY_CHUNK = 256
SEARCH_ALWAYS = 16
SEARCH_GROUP = 4
R_CHUNK = 128
ROW_BLOCK = 512
MOE_TILE = 256
SEG_ROWS = 32
SLABS = D_MODEL // LANES
VMEM_LIMIT = 56 * 1024 * 1024

C_QLAT = 0
C_CKV = C_QLAT + A_HEADS * KV_LATENT
C_QI = C_CKV + KV_LATENT
C_KI = C_QI + IDX_HEADS * IDX_DIM
C_WI = C_KI + LANES
C_QR = C_WI + LANES
C_KR = C_QR + R_HEADS * R_QK_DIM
C_VR = C_KR + R_HEADS * R_QK_DIM
C_GR = C_VR + R_HEADS * R_V_DIM
C_END = C_GR + R_HEADS * R_V_DIM

INT_MIN = -2 ** 31
NEG_BIG = -1e30

_NT = (((1,), (1,)), ((), ()))
_TN = (((0,), (0,)), ((), ()))


def _cparams(sem):
    return pltpu.CompilerParams(dimension_semantics=sem, vmem_limit_bytes=VMEM_LIMIT)


def _fold_kernel(a_ref, b_ref, o_ref, *, scale):
    o_ref[0, 0] = (jnp.dot(a_ref[0, 0], b_ref[0, 0], preferred_element_type=jnp.float32,
                           precision=lax.Precision.HIGHEST) * scale).astype(o_ref.dtype)


def _fold(a, b, scale):
    L, H, M, K = a.shape
    N = b.shape[-1]
    return pl.pallas_call(
        functools.partial(_fold_kernel, scale=scale),
        out_shape=jax.ShapeDtypeStruct((L, H, M, N), jnp.bfloat16),
        grid=(L, H),
        in_specs=[pl.BlockSpec((1, 1, M, K), lambda l, h: (l, h, 0, 0)),
                  pl.BlockSpec((1, 1, K, N), lambda l, h: (l, h, 0, 0))],
        out_specs=pl.BlockSpec((1, 1, M, N), lambda l, h: (l, h, 0, 0)),
        compiler_params=_cparams(("parallel", "parallel")),
        name="weight_fold",
    )(a, b)


def _inproj_kernel(x_ref, w_ref, kvg_ref, cos_ref, sin_ref,
                   qlat_ref, ckv_ref, qi_ref, ki_ref, wi_ref, qr_ref, kr_ref, vr_ref, sg_ref):
    x = x_ref[...]

    def proj(lo, hi):
        return jnp.dot(x, w_ref[:, lo:hi], preferred_element_type=jnp.float32)

    r = proj(C_QLAT, C_CKV)
    for h in range(A_HEADS):
        qlat_ref[h] = r[:, h * KV_LATENT:(h + 1) * KV_LATENT].astype(qlat_ref.dtype)

    r = proj(C_CKV, C_QI)
    r = r * lax.rsqrt(jnp.mean(r * r, axis=-1, keepdims=True) + EPS) * kvg_ref[...]
    ckv_ref[...] = r.astype(ckv_ref.dtype)

    r = proj(C_QI, C_KI)
    for h in range(IDX_HEADS):
        qi_ref[h] = r[:, h * IDX_DIM:(h + 1) * IDX_DIM].astype(qi_ref.dtype)

    r = proj(C_KI, C_WI)
    ki_ref[...] = r[:, :IDX_DIM].astype(ki_ref.dtype)
    r = proj(C_WI, C_QR)
    wi_ref[...] = r[:, :IDX_HEADS] * (IDX_HEADS ** -0.5)

    cos = cos_ref[...]
    sin = sin_ref[...]
    half = R_HEADS * R_QK_DIM // 2

    def rot(r, out_ref, scale):
        x1, x2 = r[:, :half], r[:, half:]
        out_ref[:, :half] = ((x1 * cos - x2 * sin) * scale).astype(out_ref.dtype)
        out_ref[:, half:] = ((x1 * sin + x2 * cos) * scale).astype(out_ref.dtype)

    rot(proj(C_QR, C_KR), qr_ref, 1.0)
    rot(proj(C_KR, C_VR), kr_ref, R_QK_DIM ** -0.5)
    vr_ref[...] = proj(C_VR, C_GR).astype(vr_ref.dtype)
    g = proj(C_GR, C_END)
    sg_ref[...] = (g / (1.0 + jnp.exp(-g))).astype(sg_ref.dtype)


def _inproj(xb, w_cat, kv_g, cos_t, sin_t, t_pad):
    n = xb.shape[0]
    tm = t_pad // 4
    per_b = t_pad // tm
    bf = jnp.bfloat16
    row = lambda i: (i, 0)
    head = lambda i: (0, i, 0)
    const = lambda i: (0, 0)
    pos = lambda i: (i % per_b, 0)
    return pl.pallas_call(
        _inproj_kernel,
        out_shape=(jax.ShapeDtypeStruct((A_HEADS, n, KV_LATENT), bf),
                   jax.ShapeDtypeStruct((n, KV_LATENT), bf),
                   jax.ShapeDtypeStruct((IDX_HEADS, n, IDX_DIM), bf),
                   jax.ShapeDtypeStruct((n, IDX_DIM), bf),
                   jax.ShapeDtypeStruct((n, IDX_HEADS), jnp.float32),
                   jax.ShapeDtypeStruct((n, R_HEADS * R_QK_DIM), bf),
                   jax.ShapeDtypeStruct((n, R_HEADS * R_QK_DIM), bf),
                   jax.ShapeDtypeStruct((n, R_HEADS * R_V_DIM), bf),
                   jax.ShapeDtypeStruct((n, R_HEADS * R_V_DIM), bf)),
        grid=(n // tm,),
        in_specs=[pl.BlockSpec((tm, D_MODEL), row),
                  pl.BlockSpec((D_MODEL, C_END), const),
                  pl.BlockSpec((1, KV_LATENT), const),
                  pl.BlockSpec((tm, LANES), pos),
                  pl.BlockSpec((tm, LANES), pos)],
        out_specs=(pl.BlockSpec((A_HEADS, tm, KV_LATENT), head),
                   pl.BlockSpec((tm, KV_LATENT), row),
                   pl.BlockSpec((IDX_HEADS, tm, IDX_DIM), head),
                   pl.BlockSpec((tm, IDX_DIM), row),
                   pl.BlockSpec((tm, IDX_HEADS), row),
                   pl.BlockSpec((tm, R_HEADS * R_QK_DIM), row),
                   pl.BlockSpec((tm, R_HEADS * R_QK_DIM), row),
                   pl.BlockSpec((tm, R_HEADS * R_V_DIM), row),
                   pl.BlockSpec((tm, R_HEADS * R_V_DIM), row)),
        compiler_params=_cparams(("parallel",)),
        name="in_proj",
    )(xb, w_cat, kv_g, cos_t, sin_t)


def _fold_rows(x, op):
    out = x[:SUBLANES]
    for r in range(1, x.shape[0] // SUBLANES):
        out = op(out, x[r * SUBLANES:(r + 1) * SUBLANES])
    return out


def _attn_kernel(qlat_ref, qi_ref, wit_ref, ckv_ref, ckvt_ref, ki_ref, o_ref, key_scr, lg_scr, acc_scr,
                 *, n_sel, t_pad):
    i = pl.program_id(1)
    nk = ((i + 1) * Q_BLOCK + KEY_CHUNK - 1) // KEY_CHUNK
    tile = (KEY_CHUNK, Q_BLOCK)
    cols_all = A_HEADS * Q_BLOCK
    q_all = qlat_ref[...].reshape(cols_all, KV_LATENT)
    qi_all = qi_ref[...].reshape(IDX_HEADS * Q_BLOCK, IDX_DIM)
    wt = wit_ref[...]
    k_off = lax.broadcasted_iota(jnp.int32, tile, 0)
    q_off = lax.broadcasted_iota(jnp.int32, tile, 1)

    def key_rows(kt):
        return pl.ds(pl.multiple_of(kt * KEY_CHUNK, KEY_CHUNK), KEY_CHUNK)

    def key_pos(kt):
        return kt * KEY_CHUNK + k_off

    def over_chunks(body, carry):
        def pair(p, c):
            return body(2 * p + 1, body(2 * p, c))
        carry = lax.fori_loop(0, nk // 2, pair, carry)
        return lax.cond(nk % 2 == 1, lambda c: body(nk - 1, c), lambda c: c, carry)

    def index_tile(kt, carry):
        qk = lax.dot_general(ki_ref[key_rows(kt), :], qi_all, _NT, preferred_element_type=jnp.float32)
        score = wt[0:1] * jnp.maximum(qk[:, :Q_BLOCK], 0.0)
        for h in range(1, IDX_HEADS):
            score = score + wt[h:h + 1] * jnp.maximum(qk[:, h * Q_BLOCK:(h + 1) * Q_BLOCK], 0.0)
        bits = lax.bitcast_convert_type(score, jnp.int32)
        key = bits ^ ((bits >> 31) & jnp.int32(0x7FFFFFFF))
        key = jnp.where(score == 0.0, 0, key)
        causal = key_pos(kt) <= (i * Q_BLOCK + q_off)
        key_scr[kt] = jnp.where(causal, key, INT_MIN)
        return carry

    over_chunks(index_tile, 0)

    def count(pred):
        def body(kt, acc):
            return acc + jnp.where(pred(key_scr[kt], kt), 1.0, 0.0)
        acc = over_chunks(body, jnp.zeros(tile, jnp.float32))
        return jnp.sum(acc, axis=0, keepdims=True)

    def thr_step(it, state):
        t, n_ge = state
        cand = t ^ lax.shift_left(jnp.int32(1), 31 - it)
        n_cand = count(lambda k, kt: k >= cand)
        keep = n_cand >= n_sel
        return jnp.where(keep, cand, t), jnp.where(keep, n_cand, n_ge)

    n_real = count(lambda k, kt: k != INT_MIN)
    state = (jnp.full((1, Q_BLOCK), INT_MIN, jnp.int32),
             jnp.zeros((1, Q_BLOCK), jnp.float32) + (nk * KEY_CHUNK).astype(jnp.float32))
    state = lax.fori_loop(0, SEARCH_ALWAYS, thr_step, state)

    def unsettled(state):
        t, n_ge = state
        return jnp.max(jnp.where((n_ge == n_sel) | (n_real < n_sel), 0.0, 1.0)) > 0.0

    def more_bits(carry):
        it, state = carry
        for _ in range(SEARCH_GROUP):
            state = thr_step(it, state)
            it = it + 1
        return it, state

    _, (t, _) = lax.while_loop(lambda c: (c[0] < 32) & unsettled(c[1]), more_bits, (jnp.int32(SEARCH_ALWAYS), state))

    need = n_sel - count(lambda k, kt: k > t)
    n_eq = count(lambda k, kt: k == t)
    idx_bits = max(1, (t_pad - 1).bit_length())
    surplus = jnp.max(jnp.where((t > INT_MIN) & (n_eq > need), 1.0, 0.0)) > 0.0

    def tie_search():
        def tie_step(it, c):
            cand = c | lax.shift_left(jnp.int32(1), idx_bits - 1 - it)
            below = count(lambda k, kt: (k == t) & (key_pos(kt) < cand))
            return jnp.where(below <= need, cand, c)
        return lax.fori_loop(0, idx_bits, tie_step, jnp.zeros((1, Q_BLOCK), jnp.int32))

    c = lax.cond(surplus, tie_search, lambda: jnp.full((1, Q_BLOCK), 1 << idx_bits, jnp.int32))

    def logit_tile(kt, m8):
        key = key_scr[kt]
        sel = ((key > t) | ((key == t) & (key_pos(kt) < c))) & (key != INT_MIN)
        lg = lax.dot_general(ckv_ref[key_rows(kt), :], q_all, _NT, preferred_element_type=jnp.float32)
        lg = jnp.where(jnp.concatenate([sel] * A_HEADS, axis=1), lg, NEG_BIG)
        lg_scr[kt] = lg
        return jnp.maximum(m8, _fold_rows(lg, jnp.maximum))

    m8 = over_chunks(logit_tile, jnp.full((SUBLANES, cols_all), NEG_BIG, jnp.float32))
    m = jnp.max(m8, axis=0, keepdims=True)
    acc_scr[...] = jnp.zeros_like(acc_scr)

    def pv_tile(kt, l8):
        p = jnp.exp(lg_scr[kt] - m)
        acc_scr[...] += jnp.dot(ckvt_ref[kt], p.astype(ckvt_ref.dtype), preferred_element_type=jnp.float32)
        return l8 + _fold_rows(p, jnp.add)

    l8 = over_chunks(pv_tile, jnp.zeros((SUBLANES, cols_all), jnp.float32))
    o_t = acc_scr[...] / jnp.sum(l8, axis=0, keepdims=True)
    for h in range(A_HEADS):
        o_ref[h] = o_t[:, h * Q_BLOCK:(h + 1) * Q_BLOCK].T.astype(o_ref.dtype)


def _attention(qlat, qi, wi, ckv, ki, batch, t_pad, n_sel):
    n = ckv.shape[0]
    nq = t_pad // Q_BLOCK
    nkc = -(-t_pad // KEY_CHUNK)
    t_keys = nkc * KEY_CHUNK

    def pad_keys(a):
        a = a.reshape(batch, t_pad, a.shape[-1])
        return jnp.pad(a, ((0, 0), (0, t_keys - t_pad), (0, 0))).reshape(batch * t_keys, a.shape[-1])

    ckv_p, ki_p = pad_keys(ckv), pad_keys(ki)
    ckv_t = jnp.swapaxes(ckv_p.reshape(batch * nkc, KEY_CHUNK, KV_LATENT), 1, 2)
    qmap = lambda b, i: (0, b * nq + i, 0)
    cols_all = A_HEADS * Q_BLOCK
    scratch = [pltpu.VMEM((nkc, KEY_CHUNK, Q_BLOCK), jnp.int32),
               pltpu.VMEM((nkc, KEY_CHUNK, cols_all), jnp.float32),
               pltpu.VMEM((KV_LATENT, cols_all), jnp.float32)]
    return pl.pallas_call(
        functools.partial(_attn_kernel, n_sel=n_sel, t_pad=t_keys),
        out_shape=jax.ShapeDtypeStruct((A_HEADS, n, KV_LATENT), jnp.bfloat16),
        grid=(batch, nq),
        in_specs=[pl.BlockSpec((A_HEADS, Q_BLOCK, KV_LATENT), qmap),
                  pl.BlockSpec((IDX_HEADS, Q_BLOCK, IDX_DIM), qmap),
                  pl.BlockSpec((IDX_HEADS, Q_BLOCK), lambda b, i: (0, b * nq + i)),
                  pl.BlockSpec((t_keys, KV_LATENT), lambda b, i: (b, 0)),
                  pl.BlockSpec((nkc, KV_LATENT, KEY_CHUNK), lambda b, i: (b, 0, 0)),
                  pl.BlockSpec((t_keys, IDX_DIM), lambda b, i: (b, 0))],
        out_specs=pl.BlockSpec((A_HEADS, Q_BLOCK, KV_LATENT), qmap),
        scratch_shapes=scratch,
        compiler_params=_cparams(("parallel", "parallel")),
        name="sparse_attention",
    )(qlat, qi, wi.T, ckv_p, ckv_t, ki_p)


def _retention_tables():
    log_g = np.log1p(-np.exp(np.linspace(math.log(1.0 / 32), math.log(1.0 / 512), R_HEADS))).astype(np.float32)
    idx = np.arange(R_CHUNK, dtype=np.float32)
    diff = idx[:, None] - idx[None, :]
    decay = np.where(diff[None] >= 0, np.exp(diff[None] * log_g[:, None, None]), 0.0).astype(np.float32)
    q_decay = np.exp((idx + 1.0)[None, :] * log_g[:, None])[:, :, None].astype(np.float32)
    k_decay = np.exp((R_CHUNK - 1.0 - idx)[None, :] * log_g[:, None])[:, :, None].astype(np.float32)
    chunk_decay = np.exp(R_CHUNK * log_g).astype(np.float32)
    return decay, q_decay, k_decay, chunk_decay


def _retention_kernel(q_ref, k_ref, v_ref, sg_ref, rw_ref, dec_ref, qd_ref, kd_ref, o_ref, state_ref,
                      *, t_pad, chunk_decay):
    state_ref[...] = jnp.zeros_like(state_ref)
    qk_w = R_HEADS * R_QK_DIM
    lane = lax.broadcasted_iota(jnp.int32, (1, qk_w), 1)
    half_w = R_QK_DIM // 2
    head_mask = [((lane % (qk_w // 2)) // half_w) == h for h in range(R_HEADS)]

    def chunk(c, carry):
        r0 = pl.multiple_of(c * R_CHUNK, R_CHUNK)
        rows = pl.ds(r0, R_CHUNK)
        qc = q_ref[rows, :]
        kc = k_ref[rows, :]
        for h in range(R_HEADS):
            cols = slice(h * R_V_DIM, (h + 1) * R_V_DIM)
            qm = jnp.where(head_mask[h], qc, jnp.zeros_like(qc))
            km = jnp.where(head_mask[h], kc, jnp.zeros_like(kc))
            vh = v_ref[rows, cols]
            inner = lax.dot_general(qm, kc, _NT, preferred_element_type=jnp.float32) * dec_ref[h]
            st = state_ref[h]
            out = (jnp.dot(inner.astype(vh.dtype), vh, preferred_element_type=jnp.float32)
                   + jnp.dot(qm, st.astype(qm.dtype), preferred_element_type=jnp.float32) * qd_ref[h])
            kdec = (km.astype(jnp.float32) * kd_ref[h]).astype(km.dtype)
            state_ref[h] = chunk_decay[h] * st + lax.dot_general(kdec, vh, _TN,
                                                                  preferred_element_type=jnp.float32)
            mu = jnp.mean(out, axis=-1, keepdims=True)
            d = out - mu
            var = jnp.mean(d * d, axis=-1, keepdims=True)
            normed = d * lax.rsqrt(var + EPS) * rw_ref[:, cols]
            o_ref[rows, cols] = (sg_ref[rows, cols].astype(jnp.float32) * normed).astype(o_ref.dtype)
        return carry

    lax.fori_loop(0, t_pad // R_CHUNK, chunk, 0)


def _retention(qr, kr, vr, sg, ret_w, batch, t_pad):
    n = qr.shape[0]
    decay, q_decay, k_decay, chunk_decay = _retention_tables()
    qk_w = R_HEADS * R_QK_DIM
    v_w = R_HEADS * R_V_DIM
    per_b = lambda b: (b, 0)
    c2 = lambda b: (0, 0)
    c3 = lambda b: (0, 0, 0)
    return pl.pallas_call(
        functools.partial(_retention_kernel, t_pad=t_pad, chunk_decay=[float(v) for v in chunk_decay]),
        out_shape=jax.ShapeDtypeStruct((n, v_w), jnp.bfloat16),
        grid=(batch,),
        in_specs=[pl.BlockSpec((t_pad, qk_w), per_b),
                  pl.BlockSpec((t_pad, qk_w), per_b),
                  pl.BlockSpec((t_pad, v_w), per_b),
                  pl.BlockSpec((t_pad, v_w), per_b),
                  pl.BlockSpec((1, v_w), c2),
                  pl.BlockSpec((R_HEADS, R_CHUNK, R_CHUNK), c3),
                  pl.BlockSpec((R_HEADS, R_CHUNK, 1), c3),
                  pl.BlockSpec((R_HEADS, R_CHUNK, 1), c3)],
        out_specs=pl.BlockSpec((t_pad, v_w), per_b),
        scratch_shapes=[pltpu.VMEM((R_HEADS, qk_w, R_V_DIM), jnp.float32)],
        compiler_params=_cparams(("parallel",)),
        name="retention",
    )(qr, kr, vr, sg, ret_w, jnp.asarray(decay), jnp.asarray(q_decay), jnp.asarray(k_decay))


def _layer_norm(y, g, b):
    mu = jnp.mean(y, axis=-1, keepdims=True)
    d = y - mu
    var = jnp.mean(d * d, axis=-1, keepdims=True)
    return d * lax.rsqrt(var + EPS) * g + b


def _mix_router_kernel(olat_ref, ob_ref, h_ref, wuvo_ref, wob_ref, g_ref, b_ref, wrt_ref, rb_ref, tri_ref, etri_ref,
                       h1_ref, h1b_ref, x3_ref, gate_ref, loc_ref, cnt_ref, tstart_ref, tcnt_ref, run_ref,
                       *, tm):
    mix = jnp.dot(ob_ref[...], wob_ref[...], preferred_element_type=jnp.float32)
    for h in range(A_HEADS):
        mix = mix + jnp.dot(olat_ref[h], wuvo_ref[h], preferred_element_type=jnp.float32)
    h1 = _layer_norm(DN_ALPHA * h_ref[...] + mix, g_ref[...], b_ref[...])
    h1_ref[...] = h1
    h1b_ref[...] = h1.astype(h1b_ref.dtype)
    for s in range(SLABS):
        x3_ref[pl.ds(s, tm, stride=SLABS), :] = h1[:, s * LANES:(s + 1) * LANES]

    logits = lax.dot_general(wrt_ref[...], h1, _NT, preferred_element_type=jnp.float32,
                             precision=lax.Precision.HIGHEST)
    scores = 1.0 / (1.0 + jnp.exp(-logits))
    sel = scores + rb_ref[...]
    neg = -jnp.inf
    iota_g = lax.broadcasted_iota(jnp.int32, (GROUP_SIZE, tm), 0)
    iota_n = lax.broadcasted_iota(jnp.int32, (N_GROUPS, tm), 0)

    def first_argmax(v, iota, big):
        m = jnp.max(v, axis=0, keepdims=True)
        return m, jnp.min(jnp.where(v == m, iota, big), axis=0, keepdims=True)

    grp_score = jnp.zeros((N_GROUPS, tm), jnp.float32)
    for g in range(N_GROUPS):
        blk = sel[g * GROUP_SIZE:(g + 1) * GROUP_SIZE]
        m1, i1 = first_argmax(blk, iota_g, GROUP_SIZE)
        m2 = jnp.max(jnp.where(iota_g == i1, neg, blk), axis=0, keepdims=True)
        grp_score = jnp.where(iota_n == g, m1 + m2, grp_score)

    grp_on = jnp.zeros((N_GROUPS, tm), jnp.float32)
    work = grp_score
    for _ in range(TOP_GROUPS):
        _, gi = first_argmax(work, iota_n, N_GROUPS)
        hit = iota_n == gi
        grp_on = jnp.where(hit, 1.0, grp_on)
        work = jnp.where(hit, neg, work)

    masked = jnp.concatenate(
        [jnp.where(grp_on[g:g + 1] > 0.0, sel[g * GROUP_SIZE:(g + 1) * GROUP_SIZE], neg)
         for g in range(N_GROUPS)], axis=0)
    iota_e = lax.broadcasted_iota(jnp.int32, (N_EXPERTS, tm), 0)
    iota_k = lax.broadcasted_iota(jnp.int32, (TOP_K, tm), 0)
    top_idx = jnp.zeros((TOP_K, tm), jnp.int32)
    top_gate = jnp.zeros((TOP_K, tm), jnp.float32)
    hits = []
    for k in range(TOP_K):
        _, ei = first_argmax(masked, iota_e, N_EXPERTS)
        hit = iota_e == ei
        hits.append(hit)
        gk = jnp.sum(jnp.where(hit, scores, 0.0), axis=0, keepdims=True)
        masked = jnp.where(hit, neg, masked)
        top_idx = jnp.where(iota_k == k, ei, top_idx)
        top_gate = jnp.where(iota_k == k, gk, top_gate)
    gate_ref[...] = top_gate / jnp.sum(top_gate, axis=0, keepdims=True) * ROUTE_SCALE

    @pl.when(pl.program_id(0) == 0)
    def _():
        run_ref[...] = jnp.zeros_like(run_ref)

    onehot = jnp.zeros((N_EXPERTS, tm), jnp.float32)
    for hit in hits:
        onehot = jnp.where(hit, 1.0, onehot)
    before = jnp.dot(onehot.astype(jnp.bfloat16), tri_ref[...], preferred_element_type=jnp.float32)

    @pl.when(pl.program_id(0) == 0)
    def _():
        tstart_ref[...] = jnp.zeros_like(tstart_ref)
        tcnt_ref[...] = jnp.zeros_like(tcnt_ref)

    lane = lax.broadcasted_iota(jnp.int32, (1, tm), 1)
    tile_col = lax.broadcasted_iota(jnp.int32, tstart_ref.shape, 1)
    subs = tm // MOE_TILE
    slot = before
    seen = jnp.zeros((N_EXPERTS, 1), jnp.float32)
    for s in range(subs):
        in_sub = (lane >= s * MOE_TILE) & (lane < (s + 1) * MOE_TILE)
        sub_cnt = jnp.sum(jnp.where(in_sub, onehot, 0.0), axis=1, keepdims=True)
        seg_chunks = jnp.floor((sub_cnt + (SEG_ROWS - 1)) * (1.0 / SEG_ROWS))
        seg_off = SEG_ROWS * jnp.dot(etri_ref[...],
                                     jnp.broadcast_to(seg_chunks, (N_EXPERTS, LANES)).astype(jnp.bfloat16),
                                     preferred_element_type=jnp.float32)[:, :1]
        slot = jnp.where(in_sub, slot + (seg_off - seen), slot)
        this_tile = tile_col == pl.program_id(0) * subs + s
        tstart_ref[...] = jnp.where(this_tile, (run_ref[...] + seen).astype(jnp.int32), tstart_ref[...])
        tcnt_ref[...] = jnp.where(this_tile, sub_cnt.astype(jnp.int32), tcnt_ref[...])
        seen = seen + sub_cnt
    loc = jnp.zeros((TOP_K, tm), jnp.float32)
    for k in range(TOP_K):
        loc = jnp.where(iota_k == k, jnp.sum(jnp.where(hits[k], slot, 0.0), axis=0, keepdims=True), loc)
    loc_ref[...] = loc.astype(jnp.int32)
    run_ref[...] += seen
    cnt_ref[...] = run_ref[...].astype(jnp.int32)


def _mix_router(olat, ob, h, wuvo, wob, ln_g, ln_b, wrt, rbias, tm):
    n = h.shape[0]
    row = lambda i: (i, 0)
    c2 = lambda i: (0, 0)
    c3 = lambda i: (0, 0, 0)
    col = lambda i: (0, i)
    v_w = R_HEADS * R_V_DIM
    tiles_pad = -(-(n // MOE_TILE) // LANES) * LANES
    tri = jnp.triu(jnp.ones((tm, tm), jnp.bfloat16), k=1)
    etri = jnp.tril(jnp.ones((N_EXPERTS, N_EXPERTS), jnp.bfloat16), k=-1)
    return pl.pallas_call(
        functools.partial(_mix_router_kernel, tm=tm),
        out_shape=(jax.ShapeDtypeStruct((n, D_MODEL), jnp.float32),
                   jax.ShapeDtypeStruct((n, D_MODEL), jnp.bfloat16),
                   jax.ShapeDtypeStruct((n * SLABS, LANES), jnp.float32),
                   jax.ShapeDtypeStruct((TOP_K, n), jnp.float32),
                   jax.ShapeDtypeStruct((TOP_K, n), jnp.int32),
                   jax.ShapeDtypeStruct((N_EXPERTS, 1), jnp.int32),
                   jax.ShapeDtypeStruct((N_EXPERTS, tiles_pad), jnp.int32),
                   jax.ShapeDtypeStruct((N_EXPERTS, tiles_pad), jnp.int32)),
        grid=(n // tm,),
        in_specs=[pl.BlockSpec((A_HEADS, tm, KV_LATENT), lambda i: (0, i, 0)),
                  pl.BlockSpec((tm, v_w), row),
                  pl.BlockSpec((tm, D_MODEL), row),
                  pl.BlockSpec((A_HEADS, KV_LATENT, D_MODEL), c3),
                  pl.BlockSpec((v_w, D_MODEL), c2),
                  pl.BlockSpec((1, D_MODEL), c2),
                  pl.BlockSpec((1, D_MODEL), c2),
                  pl.BlockSpec((N_EXPERTS, D_MODEL), c2),
                  pl.BlockSpec((N_EXPERTS, 1), c2),
                  pl.BlockSpec((tm, tm), c2),
                  pl.BlockSpec((N_EXPERTS, N_EXPERTS), c2)],
        out_specs=(pl.BlockSpec((tm, D_MODEL), row),
                   pl.BlockSpec((tm, D_MODEL), row),
                   pl.BlockSpec((tm * SLABS, LANES), row),
                   pl.BlockSpec((TOP_K, tm), col),
                   pl.BlockSpec((TOP_K, tm), col),
                   pl.BlockSpec((N_EXPERTS, 1), c2),
                   pl.BlockSpec((N_EXPERTS, tiles_pad), c2),
                   pl.BlockSpec((N_EXPERTS, tiles_pad), c2)),
        scratch_shapes=[pltpu.VMEM((N_EXPERTS, 1), jnp.float32)],
        compiler_params=_cparams(("arbitrary",)),
        name="mix_ln_router",
    )(olat, ob, h, wuvo, wob, ln_g, ln_b, wrt, rbias, tri, etri)


def _row_copy(src, dst, sem):
    return pltpu.make_async_copy(src, dst, sem)


def _chunk_tables(tstart, tcnt, cap, max_chunks):
    chunks = (tcnt + SEG_ROWS - 1) // SEG_ROWS
    cum = jnp.cumsum(chunks, axis=1)
    first = cum - chunks
    c = jnp.arange(max_chunks, dtype=jnp.int32)
    expert = jnp.minimum(jnp.sum((cum[:, None, :] <= c[None, :, None]).astype(jnp.int32), axis=2), N_EXPERTS - 1)
    is_e = expert[:, :, None] == jnp.arange(N_EXPERTS, dtype=jnp.int32)[None, None, :]
    pick = lambda a: jnp.sum(jnp.where(is_e, a[:, None, :], 0), axis=2)
    local = c[None, :] - pick(first)
    stage_row = (pick(first) + local) * SEG_ROWS
    buffer_row = expert * cap + pick(tstart) + local * SEG_ROWS
    as_smem = lambda a: a.astype(jnp.int32)[:, None, :]
    return cum[:, -1].astype(jnp.int32), as_smem(stage_row), as_smem(buffer_row)


def _dispatch_kernel(nchunk_ref, pad_lo_ref, pad_hi_ref, loc_ref, cstage_ref, crow_ref, x3_ref, rows_ref,
                     stage_ref, zero_ref, sem, *, td):
    i = pl.program_id(0)
    last = pl.num_programs(0) - 1
    slot = i % 2

    @pl.when(i == 0)
    def _():
        stage_ref[...] = jnp.zeros_like(stage_ref)

    def place(j, carry):
        row = x3_ref[j]
        for k in range(TOP_K):
            stage_ref[slot, loc_ref[k, j]] = row
        return carry

    lax.fori_loop(0, td, place, 0)

    def drain(n_chunks, which):
        def one(c, carry):
            _row_copy(stage_ref.at[which, pl.ds(0, SEG_ROWS)], rows_ref.at[pl.ds(0, SEG_ROWS)], sem.at[which]).wait()
            return carry
        lax.fori_loop(0, n_chunks, one, 0)

    @pl.when(i > 0)
    def _():
        drain(nchunk_ref[jnp.maximum(i - 1, 0)], 1 - slot)

    def send(c, carry):
        _row_copy(stage_ref.at[slot, pl.ds(cstage_ref[0, 0, c], SEG_ROWS)],
                  rows_ref.at[pl.ds(crow_ref[0, 0, c], SEG_ROWS)], sem.at[slot]).start()
        return carry

    lax.fori_loop(0, nchunk_ref[i], send, 0)

    @pl.when(i == last)
    def _():
        drain(nchunk_ref[i], slot)
        zero_ref[...] = jnp.zeros_like(zero_ref)

        def fill_chunks(e):
            return (pad_hi_ref[e] - pad_lo_ref[e] + SEG_ROWS - 1) // SEG_ROWS

        def fill(e, carry):
            def one(c, inner):
                _row_copy(zero_ref, rows_ref.at[pl.ds(pad_lo_ref[e] + c * SEG_ROWS, SEG_ROWS)], sem.at[0]).start()
                return inner
            return lax.fori_loop(0, fill_chunks(e), one, carry)

        def fill_wait(e, carry):
            def one(c, inner):
                _row_copy(zero_ref, rows_ref.at[pl.ds(0, SEG_ROWS)], sem.at[0]).wait()
                return inner
            return lax.fori_loop(0, fill_chunks(e), one, carry)

        lax.fori_loop(0, N_EXPERTS, fill, 0)
        lax.fori_loop(0, N_EXPERTS, fill_wait, 0)


def _stage_rows(td):
    return -(-(td * TOP_K + N_EXPERTS * (SEG_ROWS - 1)) // SEG_ROWS) * SEG_ROWS


def _max_chunks(td):
    return td * TOP_K // SEG_ROWS + N_EXPERTS


def _dispatch(n_chunks, pad_lo, pad_hi, loc, chunk_stage, chunk_row, x3, n_rows, td):
    n = loc.shape[1]
    col = pl.BlockSpec((TOP_K, td), lambda i, *_: (0, i), memory_space=pltpu.SMEM)
    per_tile = pl.BlockSpec((1, 1, _max_chunks(td)), lambda i, *_: (i, 0, 0), memory_space=pltpu.SMEM)
    return pl.pallas_call(
        functools.partial(_dispatch_kernel, td=td),
        out_shape=jax.ShapeDtypeStruct((n_rows, SLABS, LANES), jnp.float32),
        grid_spec=pltpu.PrefetchScalarGridSpec(
            num_scalar_prefetch=3,
            grid=(n // td,),
            in_specs=[col, per_tile, per_tile,
                      pl.BlockSpec((td, SLABS, LANES), lambda i, *_: (i, 0, 0))],
            out_specs=pl.BlockSpec(memory_space=pl.ANY),
            scratch_shapes=[pltpu.VMEM((2, _stage_rows(td), SLABS, LANES), jnp.float32),
                            pltpu.VMEM((SEG_ROWS, SLABS, LANES), jnp.float32),
                            pltpu.SemaphoreType.DMA((2,))]),
        compiler_params=pltpu.CompilerParams(dimension_semantics=("arbitrary",), has_side_effects=True,
                                             vmem_limit_bytes=VMEM_LIMIT),
        name="moe_dispatch",
    )(n_chunks, pad_lo, pad_hi, loc, chunk_stage, chunk_row, x3)


def _expert_kernel(be_ref, br_ref, nu_ref, x_ref, wg_ref, wu_ref, wd_ref, y_ref, wgu_scr, wd_scr):
    i = pl.program_id(0)

    @pl.when(i < nu_ref[0])
    def _():
        @pl.when((i == 0) | (be_ref[i] != be_ref[jnp.maximum(i - 1, 0)]))
        def _():
            wgu_scr[:, :D_EXPERT] = wg_ref[0, 0].astype(wgu_scr.dtype)
            wgu_scr[:, D_EXPERT:] = wu_ref[0, 0].astype(wgu_scr.dtype)
            wd_scr[...] = wd_ref[0, 0].astype(wd_scr.dtype)

        pair = 2 * LANES
        gu = jnp.zeros((ROW_BLOCK, 2 * D_EXPERT), jnp.float32)
        for s in range(0, SLABS, 2):
            xs = jnp.concatenate([x_ref[pl.ds(s, ROW_BLOCK, stride=SLABS), :],
                                  x_ref[pl.ds(s + 1, ROW_BLOCK, stride=SLABS), :]], axis=1).astype(jnp.bfloat16)
            gu = gu + jnp.dot(xs, wgu_scr[s * LANES:s * LANES + pair, :], preferred_element_type=jnp.float32)
        g, u = gu[:, :D_EXPERT], gu[:, D_EXPERT:]
        hdn = (g / (1.0 + jnp.exp(-g)) * u).astype(jnp.bfloat16)
        y = jnp.dot(hdn, wd_scr[...], preferred_element_type=jnp.float32)
        for s in range(SLABS):
            y_ref[pl.ds(s, ROW_BLOCK, stride=SLABS), :] = y[:, s * LANES:(s + 1) * LANES]


def _experts(block_expert, block_row, n_used, x_rows2, wg, wu, wd, layer):
    n_blocks = block_expert.shape[0]
    blk = lambda i, be, br, nu: (br[jnp.minimum(i, nu[0] - 1)], 0)
    wsel = lambda i, be, br, nu: (layer, be[jnp.minimum(i, nu[0] - 1)], 0, 0)
    return pl.pallas_call(
        _expert_kernel,
        out_shape=jax.ShapeDtypeStruct(x_rows2.shape, jnp.float32),
        grid_spec=pltpu.PrefetchScalarGridSpec(
            num_scalar_prefetch=3,
            grid=(n_blocks,),
            in_specs=[pl.BlockSpec((ROW_BLOCK * SLABS, LANES), blk),
                      pl.BlockSpec((1, 1, D_MODEL, D_EXPERT), wsel),
                      pl.BlockSpec((1, 1, D_MODEL, D_EXPERT), wsel),
                      pl.BlockSpec((1, 1, D_EXPERT, D_MODEL), wsel)],
            out_specs=pl.BlockSpec((ROW_BLOCK * SLABS, LANES), blk),
            scratch_shapes=[pltpu.VMEM((D_MODEL, 2 * D_EXPERT), jnp.bfloat16),
                            pltpu.VMEM((D_EXPERT, D_MODEL), jnp.bfloat16)]),
        compiler_params=_cparams(("arbitrary",)),
        name="moe_experts",
    )(block_expert, block_row, n_used, x_rows2, wg, wu, wd)


def _combine_kernel(nchunk_ref, loc_ref, gate_ref, cstage_ref, crow_ref, cstage_next_ref, crow_next_ref,
                    y3_ref, h1_ref, h1b_ref, wsg_ref, wsu_ref, wsd_ref, g_ref, b_ref, h2_ref, h2b_ref,
                    buf_ref, comb_ref, sem, *, tc):
    i = pl.program_id(0)
    last = pl.num_programs(0) - 1
    slot = i % 2

    def fetch(stage_tbl, row_tbl, n_chunks, which):
        def one(c, carry):
            _row_copy(y3_ref.at[pl.ds(row_tbl[0, 0, c], SEG_ROWS)],
                      buf_ref.at[which, pl.ds(stage_tbl[0, 0, c], SEG_ROWS)], sem.at[which]).start()
            return carry
        lax.fori_loop(0, n_chunks, one, 0)

    @pl.when(i == 0)
    def _():
        fetch(cstage_ref, crow_ref, nchunk_ref[0], 0)

    @pl.when(i < last)
    def _():
        fetch(cstage_next_ref, crow_next_ref, nchunk_ref[jnp.minimum(i + 1, last)], 1 - slot)

    xb = h1b_ref[...]
    gs = jnp.dot(xb, wsg_ref[...], preferred_element_type=jnp.float32)
    us = jnp.dot(xb, wsu_ref[...], preferred_element_type=jnp.float32)
    hs = (gs / (1.0 + jnp.exp(-gs)) * us).astype(jnp.bfloat16)
    shared = jnp.dot(hs, wsd_ref[...], preferred_element_type=jnp.float32)

    def drain_chunk(c, carry):
        _row_copy(y3_ref.at[pl.ds(0, SEG_ROWS)], buf_ref.at[slot, pl.ds(0, SEG_ROWS)], sem.at[slot]).wait()
        return carry

    lax.fori_loop(0, nchunk_ref[i], drain_chunk, 0)

    def weigh(j, carry):
        acc = gate_ref[0, j] * buf_ref[slot, loc_ref[0, j]]
        for k in range(1, TOP_K):
            acc = acc + gate_ref[k, j] * buf_ref[slot, loc_ref[k, j]]
        comb_ref[pl.ds(pl.multiple_of(j * SLABS, SLABS), SLABS), :] = acc
        return carry

    lax.fori_loop(0, tc, weigh, 0)
    routed = jnp.concatenate([comb_ref[pl.ds(s, tc, stride=SLABS), :] for s in range(SLABS)], axis=1)
    h2 = _layer_norm(DN_ALPHA * h1_ref[...] + (routed + shared), g_ref[...], b_ref[...])
    h2_ref[...] = h2
    h2b_ref[...] = h2.astype(h2b_ref.dtype)


def _combine(n_chunks, loc, gates, chunk_stage, chunk_row, y3, h1, h1b, wsg, wsu, wsd, ln_g, ln_b, tc):
    n = h1.shape[0]
    row = lambda i, *_: (i, 0)
    c2 = lambda i, *_: (0, 0)
    smem_col = pl.BlockSpec((TOP_K, tc), lambda i, *_: (0, i), memory_space=pltpu.SMEM)
    per_tile = pl.BlockSpec((1, 1, _max_chunks(tc)), lambda i, *_: (i, 0, 0), memory_space=pltpu.SMEM)
    next_tile = pl.BlockSpec((1, 1, _max_chunks(tc)), lambda i, *_: (jnp.minimum(i + 1, n // tc - 1), 0, 0),
                             memory_space=pltpu.SMEM)
    return pl.pallas_call(
        functools.partial(_combine_kernel, tc=tc),
        out_shape=(jax.ShapeDtypeStruct((n, D_MODEL), jnp.float32),
                   jax.ShapeDtypeStruct((n, D_MODEL), jnp.bfloat16)),
        grid_spec=pltpu.PrefetchScalarGridSpec(
            num_scalar_prefetch=1,
            grid=(n // tc,),
            in_specs=[smem_col, smem_col, per_tile, per_tile, next_tile, next_tile,
                      pl.BlockSpec(memory_space=pl.ANY),
                      pl.BlockSpec((tc, D_MODEL), row),
                      pl.BlockSpec((tc, D_MODEL), row),
                      pl.BlockSpec((D_MODEL, D_SHARED), c2),
                      pl.BlockSpec((D_MODEL, D_SHARED), c2),
                      pl.BlockSpec((D_SHARED, D_MODEL), c2),
                      pl.BlockSpec((1, D_MODEL), c2),
                      pl.BlockSpec((1, D_MODEL), c2)],
            out_specs=(pl.BlockSpec((tc, D_MODEL), row),
                       pl.BlockSpec((tc, D_MODEL), row)),
            scratch_shapes=[pltpu.VMEM((2, _stage_rows(tc), SLABS, LANES), jnp.float32),
                            pltpu.VMEM((tc * SLABS, LANES), jnp.float32),
                            pltpu.SemaphoreType.DMA((2,))]),
        compiler_params=_cparams(("arbitrary",)),
        name="moe_combine",
    )(n_chunks, loc, gates, chunk_stage, chunk_row, chunk_stage, chunk_row, y3, h1, h1b, wsg, wsu, wsd, ln_g, ln_b)


def _routing_plan(counts, n_blocks, cap):
    blocks_per_e = (counts + ROW_BLOCK - 1) // ROW_BLOCK
    blk_end = jnp.cumsum(blocks_per_e)
    blk_start = blk_end - blocks_per_e
    block_ids = jnp.arange(n_blocks, dtype=jnp.int32)
    block_expert = jnp.minimum(jnp.sum((blk_end[None, :] <= block_ids[:, None]).astype(jnp.int32), axis=1),
                               N_EXPERTS - 1)
    start_of = jnp.sum(jnp.where(block_expert[:, None] == jnp.arange(N_EXPERTS, dtype=jnp.int32)[None, :],
                                 blk_start[None, :], 0), axis=1)
    block_row = block_expert * (cap // ROW_BLOCK) + (block_ids - start_of)
    expert_row0 = jnp.arange(N_EXPERTS, dtype=jnp.int32) * cap
    pad_lo = expert_row0 + counts
    pad_hi = expert_row0 + blocks_per_e * ROW_BLOCK
    return block_expert, block_row.astype(jnp.int32), blk_end[-1:], pad_lo, pad_hi


def _prepare_weights(w_in, w_uk, w_uv, w_o):
    L = w_in.shape[0]
    o = IN_OFFS
    bf = jnp.bfloat16
    w_qa = w_in[:, :, o[0]:o[1]].reshape(L, D_MODEL, A_HEADS, A_HEAD_DIM).transpose(0, 2, 1, 3)
    w_qlat = _fold(w_qa, w_uk, A_HEAD_DIM ** -0.5)
    w_qlat = w_qlat.transpose(0, 2, 1, 3).reshape(L, D_MODEL, A_HEADS * KV_LATENT)
    w_oa = w_o[:, :A_HEADS * A_HEAD_DIM].reshape(L, A_HEADS, A_HEAD_DIM, D_MODEL)
    w_uvo = _fold(w_uv, w_oa, 1.0)
    w_ob = w_o[:, A_HEADS * A_HEAD_DIM:].astype(bf)

    def pad_cols(w, width):
        return jnp.pad(w, ((0, 0), (0, 0), (0, width - w.shape[-1])))

    half = R_QK_DIM // 2
    perm = np.concatenate([np.arange(half) + R_QK_DIM * h for h in range(R_HEADS)]
                          + [np.arange(half) + R_QK_DIM * h + half for h in range(R_HEADS)])
    w_cat = jnp.concatenate([
        w_qlat,
        w_in[:, :, o[1]:o[2]].astype(bf),
        w_in[:, :, o[2]:o[3]].astype(bf),
        pad_cols(w_in[:, :, o[3]:o[4]], LANES).astype(bf),
        pad_cols(w_in[:, :, o[4]:o[5]], LANES).astype(bf),
        w_in[:, :, o[5]:o[6]][:, :, perm].astype(bf),
        w_in[:, :, o[6]:o[7]][:, :, perm].astype(bf),
        w_in[:, :, o[7]:o[8]].astype(bf),
        w_in[:, :, o[8]:o[9]].astype(bf)], axis=-1)
    return w_cat, w_uvo, w_ob


def _rotary_tables(t_pad):
    half = R_QK_DIM // 2
    inv = ROPE_BASE ** (-jnp.arange(half, dtype=jnp.float32) / half)
    ang = jnp.arange(t_pad, dtype=jnp.float32)[:, None] * inv
    return jnp.tile(jnp.cos(ang), (1, R_HEADS)), jnp.tile(jnp.sin(ang), (1, R_HEADS))


def _pick_tile(n, prefer):
    for t in prefer:
        if n % t == 0:
            return t
    raise ValueError(f"no tile for {n}")


def kernel(x, meta_tokens, w_in, w_uk, w_uv, kv_norm_w, ret_norm_w, w_o, ln1_g, ln1_b, w_router, router_bias,
           w_gate, w_up, w_down, ws_gate, ws_up, ws_down, ln2_g, ln2_b):
    b, s, d = x.shape
    assert d == D_MODEL
    L = w_in.shape[0]
    t = s + N_META
    n_sel = min(TOPK_MAX, s // 4)
    t_pad = -(-t // LANES) * LANES
    n = b * t_pad
    bf = jnp.bfloat16

    meta = jnp.broadcast_to(meta_tokens.astype(x.dtype)[None], (b, N_META, d))
    h = jnp.concatenate([meta, x, jnp.zeros((b, t_pad - t, d), x.dtype)], axis=1).reshape(n, d)
    hb = h.astype(bf)

    w_cat, w_uvo, w_ob = _prepare_weights(w_in, w_uk, w_uv, w_o)
    cos_t, sin_t = _rotary_tables(t_pad)
    wsg, wsu, wsd = ws_gate.astype(bf), ws_up.astype(bf), ws_down.astype(bf)
    w_rt = jnp.swapaxes(w_router, 1, 2)

    tm = _pick_tile(n, (2 * MOE_TILE, MOE_TILE))
    n_tiles = n // MOE_TILE
    n_blocks = -(-(n * TOP_K) // ROW_BLOCK) + N_EXPERTS
    cap = (-(-n // ROW_BLOCK) + 1) * ROW_BLOCK
    n_rows = N_EXPERTS * cap

    for l in range(L):
        qlat, ckv, qi, ki, wi, qr, kr, vr, sg = _inproj(hb, w_cat[l], kv_norm_w[l][None], cos_t, sin_t, t_pad)
        olat = _attention(qlat, qi, wi, ckv, ki, b, t_pad, n_sel)
        ob = _retention(qr, kr, vr, sg, ret_norm_w[l][None], b, t_pad)
        h1, h1b, x3, gates, loc, counts, tstart, tcnt = _mix_router(
            olat, ob, h, w_uvo[l], w_ob[l], ln1_g[l][None], ln1_b[l][None], w_rt[l], router_bias[l][:, None], tm)
        n_chunks, chunk_stage, chunk_row = _chunk_tables(tstart[:, :n_tiles].T, tcnt[:, :n_tiles].T, cap,
                                                         _max_chunks(MOE_TILE))
        block_expert, block_row, n_used, pad_lo, pad_hi = _routing_plan(counts[:, 0], n_blocks, cap)
        x_rows = _dispatch(n_chunks, pad_lo, pad_hi, loc, chunk_stage, chunk_row, x3.reshape(n, SLABS, LANES),
                           n_rows, MOE_TILE)
        y_rows = _experts(block_expert, block_row, n_used, x_rows.reshape(n_rows * SLABS, LANES),
                          w_gate, w_up, w_down, l)
        h, hb = _combine(n_chunks, loc, gates, chunk_stage, chunk_row, y_rows.reshape(n_rows, SLABS, LANES), h1, h1b,
                         wsg[l], wsu[l], wsd[l], ln2_g[l][None], ln2_b[l][None], MOE_TILE)
    return h.reshape(b, t_pad, d)[:, N_META:t]
```

```python
import functools
import math

import numpy as np
import jax
import jax.numpy as jnp
from jax import lax
from jax.experimental import pallas as pl
from jax.experimental.pallas import tpu as pltpu

D_MODEL = 1024
N_META = 16
A_HEADS = 8
A_HEAD_DIM = 64
KV_LATENT = 128
IDX_HEADS = 8
IDX_DIM = 64
TOPK_MAX = 256
R_HEADS = 4
R_QK_DIM = 64
R_V_DIM = 128
ROPE_BASE = 10000.0
N_EXPERTS = 64
N_GROUPS = 8
GROUP_SIZE = N_EXPERTS // N_GROUPS
TOP_GROUPS = 4
TOP_K = 8
D_EXPERT = 256
D_SHARED = 256
ROUTE_SCALE = 2.5
DEPTH = 4
DN_ALPHA = (2 * DEPTH) ** 0.25
EPS = 1e-6

IN_COLS = (A_HEADS * A_HEAD_DIM, KV_LATENT, IDX_HEADS * IDX_DIM, IDX_DIM, IDX_HEADS,
           R_HEADS * R_QK_DIM, R_HEADS * R_QK_DIM, R_HEADS * R_V_DIM, R_HEADS * R_V_DIM)
IN_OFFS = tuple(int(v) for v in np.cumsum((0,) + IN_COLS))

LANES = 128
SUBLANES = 8
Q_BLOCK = 128
KEY_CHUNK = 256
SEARCH_ALWAYS = 20
SEARCH_GROUP = 4
R_CHUNK = 128
ROW_BLOCK = 512
MOE_TILE = 256
SEG_ROWS = 32
SLABS = D_MODEL // LANES
PAIRS = SLABS // 2
VMEM_LIMIT = 56 * 1024 * 1024

C_QLAT = 0
C_CKV = C_QLAT + A_HEADS * KV_LATENT
C_QI = C_CKV + KV_LATENT
C_KI = C_QI + IDX_HEADS * IDX_DIM
C_WI = C_KI + LANES
C_QR = C_WI + LANES
C_KR = C_QR + R_HEADS * R_QK_DIM
C_VR = C_KR + R_HEADS * R_QK_DIM
C_GR = C_VR + R_HEADS * R_V_DIM
C_END = C_GR + R_HEADS * R_V_DIM

INT_MIN = -2 ** 31
NEG_BIG = -1e30

_NT = (((1,), (1,)), ((), ()))
_TN = (((0,), (0,)), ((), ()))


def _cparams(sem):
    return pltpu.CompilerParams(dimension_semantics=sem, vmem_limit_bytes=VMEM_LIMIT)


def _fold_kernel(a_ref, b_ref, o_ref, *, scale):
    o_ref[0, 0] = (jnp.dot(a_ref[0, 0], b_ref[0, 0], preferred_element_type=jnp.float32,
                           precision=lax.Precision.HIGHEST) * scale).astype(o_ref.dtype)


def _fold(a, b, scale):
    L, H, M, K = a.shape
    N = b.shape[-1]
    return pl.pallas_call(
        functools.partial(_fold_kernel, scale=scale),
        out_shape=jax.ShapeDtypeStruct((L, H, M, N), jnp.bfloat16),
        grid=(L, H),
        in_specs=[pl.BlockSpec((1, 1, M, K), lambda l, h: (l, h, 0, 0)),
                  pl.BlockSpec((1, 1, K, N), lambda l, h: (l, h, 0, 0))],
        out_specs=pl.BlockSpec((1, 1, M, N), lambda l, h: (l, h, 0, 0)),
        compiler_params=_cparams(("parallel", "parallel")),
        name="weight_fold",
    )(a, b)


def _inproj_kernel(x_ref, w_ref, kvg_ref, cos_ref, sin_ref,
                   qlat_ref, ckv_ref, qi_ref, ki_ref, wi_ref, qr_ref, kr_ref, vr_ref, sg_ref):
    x = x_ref[...]

    def proj(lo, hi):
        return jnp.dot(x, w_ref[:, lo:hi], preferred_element_type=jnp.float32)

    r = proj(C_QLAT, C_CKV)
    for h in range(A_HEADS):
        qlat_ref[h] = r[:, h * KV_LATENT:(h + 1) * KV_LATENT].astype(qlat_ref.dtype)

    r = proj(C_CKV, C_QI)
    r = r * lax.rsqrt(jnp.mean(r * r, axis=-1, keepdims=True) + EPS) * kvg_ref[...]
    ckv_ref[...] = r.astype(ckv_ref.dtype)

    r = proj(C_QI, C_KI)
    for h in range(IDX_HEADS):
        qi_ref[h] = r[:, h * IDX_DIM:(h + 1) * IDX_DIM].astype(qi_ref.dtype)

    r = proj(C_KI, C_WI)
    ki_ref[...] = r[:, :IDX_DIM].astype(ki_ref.dtype)
    r = proj(C_WI, C_QR)
    wi_ref[...] = r[:, :IDX_HEADS] * (IDX_HEADS ** -0.5)

    cos = cos_ref[...]
    sin = sin_ref[...]
    half = R_HEADS * R_QK_DIM // 2

    def rot(r, out_ref, scale):
        x1, x2 = r[:, :half], r[:, half:]
        out_ref[:, :half] = ((x1 * cos - x2 * sin) * scale).astype(out_ref.dtype)
        out_ref[:, half:] = ((x1 * sin + x2 * cos) * scale).astype(out_ref.dtype)

    rot(proj(C_QR, C_KR), qr_ref, 1.0)
    rot(proj(C_KR, C_VR), kr_ref, R_QK_DIM ** -0.5)
    vr_ref[...] = proj(C_VR, C_GR).astype(vr_ref.dtype)
    g = proj(C_GR, C_END)
    sg_ref[...] = (g / (1.0 + jnp.exp(-g))).astype(sg_ref.dtype)


def _inproj(xb, w_cat, kv_g, cos_t, sin_t, t_pad):
    n = xb.shape[0]
    tm = t_pad // 4
    per_b = t_pad // tm
    bf = jnp.bfloat16
    row = lambda i: (i, 0)
    head = lambda i: (0, i, 0)
    const = lambda i: (0, 0)
    pos = lambda i: (i % per_b, 0)
    return pl.pallas_call(
        _inproj_kernel,
        out_shape=(jax.ShapeDtypeStruct((A_HEADS, n, KV_LATENT), bf),
                   jax.ShapeDtypeStruct((n, KV_LATENT), bf),
                   jax.ShapeDtypeStruct((IDX_HEADS, n, IDX_DIM), bf),
                   jax.ShapeDtypeStruct((n, IDX_DIM), bf),
                   jax.ShapeDtypeStruct((n, IDX_HEADS), jnp.float32),
                   jax.ShapeDtypeStruct((n, R_HEADS * R_QK_DIM), bf),
                   jax.ShapeDtypeStruct((n, R_HEADS * R_QK_DIM), bf),
                   jax.ShapeDtypeStruct((n, R_HEADS * R_V_DIM), bf),
                   jax.ShapeDtypeStruct((n, R_HEADS * R_V_DIM), bf)),
        grid=(n // tm,),
        in_specs=[pl.BlockSpec((tm, D_MODEL), row),
                  pl.BlockSpec((D_MODEL, C_END), const),
                  pl.BlockSpec((1, KV_LATENT), const),
                  pl.BlockSpec((tm, LANES), pos),
                  pl.BlockSpec((tm, LANES), pos)],
        out_specs=(pl.BlockSpec((A_HEADS, tm, KV_LATENT), head),
                   pl.BlockSpec((tm, KV_LATENT), row),
                   pl.BlockSpec((IDX_HEADS, tm, IDX_DIM), head),
                   pl.BlockSpec((tm, IDX_DIM), row),
                   pl.BlockSpec((tm, IDX_HEADS), row),
                   pl.BlockSpec((tm, R_HEADS * R_QK_DIM), row),
                   pl.BlockSpec((tm, R_HEADS * R_QK_DIM), row),
                   pl.BlockSpec((tm, R_HEADS * R_V_DIM), row),
                   pl.BlockSpec((tm, R_HEADS * R_V_DIM), row)),
        compiler_params=_cparams(("parallel",)),
        name="in_proj",
    )(xb, w_cat, kv_g, cos_t, sin_t)


def _fold_rows(x, op):
    out = x[:SUBLANES]
    for r in range(1, x.shape[0] // SUBLANES):
        out = op(out, x[r * SUBLANES:(r + 1) * SUBLANES])
    return out


def _attn_kernel(qlat_ref, qi_ref, wit_ref, ckv_ref, ckvt_ref, ki_ref, o_ref, key_scr, lg_scr, acc_scr,
                 *, n_sel, t_pad):
    i = pl.program_id(1)
    nk = ((i + 1) * Q_BLOCK + KEY_CHUNK - 1) // KEY_CHUNK
    tile = (KEY_CHUNK, Q_BLOCK)
    cols_all = A_HEADS * Q_BLOCK
    q_all = qlat_ref[...].reshape(cols_all, KV_LATENT)
    qi_all = qi_ref[...].reshape(IDX_HEADS * Q_BLOCK, IDX_DIM)
    wt = wit_ref[...]
    k_off = lax.broadcasted_iota(jnp.int32, tile, 0)
    q_off = lax.broadcasted_iota(jnp.int32, tile, 1)

    def key_rows(kt):
        return pl.ds(pl.multiple_of(kt * KEY_CHUNK, KEY_CHUNK), KEY_CHUNK)

    def key_pos(kt):
        return kt * KEY_CHUNK + k_off

    def over_chunks(body, carry):
        def pair(p, c):
            return body(2 * p + 1, body(2 * p, c))
        carry = lax.fori_loop(0, nk // 2, pair, carry)
        return lax.cond(nk % 2 == 1, lambda c: body(nk - 1, c), lambda c: c, carry)

    def index_tile(kt, carry):
        qk = lax.dot_general(ki_ref[key_rows(kt), :], qi_all, _NT, preferred_element_type=jnp.float32)
        score = wt[0:1] * jnp.maximum(qk[:, :Q_BLOCK], 0.0)
        for h in range(1, IDX_HEADS):
            score = score + wt[h:h + 1] * jnp.maximum(qk[:, h * Q_BLOCK:(h + 1) * Q_BLOCK], 0.0)
        bits = lax.bitcast_convert_type(score, jnp.int32)
        key = bits ^ ((bits >> 31) & jnp.int32(0x7FFFFFFF))
        key = jnp.where(score == 0.0, 0, key)
        causal = key_pos(kt) <= (i * Q_BLOCK + q_off)
        key_scr[kt] = jnp.where(causal, key, INT_MIN)
        return carry

    over_chunks(index_tile, 0)

    def count(pred):
        def body(kt, acc):
            return acc + jnp.where(pred(key_scr[kt], kt), 1.0, 0.0)
        acc = over_chunks(body, jnp.zeros(tile, jnp.float32))
        return jnp.sum(acc, axis=0, keepdims=True)

    def thr_step(it, state):
        t, n_ge = state
        cand = t ^ lax.shift_left(jnp.int32(1), 31 - it)
        n_cand = count(lambda k, kt: k >= cand)
        keep = n_cand >= n_sel
        return jnp.where(keep, cand, t), jnp.where(keep, n_cand, n_ge)

    n_real = count(lambda k, kt: k != INT_MIN)
    state = (jnp.full((1, Q_BLOCK), INT_MIN, jnp.int32),
             jnp.zeros((1, Q_BLOCK), jnp.float32) + (nk * KEY_CHUNK).astype(jnp.float32))
    state = lax.fori_loop(0, SEARCH_ALWAYS, thr_step, state)

    def unsettled(state):
        t, n_ge = state
        n_gt = count(lambda k, kt: k > t)
        done = (n_ge == n_sel) | (n_real < n_sel) | (n_gt < n_sel)
        return (jnp.max(jnp.where(done, 0.0, 1.0)) > 0.0).astype(jnp.int32)

    def more_bits(carry):
        it, state, _ = carry
        for _ in range(SEARCH_GROUP):
            state = thr_step(it, state)
            it = it + 1
        return it, state, lax.cond(it < 32, lambda: unsettled(state), lambda: jnp.int32(0))

    _, (t, _), _ = lax.while_loop(lambda c: c[2] > 0, more_bits,
                                  (jnp.int32(SEARCH_ALWAYS), state, unsettled(state)))

    need = n_sel - count(lambda k, kt: k > t)
    n_eq = count(lambda k, kt: k == t)
    idx_bits = max(1, (t_pad - 1).bit_length())
    surplus = jnp.max(jnp.where((t > INT_MIN) & (n_eq > need), 1.0, 0.0)) > 0.0

    def tie_search():
        def tie_step(it, c):
            cand = c | lax.shift_left(jnp.int32(1), idx_bits - 1 - it)
            below = count(lambda k, kt: (k == t) & (key_pos(kt) < cand))
            return jnp.where(below <= need, cand, c)
        return lax.fori_loop(0, idx_bits, tie_step, jnp.zeros((1, Q_BLOCK), jnp.int32))

    c = lax.cond(surplus, tie_search, lambda: jnp.full((1, Q_BLOCK), 1 << idx_bits, jnp.int32))

    def logit_tile(kt, m8):
        key = key_scr[kt]
        sel = ((key > t) | ((key == t) & (key_pos(kt) < c))) & (key != INT_MIN)
        lg = lax.dot_general(ckv_ref[key_rows(kt), :], q_all, _NT, preferred_element_type=jnp.float32)
        lg = jnp.where(jnp.concatenate([sel] * A_HEADS, axis=1), lg, NEG_BIG)
        lg_scr[kt] = lg
        return jnp.maximum(m8, _fold_rows(lg, jnp.maximum))

    m8 = over_chunks(logit_tile, jnp.full((SUBLANES, cols_all), NEG_BIG, jnp.float32))
    m = jnp.max(m8, axis=0, keepdims=True)
    acc_scr[...] = jnp.zeros_like(acc_scr)

    def pv_tile(kt, l8):
        p = jnp.exp(lg_scr[kt] - m)
        acc_scr[...] += jnp.dot(ckvt_ref[kt], p.astype(ckvt_ref.dtype), preferred_element_type=jnp.float32)
        return l8 + _fold_rows(p, jnp.add)

    l8 = over_chunks(pv_tile, jnp.zeros((SUBLANES, cols_all), jnp.float32))
    o_t = acc_scr[...] / jnp.sum(l8, axis=0, keepdims=True)
    for h in range(A_HEADS):
        o_ref[h] = o_t[:, h * Q_BLOCK:(h + 1) * Q_BLOCK].T.astype(o_ref.dtype)


def _attention(qlat, qi, wi, ckv, ki, batch, t_pad, n_sel):
    n = ckv.shape[0]
    nq = t_pad // Q_BLOCK
    nkc = -(-t_pad // KEY_CHUNK)
    t_keys = nkc * KEY_CHUNK

    def pad_keys(a):
        a = a.reshape(batch, t_pad, a.shape[-1])
        return jnp.pad(a, ((0, 0), (0, t_keys - t_pad), (0, 0))).reshape(batch * t_keys, a.shape[-1])

    ckv_p, ki_p = pad_keys(ckv), pad_keys(ki)
    ckv_t = jnp.swapaxes(ckv_p.reshape(batch * nkc, KEY_CHUNK, KV_LATENT), 1, 2)
    qmap = lambda b, i: (0, b * nq + i, 0)
    cols_all = A_HEADS * Q_BLOCK
    scratch = [pltpu.VMEM((nkc, KEY_CHUNK, Q_BLOCK), jnp.int32),
               pltpu.VMEM((nkc, KEY_CHUNK, cols_all), jnp.float32),
               pltpu.VMEM((KV_LATENT, cols_all), jnp.float32)]
    return pl.pallas_call(
        functools.partial(_attn_kernel, n_sel=n_sel, t_pad=t_keys),
        out_shape=jax.ShapeDtypeStruct((A_HEADS, n, KV_LATENT), jnp.bfloat16),
        grid=(batch, nq),
        in_specs=[pl.BlockSpec((A_HEADS, Q_BLOCK, KV_LATENT), qmap),
                  pl.BlockSpec((IDX_HEADS, Q_BLOCK, IDX_DIM), qmap),
                  pl.BlockSpec((IDX_HEADS, Q_BLOCK), lambda b, i: (0, b * nq + i)),
                  pl.BlockSpec((t_keys, KV_LATENT), lambda b, i: (b, 0)),
                  pl.BlockSpec((nkc, KV_LATENT, KEY_CHUNK), lambda b, i: (b, 0, 0)),
                  pl.BlockSpec((t_keys, IDX_DIM), lambda b, i: (b, 0))],
        out_specs=pl.BlockSpec((A_HEADS, Q_BLOCK, KV_LATENT), qmap),
        scratch_shapes=scratch,
        compiler_params=_cparams(("parallel", "parallel")),
        name="sparse_attention",
    )(qlat, qi, wi.T, ckv_p, ckv_t, ki_p)


def _retention_tables():
    log_g = np.log1p(-np.exp(np.linspace(math.log(1.0 / 32), math.log(1.0 / 512), R_HEADS))).astype(np.float32)
    idx = np.arange(R_CHUNK, dtype=np.float32)
    diff = idx[:, None] - idx[None, :]
    decay = np.where(diff[None] >= 0, np.exp(diff[None] * log_g[:, None, None]), 0.0).astype(np.float32)
    q_decay = np.exp((idx + 1.0)[None, :] * log_g[:, None])[:, :, None].astype(np.float32)
    k_decay = np.exp((R_CHUNK - 1.0 - idx)[None, :] * log_g[:, None])[:, :, None].astype(np.float32)
    chunk_decay = np.exp(R_CHUNK * log_g).astype(np.float32)
    return decay, q_decay, k_decay, chunk_decay


def _retention_kernel(q_ref, k_ref, v_ref, sg_ref, rw_ref, dec_ref, qd_ref, kd_ref, o_ref, state_ref,
                      *, t_pad, chunk_decay):
    state_ref[...] = jnp.zeros_like(state_ref)
    qk_w = R_HEADS * R_QK_DIM
    lane = lax.broadcasted_iota(jnp.int32, (1, qk_w), 1)
    half_w = R_QK_DIM // 2
    head_mask = [((lane % (qk_w // 2)) // half_w) == h for h in range(R_HEADS)]

    def chunk(c, carry):
        r0 = pl.multiple_of(c * R_CHUNK, R_CHUNK)
        rows = pl.ds(r0, R_CHUNK)
        qc = q_ref[rows, :]
        kc = k_ref[rows, :]
        for h in range(R_HEADS):
            cols = slice(h * R_V_DIM, (h + 1) * R_V_DIM)
            qm = jnp.where(head_mask[h], qc, jnp.zeros_like(qc))
            km = jnp.where(head_mask[h], kc, jnp.zeros_like(kc))
            vh = v_ref[rows, cols]
            inner = lax.dot_general(qm, kc, _NT, preferred_element_type=jnp.float32) * dec_ref[h]
            st = state_ref[h]
            out = (jnp.dot(inner.astype(vh.dtype), vh, preferred_element_type=jnp.float32)
                   + jnp.dot(qm, st.astype(qm.dtype), preferred_element_type=jnp.float32) * qd_ref[h])
            kdec = (km.astype(jnp.float32) * kd_ref[h]).astype(km.dtype)
            state_ref[h] = chunk_decay[h] * st + lax.dot_general(kdec, vh, _TN,
                                                                  preferred_element_type=jnp.float32)
            mu = jnp.mean(out, axis=-1, keepdims=True)
            d = out - mu
            var = jnp.mean(d * d, axis=-1, keepdims=True)
            normed = d * lax.rsqrt(var + EPS) * rw_ref[:, cols]
            o_ref[rows, cols] = (sg_ref[rows, cols].astype(jnp.float32) * normed).astype(o_ref.dtype)
        return carry

    lax.fori_loop(0, t_pad // R_CHUNK, chunk, 0)


def _retention(qr, kr, vr, sg, ret_w, batch, t_pad):
    n = qr.shape[0]
    decay, q_decay, k_decay, chunk_decay = _retention_tables()
    qk_w = R_HEADS * R_QK_DIM
    v_w = R_HEADS * R_V_DIM
    per_b = lambda b: (b, 0)
    c2 = lambda b: (0, 0)
    c3 = lambda b: (0, 0, 0)
    return pl.pallas_call(
        functools.partial(_retention_kernel, t_pad=t_pad, chunk_decay=[float(v) for v in chunk_decay]),
        out_shape=jax.ShapeDtypeStruct((n, v_w), jnp.bfloat16),
        grid=(batch,),
        in_specs=[pl.BlockSpec((t_pad, qk_w), per_b),
                  pl.BlockSpec((t_pad, qk_w), per_b),
                  pl.BlockSpec((t_pad, v_w), per_b),
                  pl.BlockSpec((t_pad, v_w), per_b),
                  pl.BlockSpec((1, v_w), c2),
                  pl.BlockSpec((R_HEADS, R_CHUNK, R_CHUNK), c3),
                  pl.BlockSpec((R_HEADS, R_CHUNK, 1), c3),
                  pl.BlockSpec((R_HEADS, R_CHUNK, 1), c3)],
        out_specs=pl.BlockSpec((t_pad, v_w), per_b),
        scratch_shapes=[pltpu.VMEM((R_HEADS, qk_w, R_V_DIM), jnp.float32)],
        compiler_params=_cparams(("parallel",)),
        name="retention",
    )(qr, kr, vr, sg, ret_w, jnp.asarray(decay), jnp.asarray(q_decay), jnp.asarray(k_decay))


def _pack_pair(lo, hi):
    as_bits = lambda a: lax.bitcast_convert_type(a.astype(jnp.bfloat16).astype(jnp.float32), jnp.int32)
    return lax.shift_right_logical(as_bits(lo), 16) | as_bits(hi)


def _unpack_pair(words):
    lo = lax.bitcast_convert_type(lax.shift_left(words, 16), jnp.float32)
    hi = lax.bitcast_convert_type(words & jnp.int32(-65536), jnp.float32)
    return lo, hi


def _layer_norm(y, g, b):
    mu = jnp.mean(y, axis=-1, keepdims=True)
    d = y - mu
    var = jnp.mean(d * d, axis=-1, keepdims=True)
    return d * lax.rsqrt(var + EPS) * g + b


def _mix_router_kernel(olat_ref, ob_ref, h_ref, wuvo_ref, wob_ref, g_ref, b_ref, wr_ref, rb_ref, tri_ref, etri_ref,
                       h1_ref, h1b_ref, x3_ref, gate_ref, loc_ref, cnt_ref, tstart_ref, tcnt_ref, run_ref,
                       *, tm):
    mix = jnp.dot(ob_ref[...], wob_ref[...], preferred_element_type=jnp.float32)
    for h in range(A_HEADS):
        mix = mix + jnp.dot(olat_ref[h], wuvo_ref[h], preferred_element_type=jnp.float32)
    h1 = _layer_norm(DN_ALPHA * h_ref[...] + mix, g_ref[...], b_ref[...])
    h1_ref[...] = h1
    h1b_ref[...] = h1.astype(h1b_ref.dtype)
    for p in range(PAIRS):
        x3_ref[pl.ds(p, tm, stride=PAIRS), :] = _pack_pair(h1[:, 2 * p * LANES:(2 * p + 1) * LANES],
                                                           h1[:, (2 * p + 1) * LANES:(2 * p + 2) * LANES])

    def split(a):
        hi = a.astype(jnp.bfloat16)
        return hi, (a - hi.astype(jnp.float32)).astype(jnp.bfloat16)

    w_hi, w_lo = split(wr_ref[...])
    h_hi, h_lo = split(h1)
    nt = lambda a, b: lax.dot_general(a, b, _NT, preferred_element_type=jnp.float32)
    logits = nt(w_hi, h_hi) + (nt(w_hi, h_lo) + nt(w_lo, h_hi))
    scores = 1.0 / (1.0 + jnp.exp(-logits))
    sel = scores + rb_ref[...]
    neg = -jnp.inf
    iota_g = lax.broadcasted_iota(jnp.int32, (GROUP_SIZE, tm), 0)
    iota_n = lax.broadcasted_iota(jnp.int32, (N_GROUPS, tm), 0)

    def first_argmax(v, iota, big):
        m = jnp.max(v, axis=0, keepdims=True)
        return m, jnp.min(jnp.where(v == m, iota, big), axis=0, keepdims=True)

    grp_score = jnp.zeros((N_GROUPS, tm), jnp.float32)
    for g in range(N_GROUPS):
        blk = sel[g * GROUP_SIZE:(g + 1) * GROUP_SIZE]
        m1, i1 = first_argmax(blk, iota_g, GROUP_SIZE)
        m2 = jnp.max(jnp.where(iota_g == i1, neg, blk), axis=0, keepdims=True)
        grp_score = jnp.where(iota_n == g, m1 + m2, grp_score)

    grp_on = jnp.zeros((N_GROUPS, tm), jnp.float32)
    work = grp_score
    for _ in range(TOP_GROUPS):
        _, gi = first_argmax(work, iota_n, N_GROUPS)
        hit = iota_n == gi
        grp_on = jnp.where(hit, 1.0, grp_on)
        work = jnp.where(hit, neg, work)

    masked = jnp.concatenate(
        [jnp.where(grp_on[g:g + 1] > 0.0, sel[g * GROUP_SIZE:(g + 1) * GROUP_SIZE], neg)
         for g in range(N_GROUPS)], axis=0)
    iota_e = lax.broadcasted_iota(jnp.int32, (N_EXPERTS, tm), 0)
    iota_k = lax.broadcasted_iota(jnp.int32, (TOP_K, tm), 0)
    top_idx = jnp.zeros((TOP_K, tm), jnp.int32)
    top_gate = jnp.zeros((TOP_K, tm), jnp.float32)
    hits = []
    for k in range(TOP_K):
        _, ei = first_argmax(masked, iota_e, N_EXPERTS)
        hit = iota_e == ei
        hits.append(hit)
        gk = jnp.sum(jnp.where(hit, scores, 0.0), axis=0, keepdims=True)
        masked = jnp.where(hit, neg, masked)
        top_idx = jnp.where(iota_k == k, ei, top_idx)
        top_gate = jnp.where(iota_k == k, gk, top_gate)
    gate_ref[...] = top_gate / jnp.sum(top_gate, axis=0, keepdims=True) * ROUTE_SCALE

    @pl.when(pl.program_id(0) == 0)
    def _():
        run_ref[...] = jnp.zeros_like(run_ref)

    onehot = jnp.zeros((N_EXPERTS, tm), jnp.float32)
    for hit in hits:
        onehot = jnp.where(hit, 1.0, onehot)
    before = jnp.dot(onehot.astype(jnp.bfloat16), tri_ref[...], preferred_element_type=jnp.float32)

    @pl.when(pl.program_id(0) == 0)
    def _():
        tstart_ref[...] = jnp.zeros_like(tstart_ref)
        tcnt_ref[...] = jnp.zeros_like(tcnt_ref)

    lane = lax.broadcasted_iota(jnp.int32, (1, tm), 1)
    tile_col = lax.broadcasted_iota(jnp.int32, tstart_ref.shape, 1)
    subs = tm // MOE_TILE
    slot = before
    seen = jnp.zeros((N_EXPERTS, 1), jnp.float32)
    for s in range(subs):
        in_sub = (lane >= s * MOE_TILE) & (lane < (s + 1) * MOE_TILE)
        sub_cnt = jnp.sum(jnp.where(in_sub, onehot, 0.0), axis=1, keepdims=True)
        seg_chunks = jnp.floor((sub_cnt + (SEG_ROWS - 1)) * (1.0 / SEG_ROWS))
        seg_off = SEG_ROWS * jnp.dot(etri_ref[...],
                                     jnp.broadcast_to(seg_chunks, (N_EXPERTS, LANES)).astype(jnp.bfloat16),
                                     preferred_element_type=jnp.float32)[:, :1]
        slot = jnp.where(in_sub, slot + (seg_off - seen), slot)
        this_tile = tile_col == pl.program_id(0) * subs + s
        tstart_ref[...] = jnp.where(this_tile, (run_ref[...] + seen).astype(jnp.int32), tstart_ref[...])
        tcnt_ref[...] = jnp.where(this_tile, sub_cnt.astype(jnp.int32), tcnt_ref[...])
        seen = seen + sub_cnt
    loc = jnp.zeros((TOP_K, tm), jnp.float32)
    for k in range(TOP_K):
        loc = jnp.where(iota_k == k, jnp.sum(jnp.where(hits[k], slot, 0.0), axis=0, keepdims=True), loc)
    loc_ref[...] = loc.astype(jnp.int32)
    run_ref[...] += seen
    cnt_ref[...] = run_ref[...].astype(jnp.int32)


def _mix_router(olat, ob, h, wuvo, wob, ln_g, ln_b, wrt, rbias, tm):
    n = h.shape[0]
    row = lambda i: (i, 0)
    c2 = lambda i: (0, 0)
    c3 = lambda i: (0, 0, 0)
    col = lambda i: (0, i)
    v_w = R_HEADS * R_V_DIM
    tiles_pad = -(-(n // MOE_TILE) // LANES) * LANES
    tri = jnp.triu(jnp.ones((tm, tm), jnp.bfloat16), k=1)
    etri = jnp.tril(jnp.ones((N_EXPERTS, N_EXPERTS), jnp.bfloat16), k=-1)
    return pl.pallas_call(
        functools.partial(_mix_router_kernel, tm=tm),
        out_shape=(jax.ShapeDtypeStruct((n, D_MODEL), jnp.float32),
                   jax.ShapeDtypeStruct((n, D_MODEL), jnp.bfloat16),
                   jax.ShapeDtypeStruct((n * PAIRS, LANES), jnp.int32),
                   jax.ShapeDtypeStruct((TOP_K, n), jnp.float32),
                   jax.ShapeDtypeStruct((TOP_K, n), jnp.int32),
                   jax.ShapeDtypeStruct((N_EXPERTS, 1), jnp.int32),
                   jax.ShapeDtypeStruct((N_EXPERTS, tiles_pad), jnp.int32),
                   jax.ShapeDtypeStruct((N_EXPERTS, tiles_pad), jnp.int32)),
        grid=(n // tm,),
        in_specs=[pl.BlockSpec((A_HEADS, tm, KV_LATENT), lambda i: (0, i, 0)),
                  pl.BlockSpec((tm, v_w), row),
                  pl.BlockSpec((tm, D_MODEL), row),
                  pl.BlockSpec((A_HEADS, KV_LATENT, D_MODEL), c3),
                  pl.BlockSpec((v_w, D_MODEL), c2),
                  pl.BlockSpec((1, D_MODEL), c2),
                  pl.BlockSpec((1, D_MODEL), c2),
                  pl.BlockSpec((N_EXPERTS, D_MODEL), c2),
                  pl.BlockSpec((N_EXPERTS, 1), c2),
                  pl.BlockSpec((tm, tm), c2),
                  pl.BlockSpec((N_EXPERTS, N_EXPERTS), c2)],
        out_specs=(pl.BlockSpec((tm, D_MODEL), row),
                   pl.BlockSpec((tm, D_MODEL), row),
                   pl.BlockSpec((tm * PAIRS, LANES), row),
                   pl.BlockSpec((TOP_K, tm), col),
                   pl.BlockSpec((TOP_K, tm), col),
                   pl.BlockSpec((N_EXPERTS, 1), c2),
                   pl.BlockSpec((N_EXPERTS, tiles_pad), c2),
                   pl.BlockSpec((N_EXPERTS, tiles_pad), c2)),
        scratch_shapes=[pltpu.VMEM((N_EXPERTS, 1), jnp.float32)],
        compiler_params=_cparams(("arbitrary",)),
        name="mix_ln_router",
    )(olat, ob, h, wuvo, wob, ln_g, ln_b, wrt, rbias, tri, etri)


def _row_copy(src, dst, sem):
    return pltpu.make_async_copy(src, dst, sem)


def _chunk_tables(tstart, tcnt, cap, max_chunks):
    chunks = (tcnt + SEG_ROWS - 1) // SEG_ROWS
    cum = jnp.cumsum(chunks, axis=1)
    first = cum - chunks
    c = jnp.arange(max_chunks, dtype=jnp.int32)
    expert = jnp.minimum(jnp.sum((cum[:, None, :] <= c[None, :, None]).astype(jnp.int32), axis=2), N_EXPERTS - 1)
    is_e = expert[:, :, None] == jnp.arange(N_EXPERTS, dtype=jnp.int32)[None, None, :]
    pick = lambda a: jnp.sum(jnp.where(is_e, a[:, None, :], 0), axis=2)
    local = c[None, :] - pick(first)
    stage_row = (pick(first) + local) * SEG_ROWS
    buffer_row = expert * cap + pick(tstart) + local * SEG_ROWS
    as_smem = lambda a: a.astype(jnp.int32)[:, None, :]
    return cum[:, -1].astype(jnp.int32), as_smem(stage_row), as_smem(buffer_row)


def _dispatch_kernel(nchunk_ref, pad_lo_ref, pad_hi_ref, loc_ref, cstage_ref, crow_ref, x3_ref, rows_ref,
                     stage_ref, zero_ref, sem, *, td):
    i = pl.program_id(0)
    last = pl.num_programs(0) - 1
    slot = i % 2

    @pl.when(i == 0)
    def _():
        stage_ref[...] = jnp.zeros_like(stage_ref)

    def place(j, carry):
        row = x3_ref[j]
        for k in range(TOP_K):
            stage_ref[slot, loc_ref[k, j]] = row
        return carry

    lax.fori_loop(0, td, place, 0)

    def drain(n_chunks, which):
        def one(c, carry):
            _row_copy(stage_ref.at[which, pl.ds(0, SEG_ROWS)], rows_ref.at[pl.ds(0, SEG_ROWS)], sem.at[which]).wait()
            return carry
        lax.fori_loop(0, n_chunks, one, 0)

    @pl.when(i > 0)
    def _():
        drain(nchunk_ref[jnp.maximum(i - 1, 0)], 1 - slot)

    def send(c, carry):
        _row_copy(stage_ref.at[slot, pl.ds(cstage_ref[0, 0, c], SEG_ROWS)],
                  rows_ref.at[pl.ds(crow_ref[0, 0, c], SEG_ROWS)], sem.at[slot]).start()
        return carry

    lax.fori_loop(0, nchunk_ref[i], send, 0)

    @pl.when(i == last)
    def _():
        drain(nchunk_ref[i], slot)
        zero_ref[...] = jnp.zeros_like(zero_ref)

        def fill_chunks(e):
            return (pad_hi_ref[e] - pad_lo_ref[e] + SEG_ROWS - 1) // SEG_ROWS

        def fill(e, carry):
            def one(c, inner):
                _row_copy(zero_ref, rows_ref.at[pl.ds(pad_lo_ref[e] + c * SEG_ROWS, SEG_ROWS)], sem.at[0]).start()
                return inner
            return lax.fori_loop(0, fill_chunks(e), one, carry)

        def fill_wait(e, carry):
            def one(c, inner):
                _row_copy(zero_ref, rows_ref.at[pl.ds(0, SEG_ROWS)], sem.at[0]).wait()
                return inner
            return lax.fori_loop(0, fill_chunks(e), one, carry)

        lax.fori_loop(0, N_EXPERTS, fill, 0)
        lax.fori_loop(0, N_EXPERTS, fill_wait, 0)


def _stage_rows(td):
    return -(-(td * TOP_K + N_EXPERTS * (SEG_ROWS - 1)) // SEG_ROWS) * SEG_ROWS


def _max_chunks(td):
    return td * TOP_K // SEG_ROWS + N_EXPERTS


def _dispatch(n_chunks, pad_lo, pad_hi, loc, chunk_stage, chunk_row, x3, n_rows, td):
    n = loc.shape[1]
    col = pl.BlockSpec((TOP_K, td), lambda i, *_: (0, i), memory_space=pltpu.SMEM)
    per_tile = pl.BlockSpec((1, 1, _max_chunks(td)), lambda i, *_: (i, 0, 0), memory_space=pltpu.SMEM)
    return pl.pallas_call(
        functools.partial(_dispatch_kernel, td=td),
        out_shape=jax.ShapeDtypeStruct((n_rows, PAIRS, LANES), jnp.int32),
        grid_spec=pltpu.PrefetchScalarGridSpec(
            num_scalar_prefetch=3,
            grid=(n // td,),
            in_specs=[col, per_tile, per_tile,
                      pl.BlockSpec((td, PAIRS, LANES), lambda i, *_: (i, 0, 0))],
            out_specs=pl.BlockSpec(memory_space=pl.ANY),
            scratch_shapes=[pltpu.VMEM((2, _stage_rows(td), PAIRS, LANES), jnp.int32),
                            pltpu.VMEM((SEG_ROWS, PAIRS, LANES), jnp.int32),
                            pltpu.SemaphoreType.DMA((2,))]),
        compiler_params=pltpu.CompilerParams(dimension_semantics=("arbitrary",), has_side_effects=True,
                                             vmem_limit_bytes=VMEM_LIMIT),
        name="moe_dispatch",
    )(n_chunks, pad_lo, pad_hi, loc, chunk_stage, chunk_row, x3)


def _expert_kernel(be_ref, br_ref, nu_ref, x_ref, wg_ref, wu_ref, wd_ref, y_ref, wgu_scr, wd_scr):
    i = pl.program_id(0)

    @pl.when(i < nu_ref[0])
    def _():
        @pl.when((i == 0) | (be_ref[i] != be_ref[jnp.maximum(i - 1, 0)]))
        def _():
            wgu_scr[:, :D_EXPERT] = wg_ref[0, 0].astype(wgu_scr.dtype)
            wgu_scr[:, D_EXPERT:] = wu_ref[0, 0].astype(wgu_scr.dtype)
            wd_scr[...] = wd_ref[0, 0].astype(wd_scr.dtype)

        pair = 2 * LANES
        gu = jnp.zeros((ROW_BLOCK, 2 * D_EXPERT), jnp.float32)
        for p in range(PAIRS):
            lo, hi = _unpack_pair(x_ref[pl.ds(p, ROW_BLOCK, stride=PAIRS), :])
            xs = jnp.concatenate([lo, hi], axis=1).astype(jnp.bfloat16)
            gu = gu + jnp.dot(xs, wgu_scr[p * pair:(p + 1) * pair, :], preferred_element_type=jnp.float32)
        g, u = gu[:, :D_EXPERT], gu[:, D_EXPERT:]
        hdn = (g / (1.0 + jnp.exp(-g)) * u).astype(jnp.bfloat16)
        y = jnp.dot(hdn, wd_scr[...], preferred_element_type=jnp.float32)
        for s in range(SLABS):
            y_ref[pl.ds(s, ROW_BLOCK, stride=SLABS), :] = y[:, s * LANES:(s + 1) * LANES]


def _experts(block_expert, block_row, n_used, x_rows2, wg, wu, wd, layer):
    n_blocks = block_expert.shape[0]
    blk = lambda i, be, br, nu: (br[jnp.minimum(i, nu[0] - 1)], 0)
    wsel = lambda i, be, br, nu: (layer, be[jnp.minimum(i, nu[0] - 1)], 0, 0)
    return pl.pallas_call(
        _expert_kernel,
        out_shape=jax.ShapeDtypeStruct((x_rows2.shape[0] // PAIRS * SLABS, LANES), jnp.float32),
        grid_spec=pltpu.PrefetchScalarGridSpec(
            num_scalar_prefetch=3,
            grid=(n_blocks,),
            in_specs=[pl.BlockSpec((ROW_BLOCK * PAIRS, LANES), blk),
                      pl.BlockSpec((1, 1, D_MODEL, D_EXPERT), wsel),
                      pl.BlockSpec((1, 1, D_MODEL, D_EXPERT), wsel),
                      pl.BlockSpec((1, 1, D_EXPERT, D_MODEL), wsel)],
            out_specs=pl.BlockSpec((ROW_BLOCK * SLABS, LANES), blk),
            scratch_shapes=[pltpu.VMEM((D_MODEL, 2 * D_EXPERT), jnp.bfloat16),
                            pltpu.VMEM((D_EXPERT, D_MODEL), jnp.bfloat16)]),
        compiler_params=_cparams(("arbitrary",)),
        name="moe_experts",
    )(block_expert, block_row, n_used, x_rows2, wg, wu, wd)


def _combine_kernel(nchunk_ref, loc_ref, gate_ref, cstage_ref, crow_ref, cstage_next_ref, crow_next_ref,
                    y3_ref, h1_ref, h1b_ref, wsg_ref, wsu_ref, wsd_ref, g_ref, b_ref, h2_ref, h2b_ref,
                    buf_ref, comb_ref, sem, *, tc):
    i = pl.program_id(0)
    last = pl.num_programs(0) - 1
    slot = i % 2

    def fetch(stage_tbl, row_tbl, n_chunks, which):
        def one(c, carry):
            _row_copy(y3_ref.at[pl.ds(row_tbl[0, 0, c], SEG_ROWS)],
                      buf_ref.at[which, pl.ds(stage_tbl[0, 0, c], SEG_ROWS)], sem.at[which]).start()
            return carry
        lax.fori_loop(0, n_chunks, one, 0)

    @pl.when(i == 0)
    def _():
        fetch(cstage_ref, crow_ref, nchunk_ref[0], 0)

    @pl.when(i < last)
    def _():
        fetch(cstage_next_ref, crow_next_ref, nchunk_ref[jnp.minimum(i + 1, last)], 1 - slot)

    def drain_chunk(c, carry):
        _row_copy(y3_ref.at[pl.ds(0, SEG_ROWS)], buf_ref.at[slot, pl.ds(0, SEG_ROWS)], sem.at[slot]).wait()
        return carry

    lax.fori_loop(0, nchunk_ref[i], drain_chunk, 0)

    def weigh(j, carry):
        acc = gate_ref[0, j] * buf_ref[slot, loc_ref[0, j]]
        for k in range(1, TOP_K):
            acc = acc + gate_ref[k, j] * buf_ref[slot, loc_ref[k, j]]
        comb_ref[pl.ds(pl.multiple_of(j * SLABS, SLABS), SLABS), :] = acc
        return carry

    lax.fori_loop(0, tc, weigh, 0)

    xb = h1b_ref[...]
    gs = jnp.dot(xb, wsg_ref[...], preferred_element_type=jnp.float32)
    us = jnp.dot(xb, wsu_ref[...], preferred_element_type=jnp.float32)
    hs = (gs / (1.0 + jnp.exp(-gs)) * us).astype(jnp.bfloat16)
    shared = jnp.dot(hs, wsd_ref[...], preferred_element_type=jnp.float32)
    routed = jnp.concatenate([comb_ref[pl.ds(s, tc, stride=SLABS), :] for s in range(SLABS)], axis=1)
    h2 = _layer_norm(DN_ALPHA * h1_ref[...] + (routed + shared), g_ref[...], b_ref[...])
    h2_ref[...] = h2
    h2b_ref[...] = h2.astype(h2b_ref.dtype)


def _combine(n_chunks, loc, gates, chunk_stage, chunk_row, y3, h1, h1b, wsg, wsu, wsd, ln_g, ln_b, tc):
    n = h1.shape[0]
    row = lambda i, *_: (i, 0)
    c2 = lambda i, *_: (0, 0)
    smem_col = pl.BlockSpec((TOP_K, tc), lambda i, *_: (0, i), memory_space=pltpu.SMEM)
    per_tile = pl.BlockSpec((1, 1, _max_chunks(tc)), lambda i, *_: (i, 0, 0), memory_space=pltpu.SMEM)
    next_tile = pl.BlockSpec((1, 1, _max_chunks(tc)), lambda i, *_: (jnp.minimum(i + 1, n // tc - 1), 0, 0),
                             memory_space=pltpu.SMEM)
    return pl.pallas_call(
        functools.partial(_combine_kernel, tc=tc),
        out_shape=(jax.ShapeDtypeStruct((n, D_MODEL), jnp.float32),
                   jax.ShapeDtypeStruct((n, D_MODEL), jnp.bfloat16)),
        grid_spec=pltpu.PrefetchScalarGridSpec(
            num_scalar_prefetch=1,
            grid=(n // tc,),
            in_specs=[smem_col, smem_col, per_tile, per_tile, next_tile, next_tile,
                      pl.BlockSpec(memory_space=pl.ANY),
                      pl.BlockSpec((tc, D_MODEL), row),
                      pl.BlockSpec((tc, D_MODEL), row),
                      pl.BlockSpec((D_MODEL, D_SHARED), c2),
                      pl.BlockSpec((D_MODEL, D_SHARED), c2),
                      pl.BlockSpec((D_SHARED, D_MODEL), c2),
                      pl.BlockSpec((1, D_MODEL), c2),
                      pl.BlockSpec((1, D_MODEL), c2)],
            out_specs=(pl.BlockSpec((tc, D_MODEL), row),
                       pl.BlockSpec((tc, D_MODEL), row)),
            scratch_shapes=[pltpu.VMEM((2, _stage_rows(tc), SLABS, LANES), jnp.float32),
                            pltpu.VMEM((tc * SLABS, LANES), jnp.float32),
                            pltpu.SemaphoreType.DMA((2,))]),
        compiler_params=_cparams(("arbitrary",)),
        name="moe_combine",
    )(n_chunks, loc, gates, chunk_stage, chunk_row, chunk_stage, chunk_row, y3, h1, h1b, wsg, wsu, wsd, ln_g, ln_b)


def _routing_plan(counts, n_blocks, cap):
    blocks_per_e = (counts + ROW_BLOCK - 1) // ROW_BLOCK
    blk_end = jnp.cumsum(blocks_per_e)
    blk_start = blk_end - blocks_per_e
    block_ids = jnp.arange(n_blocks, dtype=jnp.int32)
    block_expert = jnp.minimum(jnp.sum((blk_end[None, :] <= block_ids[:, None]).astype(jnp.int32), axis=1),
                               N_EXPERTS - 1)
    start_of = jnp.sum(jnp.where(block_expert[:, None] == jnp.arange(N_EXPERTS, dtype=jnp.int32)[None, :],
                                 blk_start[None, :], 0), axis=1)
    block_row = block_expert * (cap // ROW_BLOCK) + (block_ids - start_of)
    expert_row0 = jnp.arange(N_EXPERTS, dtype=jnp.int32) * cap
    pad_lo = expert_row0 + counts
    pad_hi = expert_row0 + blocks_per_e * ROW_BLOCK
    return block_expert, block_row.astype(jnp.int32), blk_end[-1:], pad_lo, pad_hi


def _prepare_weights(w_in, w_uk, w_uv, w_o):
    L = w_in.shape[0]
    o = IN_OFFS
    bf = jnp.bfloat16
    w_qa = w_in[:, :, o[0]:o[1]].reshape(L, D_MODEL, A_HEADS, A_HEAD_DIM).transpose(0, 2, 1, 3)
    w_qlat = _fold(w_qa, w_uk, A_HEAD_DIM ** -0.5)
    w_qlat = w_qlat.transpose(0, 2, 1, 3).reshape(L, D_MODEL, A_HEADS * KV_LATENT)
    w_oa = w_o[:, :A_HEADS * A_HEAD_DIM].reshape(L, A_HEADS, A_HEAD_DIM, D_MODEL)
    w_uvo = _fold(w_uv, w_oa, 1.0)
    w_ob = w_o[:, A_HEADS * A_HEAD_DIM:].astype(bf)

    def pad_cols(w, width):
        return jnp.pad(w, ((0, 0), (0, 0), (0, width - w.shape[-1])))

    half = R_QK_DIM // 2
    perm = np.concatenate([np.arange(half) + R_QK_DIM * h for h in range(R_HEADS)]
                          + [np.arange(half) + R_QK_DIM * h + half for h in range(R_HEADS)])
    w_cat = jnp.concatenate([
        w_qlat,
        w_in[:, :, o[1]:o[2]].astype(bf),
        w_in[:, :, o[2]:o[3]].astype(bf),
        pad_cols(w_in[:, :, o[3]:o[4]], LANES).astype(bf),
        pad_cols(w_in[:, :, o[4]:o[5]], LANES).astype(bf),
        w_in[:, :, o[5]:o[6]][:, :, perm].astype(bf),
        w_in[:, :, o[6]:o[7]][:, :, perm].astype(bf),
        w_in[:, :, o[7]:o[8]].astype(bf),
        w_in[:, :, o[8]:o[9]].astype(bf)], axis=-1)
    return w_cat, w_uvo, w_ob


def _rotary_tables(t_pad):
    half = R_QK_DIM // 2
    inv = ROPE_BASE ** (-jnp.arange(half, dtype=jnp.float32) / half)
    ang = jnp.arange(t_pad, dtype=jnp.float32)[:, None] * inv
    return jnp.tile(jnp.cos(ang), (1, R_HEADS)), jnp.tile(jnp.sin(ang), (1, R_HEADS))


def _pick_tile(n, prefer):
    for t in prefer:
        if n % t == 0:
            return t
    raise ValueError(f"no tile for {n}")


def kernel(x, meta_tokens, w_in, w_uk, w_uv, kv_norm_w, ret_norm_w, w_o, ln1_g, ln1_b, w_router, router_bias,
           w_gate, w_up, w_down, ws_gate, ws_up, ws_down, ln2_g, ln2_b):
    b, s, d = x.shape
    assert d == D_MODEL
    L = w_in.shape[0]
    t = s + N_META
    n_sel = min(TOPK_MAX, s // 4)
    t_pad = -(-t // LANES) * LANES
    n = b * t_pad
    bf = jnp.bfloat16

    meta = jnp.broadcast_to(meta_tokens.astype(x.dtype)[None], (b, N_META, d))
    h = jnp.concatenate([meta, x, jnp.zeros((b, t_pad - t, d), x.dtype)], axis=1).reshape(n, d)
    hb = h.astype(bf)

    w_cat, w_uvo, w_ob = _prepare_weights(w_in, w_uk, w_uv, w_o)
    cos_t, sin_t = _rotary_tables(t_pad)
    wsg, wsu, wsd = ws_gate.astype(bf), ws_up.astype(bf), ws_down.astype(bf)
    w_rt = jnp.swapaxes(w_router, 1, 2)

    tm = _pick_tile(n, (2 * MOE_TILE, MOE_TILE))
    n_tiles = n // MOE_TILE
    n_blocks = -(-(n * TOP_K) // ROW_BLOCK) + N_EXPERTS
    cap = (-(-n // ROW_BLOCK) + 1) * ROW_BLOCK
    n_rows = N_EXPERTS * cap

    for l in range(L):
        qlat, ckv, qi, ki, wi, qr, kr, vr, sg = _inproj(hb, w_cat[l], kv_norm_w[l][None], cos_t, sin_t, t_pad)
        olat = _attention(qlat, qi, wi, ckv, ki, b, t_pad, n_sel)
        ob = _retention(qr, kr, vr, sg, ret_norm_w[l][None], b, t_pad)
        h1, h1b, x3, gates, loc, counts, tstart, tcnt = _mix_router(
            olat, ob, h, w_uvo[l], w_ob[l], ln1_g[l][None], ln1_b[l][None], w_rt[l], router_bias[l][:, None], tm)
        n_chunks, chunk_stage, chunk_row = _chunk_tables(tstart[:, :n_tiles].T, tcnt[:, :n_tiles].T, cap,
                                                         _max_chunks(MOE_TILE))
        block_expert, block_row, n_used, pad_lo, pad_hi = _routing_plan(counts[:, 0], n_blocks, cap)
        x_rows = _dispatch(n_chunks, pad_lo, pad_hi, loc, chunk_stage, chunk_row, x3.reshape(n, PAIRS, LANES),
                           n_rows, MOE_TILE)
        y_rows = _experts(block_expert, block_row, n_used, x_rows.reshape(n_rows * PAIRS, LANES),
                          w_gate, w_up, w_down, l)
        h, hb = _combine(n_chunks, loc, gates, chunk_stage, chunk_row, y_rows.reshape(n_rows, SLABS, LANES), h1, h1b,
                         wsg[l], wsu[l], wsd[l], ln2_g[l][None], ln2_b[l][None], MOE_TILE)
    return h.reshape(b, t_pad, d)[:, N_META:t]
```

```python
import functools
import math

import numpy as np
import jax
import jax.numpy as jnp
from jax import lax
from jax.experimental import pallas as pl
from jax.experimental.pallas import tpu as pltpu

D_MODEL = 1024
N_META = 16
A_HEADS = 8
A_HEAD_DIM = 64
KV_LATENT = 128
IDX_HEADS = 8
IDX_DIM = 64
TOPK_MAX = 256
R_HEADS = 4
R_QK_DIM = 64
R_V_DIM = 128
ROPE_BASE = 10000.0
N_EXPERTS = 64
N_GROUPS = 8
GROUP_SIZE = N_EXPERTS // N_GROUPS
TOP_GROUPS = 4
TOP_K = 8
D_EXPERT = 256
D_SHARED = 256
ROUTE_SCALE = 2.5
DEPTH = 4
DN_ALPHA = (2 * DEPTH) ** 0.25
EPS = 1e-6

IN_COLS = (A_HEADS * A_HEAD_DIM, KV_LATENT, IDX_HEADS * IDX_DIM, IDX_DIM, IDX_HEADS,
           R_HEADS * R_QK_DIM, R_HEADS * R_QK_DIM, R_HEADS * R_V_DIM, R_HEADS * R_V_DIM)
IN_OFFS = tuple(int(v) for v in np.cumsum((0,) + IN_COLS))

LANES = 128
SUBLANES = 8
Q_BLOCK = 128
KEY_CHUNK = 256
SEARCH_ALWAYS = 20
SEARCH_GROUP = 4
R_CHUNK = 128
ROW_BLOCK = 512
MOE_TILE = 256
SEG_ROWS = 32
SLABS = D_MODEL // LANES
PAIRS = SLABS // 2
VMEM_LIMIT = 56 * 1024 * 1024

C_QLAT = 0
C_CKV = C_QLAT + A_HEADS * KV_LATENT
C_QI = C_CKV + KV_LATENT
C_KI = C_QI + IDX_HEADS * IDX_DIM
C_WI = C_KI + LANES
C_QR = C_WI + LANES
C_KR = C_QR + R_HEADS * R_QK_DIM
C_VR = C_KR + R_HEADS * R_QK_DIM
C_GR = C_VR + R_HEADS * R_V_DIM
C_END = C_GR + R_HEADS * R_V_DIM

INT_MIN = -2 ** 31
NEG_BIG = -1e30

_NT = (((1,), (1,)), ((), ()))
_TN = (((0,), (0,)), ((), ()))


def _cparams(sem):
    return pltpu.CompilerParams(dimension_semantics=sem, vmem_limit_bytes=VMEM_LIMIT)


def _fold_kernel(a_ref, b_ref, o_ref, *, scale):
    o_ref[0, 0] = (jnp.dot(a_ref[0, 0], b_ref[0, 0], preferred_element_type=jnp.float32,
                           precision=lax.Precision.HIGHEST) * scale).astype(o_ref.dtype)


def _fold(a, b, scale):
    L, H, M, K = a.shape
    N = b.shape[-1]
    return pl.pallas_call(
        functools.partial(_fold_kernel, scale=scale),
        out_shape=jax.ShapeDtypeStruct((L, H, M, N), jnp.bfloat16),
        grid=(L, H),
        in_specs=[pl.BlockSpec((1, 1, M, K), lambda l, h: (l, h, 0, 0)),
                  pl.BlockSpec((1, 1, K, N), lambda l, h: (l, h, 0, 0))],
        out_specs=pl.BlockSpec((1, 1, M, N), lambda l, h: (l, h, 0, 0)),
        compiler_params=_cparams(("parallel", "parallel")),
        name="weight_fold",
    )(a, b)


def _inproj_kernel(x_ref, w_ref, kvg_ref, cos_ref, sin_ref,
                   qlat_ref, ckv_ref, qi_ref, ki_ref, wi_ref, qr_ref, kr_ref, vr_ref, sg_ref):
    x = x_ref[...]

    def proj(lo, hi):
        return jnp.dot(x, w_ref[:, lo:hi], preferred_element_type=jnp.float32)

    r = proj(C_QLAT, C_CKV)
    for h in range(A_HEADS):
        qlat_ref[h] = r[:, h * KV_LATENT:(h + 1) * KV_LATENT].astype(qlat_ref.dtype)

    r = proj(C_CKV, C_QI)
    r = r * lax.rsqrt(jnp.mean(r * r, axis=-1, keepdims=True) + EPS) * kvg_ref[...]
    ckv_ref[...] = r.astype(ckv_ref.dtype)

    r = proj(C_QI, C_KI)
    for h in range(IDX_HEADS):
        qi_ref[h] = r[:, h * IDX_DIM:(h + 1) * IDX_DIM].astype(qi_ref.dtype)

    r = proj(C_KI, C_WI)
    ki_ref[...] = r[:, :IDX_DIM].astype(ki_ref.dtype)
    r = proj(C_WI, C_QR)
    wi_ref[...] = r[:, :IDX_HEADS] * (IDX_HEADS ** -0.5)

    cos = cos_ref[...]
    sin = sin_ref[...]
    half = R_HEADS * R_QK_DIM // 2

    def rot(r, out_ref, scale):
        x1, x2 = r[:, :half], r[:, half:]
        out_ref[:, :half] = ((x1 * cos - x2 * sin) * scale).astype(out_ref.dtype)
        out_ref[:, half:] = ((x1 * sin + x2 * cos) * scale).astype(out_ref.dtype)

    rot(proj(C_QR, C_KR), qr_ref, 1.0)
    rot(proj(C_KR, C_VR), kr_ref, R_QK_DIM ** -0.5)
    vr_ref[...] = proj(C_VR, C_GR).astype(vr_ref.dtype)
    g = proj(C_GR, C_END)
    sg_ref[...] = (g / (1.0 + jnp.exp(-g))).astype(sg_ref.dtype)


def _inproj(xb, w_cat, kv_g, cos_t, sin_t, t_pad):
    n = xb.shape[0]
    tm = t_pad // 4
    per_b = t_pad // tm
    bf = jnp.bfloat16
    row = lambda i: (i, 0)
    head = lambda i: (0, i, 0)
    const = lambda i: (0, 0)
    pos = lambda i: (i % per_b, 0)
    return pl.pallas_call(
        _inproj_kernel,
        out_shape=(jax.ShapeDtypeStruct((A_HEADS, n, KV_LATENT), bf),
                   jax.ShapeDtypeStruct((n, KV_LATENT), bf),
                   jax.ShapeDtypeStruct((IDX_HEADS, n, IDX_DIM), bf),
                   jax.ShapeDtypeStruct((n, IDX_DIM), bf),
                   jax.ShapeDtypeStruct((n, IDX_HEADS), jnp.float32),
                   jax.ShapeDtypeStruct((n, R_HEADS * R_QK_DIM), bf),
                   jax.ShapeDtypeStruct((n, R_HEADS * R_QK_DIM), bf),
                   jax.ShapeDtypeStruct((n, R_HEADS * R_V_DIM), bf),
                   jax.ShapeDtypeStruct((n, R_HEADS * R_V_DIM), bf)),
        grid=(n // tm,),
        in_specs=[pl.BlockSpec((tm, D_MODEL), row),
                  pl.BlockSpec((D_MODEL, C_END), const),
                  pl.BlockSpec((1, KV_LATENT), const),
                  pl.BlockSpec((tm, LANES), pos),
                  pl.BlockSpec((tm, LANES), pos)],
        out_specs=(pl.BlockSpec((A_HEADS, tm, KV_LATENT), head),
                   pl.BlockSpec((tm, KV_LATENT), row),
                   pl.BlockSpec((IDX_HEADS, tm, IDX_DIM), head),
                   pl.BlockSpec((tm, IDX_DIM), row),
                   pl.BlockSpec((tm, IDX_HEADS), row),
                   pl.BlockSpec((tm, R_HEADS * R_QK_DIM), row),
                   pl.BlockSpec((tm, R_HEADS * R_QK_DIM), row),
                   pl.BlockSpec((tm, R_HEADS * R_V_DIM), row),
                   pl.BlockSpec((tm, R_HEADS * R_V_DIM), row)),
        compiler_params=_cparams(("parallel",)),
        name="in_proj",
    )(xb, w_cat, kv_g, cos_t, sin_t)


def _fold_rows(x, op):
    out = x[:SUBLANES]
    for r in range(1, x.shape[0] // SUBLANES):
        out = op(out, x[r * SUBLANES:(r + 1) * SUBLANES])
    return out


def _attn_kernel(qlat_ref, qi_ref, wit_ref, ckv_ref, ckvt_ref, ki_ref, o_ref, key_scr, lg_scr, acc_scr,
                 *, n_sel, t_pad):
    i = pl.program_id(1)
    nk = ((i + 1) * Q_BLOCK + KEY_CHUNK - 1) // KEY_CHUNK
    tile = (KEY_CHUNK, Q_BLOCK)
    cols_all = A_HEADS * Q_BLOCK
    q_all = qlat_ref[...].reshape(cols_all, KV_LATENT)
    qi_all = qi_ref[...].reshape(IDX_HEADS * Q_BLOCK, IDX_DIM)
    wt = wit_ref[...]
    k_off = lax.broadcasted_iota(jnp.int32, tile, 0)
    q_off = lax.broadcasted_iota(jnp.int32, tile, 1)

    def key_rows(kt):
        return pl.ds(pl.multiple_of(kt * KEY_CHUNK, KEY_CHUNK), KEY_CHUNK)

    def key_pos(kt):
        return kt * KEY_CHUNK + k_off

    def over_chunks(body, carry):
        def pair(p, c):
            return body(2 * p + 1, body(2 * p, c))
        carry = lax.fori_loop(0, nk // 2, pair, carry)
        return lax.cond(nk % 2 == 1, lambda c: body(nk - 1, c), lambda c: c, carry)

    def index_tile(kt, carry):
        qk = lax.dot_general(ki_ref[key_rows(kt), :], qi_all, _NT, preferred_element_type=jnp.float32)
        score = wt[0:1] * jnp.maximum(qk[:, :Q_BLOCK], 0.0)
        for h in range(1, IDX_HEADS):
            score = score + wt[h:h + 1] * jnp.maximum(qk[:, h * Q_BLOCK:(h + 1) * Q_BLOCK], 0.0)
        bits = lax.bitcast_convert_type(score, jnp.int32)
        key = bits ^ ((bits >> 31) & jnp.int32(0x7FFFFFFF))
        key = jnp.where(score == 0.0, 0, key)
        causal = key_pos(kt) <= (i * Q_BLOCK + q_off)
        key_scr[kt] = jnp.where(causal, key, INT_MIN)
        return carry

    over_chunks(index_tile, 0)

    def count(pred):
        def body(kt, acc):
            return acc + jnp.where(pred(key_scr[kt], kt), 1.0, 0.0)
        acc = over_chunks(body, jnp.zeros(tile, jnp.float32))
        return jnp.sum(acc, axis=0, keepdims=True)

    def thr_step(it, state):
        t, n_ge = state
        cand = t ^ lax.shift_left(jnp.int32(1), 31 - it)
        n_cand = count(lambda k, kt: k >= cand)
        keep = n_cand >= n_sel
        return jnp.where(keep, cand, t), jnp.where(keep, n_cand, n_ge)

    n_real = count(lambda k, kt: k != INT_MIN)
    state = (jnp.full((1, Q_BLOCK), INT_MIN, jnp.int32),
             jnp.zeros((1, Q_BLOCK), jnp.float32) + (nk * KEY_CHUNK).astype(jnp.float32))
    state = lax.fori_loop(0, SEARCH_ALWAYS, thr_step, state)

    def unsettled(state):
        t, n_ge = state
        n_gt = count(lambda k, kt: k > t)
        done = (n_ge == n_sel) | (n_real < n_sel) | (n_gt < n_sel)
        return (jnp.max(jnp.where(done, 0.0, 1.0)) > 0.0).astype(jnp.int32)

    def more_bits(carry):
        it, state, _ = carry
        for _ in range(SEARCH_GROUP):
            state = thr_step(it, state)
            it = it + 1
        return it, state, lax.cond(it < 32, lambda: unsettled(state), lambda: jnp.int32(0))

    _, (t, _), _ = lax.while_loop(lambda c: c[2] > 0, more_bits,
                                  (jnp.int32(SEARCH_ALWAYS), state, unsettled(state)))

    need = n_sel - count(lambda k, kt: k > t)
    n_eq = count(lambda k, kt: k == t)
    idx_bits = max(1, (t_pad - 1).bit_length())
    surplus = jnp.max(jnp.where((t > INT_MIN) & (n_eq > need), 1.0, 0.0)) > 0.0

    def tie_search():
        def tie_step(it, c):
            cand = c | lax.shift_left(jnp.int32(1), idx_bits - 1 - it)
            below = count(lambda k, kt: (k == t) & (key_pos(kt) < cand))
            return jnp.where(below <= need, cand, c)
        return lax.fori_loop(0, idx_bits, tie_step, jnp.zeros((1, Q_BLOCK), jnp.int32))

    c = lax.cond(surplus, tie_search, lambda: jnp.full((1, Q_BLOCK), 1 << idx_bits, jnp.int32))

    def logit_tile(kt, m8):
        key = key_scr[kt]
        sel = ((key > t) | ((key == t) & (key_pos(kt) < c))) & (key != INT_MIN)
        lg = lax.dot_general(ckv_ref[key_rows(kt), :], q_all, _NT, preferred_element_type=jnp.float32)
        lg = jnp.where(jnp.concatenate([sel] * A_HEADS, axis=1), lg, NEG_BIG)
        lg_scr[kt] = lg
        return jnp.maximum(m8, _fold_rows(lg, jnp.maximum))

    m8 = over_chunks(logit_tile, jnp.full((SUBLANES, cols_all), NEG_BIG, jnp.float32))
    m = jnp.max(m8, axis=0, keepdims=True)
    acc_scr[...] = jnp.zeros_like(acc_scr)

    def pv_tile(kt, l8):
        p = jnp.exp(lg_scr[kt] - m)
        acc_scr[...] += jnp.dot(ckvt_ref[kt], p.astype(ckvt_ref.dtype), preferred_element_type=jnp.float32)
        return l8 + _fold_rows(p, jnp.add)

    l8 = over_chunks(pv_tile, jnp.zeros((SUBLANES, cols_all), jnp.float32))
    o_t = acc_scr[...] / jnp.sum(l8, axis=0, keepdims=True)
    for h in range(A_HEADS):
        o_ref[h] = o_t[:, h * Q_BLOCK:(h + 1) * Q_BLOCK].T.astype(o_ref.dtype)


def _attention(qlat, qi, wi, ckv, ki, batch, t_pad, n_sel):
    n = ckv.shape[0]
    nq = t_pad // Q_BLOCK
    nkc = -(-t_pad // KEY_CHUNK)
    t_keys = nkc * KEY_CHUNK

    def pad_keys(a):
        a = a.reshape(batch, t_pad, a.shape[-1])
        return jnp.pad(a, ((0, 0), (0, t_keys - t_pad), (0, 0))).reshape(batch * t_keys, a.shape[-1])

    ckv_p, ki_p = pad_keys(ckv), pad_keys(ki)
    ckv_t = jnp.swapaxes(ckv_p.reshape(batch * nkc, KEY_CHUNK, KV_LATENT), 1, 2)
    qmap = lambda b, i: (0, b * nq + i, 0)
    cols_all = A_HEADS * Q_BLOCK
    scratch = [pltpu.VMEM((nkc, KEY_CHUNK, Q_BLOCK), jnp.int32),
               pltpu.VMEM((nkc, KEY_CHUNK, cols_all), jnp.float32),
               pltpu.VMEM((KV_LATENT, cols_all), jnp.float32)]
    return pl.pallas_call(
        functools.partial(_attn_kernel, n_sel=n_sel, t_pad=t_keys),
        out_shape=jax.ShapeDtypeStruct((A_HEADS, n, KV_LATENT), jnp.bfloat16),
        grid=(batch, nq),
        in_specs=[pl.BlockSpec((A_HEADS, Q_BLOCK, KV_LATENT), qmap),
                  pl.BlockSpec((IDX_HEADS, Q_BLOCK, IDX_DIM), qmap),
                  pl.BlockSpec((IDX_HEADS, Q_BLOCK), lambda b, i: (0, b * nq + i)),
                  pl.BlockSpec((t_keys, KV_LATENT), lambda b, i: (b, 0)),
                  pl.BlockSpec((nkc, KV_LATENT, KEY_CHUNK), lambda b, i: (b, 0, 0)),
                  pl.BlockSpec((t_keys, IDX_DIM), lambda b, i: (b, 0))],
        out_specs=pl.BlockSpec((A_HEADS, Q_BLOCK, KV_LATENT), qmap),
        scratch_shapes=scratch,
        compiler_params=_cparams(("parallel", "parallel")),
        name="sparse_attention",
    )(qlat, qi, wi.T, ckv_p, ckv_t, ki_p)


def _retention_tables():
    log_g = np.log1p(-np.exp(np.linspace(math.log(1.0 / 32), math.log(1.0 / 512), R_HEADS))).astype(np.float32)
    idx = np.arange(R_CHUNK, dtype=np.float32)
    diff = idx[:, None] - idx[None, :]
    decay = np.where(diff[None] >= 0, np.exp(diff[None] * log_g[:, None, None]), 0.0).astype(np.float32)
    q_decay = np.exp((idx + 1.0)[None, :] * log_g[:, None])[:, :, None].astype(np.float32)
    k_decay = np.exp((R_CHUNK - 1.0 - idx)[None, :] * log_g[:, None])[:, :, None].astype(np.float32)
    chunk_decay = np.exp(R_CHUNK * log_g).astype(np.float32)
    return decay, q_decay, k_decay, chunk_decay


def _retention_kernel(q_ref, k_ref, v_ref, sg_ref, rw_ref, dec_ref, qd_ref, kd_ref, o_ref, state_ref,
                      *, t_pad, chunk_decay):
    state_ref[...] = jnp.zeros_like(state_ref)
    qk_w = R_HEADS * R_QK_DIM
    lane = lax.broadcasted_iota(jnp.int32, (1, qk_w), 1)
    half_w = R_QK_DIM // 2
    head_mask = [((lane % (qk_w // 2)) // half_w) == h for h in range(R_HEADS)]

    def chunk(c, carry):
        r0 = pl.multiple_of(c * R_CHUNK, R_CHUNK)
        rows = pl.ds(r0, R_CHUNK)
        qc = q_ref[rows, :]
        kc = k_ref[rows, :]
        for h in range(R_HEADS):
            cols = slice(h * R_V_DIM, (h + 1) * R_V_DIM)
            qm = jnp.where(head_mask[h], qc, jnp.zeros_like(qc))
            km = jnp.where(head_mask[h], kc, jnp.zeros_like(kc))
            vh = v_ref[rows, cols]
            inner = lax.dot_general(qm, kc, _NT, preferred_element_type=jnp.float32) * dec_ref[h]
            st = state_ref[h]
            out = (jnp.dot(inner.astype(vh.dtype), vh, preferred_element_type=jnp.float32)
                   + jnp.dot(qm, st.astype(qm.dtype), preferred_element_type=jnp.float32) * qd_ref[h])
            kdec = (km.astype(jnp.float32) * kd_ref[h]).astype(km.dtype)
            state_ref[h] = chunk_decay[h] * st + lax.dot_general(kdec, vh, _TN,
                                                                  preferred_element_type=jnp.float32)
            mu = jnp.mean(out, axis=-1, keepdims=True)
            d = out - mu
            var = jnp.mean(d * d, axis=-1, keepdims=True)
            normed = d * lax.rsqrt(var + EPS) * rw_ref[:, cols]
            o_ref[rows, cols] = (sg_ref[rows, cols].astype(jnp.float32) * normed).astype(o_ref.dtype)
        return carry

    lax.fori_loop(0, t_pad // R_CHUNK, chunk, 0)


def _retention(qr, kr, vr, sg, ret_w, batch, t_pad):
    n = qr.shape[0]
    decay, q_decay, k_decay, chunk_decay = _retention_tables()
    qk_w = R_HEADS * R_QK_DIM
    v_w = R_HEADS * R_V_DIM
    per_b = lambda b: (b, 0)
    c2 = lambda b: (0, 0)
    c3 = lambda b: (0, 0, 0)
    return pl.pallas_call(
        functools.partial(_retention_kernel, t_pad=t_pad, chunk_decay=[float(v) for v in chunk_decay]),
        out_shape=jax.ShapeDtypeStruct((n, v_w), jnp.bfloat16),
        grid=(batch,),
        in_specs=[pl.BlockSpec((t_pad, qk_w), per_b),
                  pl.BlockSpec((t_pad, qk_w), per_b),
                  pl.BlockSpec((t_pad, v_w), per_b),
                  pl.BlockSpec((t_pad, v_w), per_b),
                  pl.BlockSpec((1, v_w), c2),
                  pl.BlockSpec((R_HEADS, R_CHUNK, R_CHUNK), c3),
                  pl.BlockSpec((R_HEADS, R_CHUNK, 1), c3),
                  pl.BlockSpec((R_HEADS, R_CHUNK, 1), c3)],
        out_specs=pl.BlockSpec((t_pad, v_w), per_b),
        scratch_shapes=[pltpu.VMEM((R_HEADS, qk_w, R_V_DIM), jnp.float32)],
        compiler_params=_cparams(("parallel",)),
        name="retention",
    )(qr, kr, vr, sg, ret_w, jnp.asarray(decay), jnp.asarray(q_decay), jnp.asarray(k_decay))


def _pack_pair(lo, hi):
    as_bits = lambda a: lax.bitcast_convert_type(a.astype(jnp.bfloat16).astype(jnp.float32), jnp.int32)
    return lax.shift_right_logical(as_bits(lo), 16) | as_bits(hi)


def _unpack_pair(words):
    lo = lax.bitcast_convert_type(lax.shift_left(words, 16), jnp.float32)
    hi = lax.bitcast_convert_type(words & jnp.int32(-65536), jnp.float32)
    return lo, hi


def _layer_norm(y, g, b):
    mu = jnp.mean(y, axis=-1, keepdims=True)
    d = y - mu
    var = jnp.mean(d * d, axis=-1, keepdims=True)
    return d * lax.rsqrt(var + EPS) * g + b


def _mix_router_kernel(olat_ref, ob_ref, h_ref, wuvo_ref, wob_ref, g_ref, b_ref, wr_ref, rb_ref, tri_ref, etri_ref,
                       h1_ref, h1b_ref, x3_ref, gate_ref, loc_ref, cnt_ref, tstart_ref, tcnt_ref, run_ref,
                       *, tm):
    mix = jnp.dot(ob_ref[...], wob_ref[...], preferred_element_type=jnp.float32)
    for h in range(A_HEADS):
        mix = mix + jnp.dot(olat_ref[h], wuvo_ref[h], preferred_element_type=jnp.float32)
    h1 = _layer_norm(DN_ALPHA * h_ref[...] + mix, g_ref[...], b_ref[...])
    h1_ref[...] = h1
    h1b_ref[...] = h1.astype(h1b_ref.dtype)
    for p in range(PAIRS):
        x3_ref[pl.ds(p, tm, stride=PAIRS), :] = _pack_pair(h1[:, 2 * p * LANES:(2 * p + 1) * LANES],
                                                           h1[:, (2 * p + 1) * LANES:(2 * p + 2) * LANES])

    def split(a):
        hi = a.astype(jnp.bfloat16)
        return hi, (a - hi.astype(jnp.float32)).astype(jnp.bfloat16)

    w_hi, w_lo = split(wr_ref[...])
    h_hi, h_lo = split(h1)
    nt = lambda a, b: lax.dot_general(a, b, _NT, preferred_element_type=jnp.float32)
    logits = nt(w_hi, h_hi) + (nt(w_hi, h_lo) + nt(w_lo, h_hi))
    scores = 1.0 / (1.0 + jnp.exp(-logits))
    sel = scores + rb_ref[...]
    neg = -jnp.inf
    iota_g = lax.broadcasted_iota(jnp.int32, (GROUP_SIZE, tm), 0)
    iota_n = lax.broadcasted_iota(jnp.int32, (N_GROUPS, tm), 0)

    def first_argmax(v, iota, big):
        m = jnp.max(v, axis=0, keepdims=True)
        return m, jnp.min(jnp.where(v == m, iota, big), axis=0, keepdims=True)

    grp_score = jnp.zeros((N_GROUPS, tm), jnp.float32)
    for g in range(N_GROUPS):
        blk = sel[g * GROUP_SIZE:(g + 1) * GROUP_SIZE]
        m1, i1 = first_argmax(blk, iota_g, GROUP_SIZE)
        m2 = jnp.max(jnp.where(iota_g == i1, neg, blk), axis=0, keepdims=True)
        grp_score = jnp.where(iota_n == g, m1 + m2, grp_score)

    grp_on = jnp.zeros((N_GROUPS, tm), jnp.float32)
    work = grp_score
    for _ in range(TOP_GROUPS):
        _, gi = first_argmax(work, iota_n, N_GROUPS)
        hit = iota_n == gi
        grp_on = jnp.where(hit, 1.0, grp_on)
        work = jnp.where(hit, neg, work)

    masked = jnp.concatenate(
        [jnp.where(grp_on[g:g + 1] > 0.0, sel[g * GROUP_SIZE:(g + 1) * GROUP_SIZE], neg)
         for g in range(N_GROUPS)], axis=0)
    iota_e = lax.broadcasted_iota(jnp.int32, (N_EXPERTS, tm), 0)
    iota_k = lax.broadcasted_iota(jnp.int32, (TOP_K, tm), 0)
    top_idx = jnp.zeros((TOP_K, tm), jnp.int32)
    top_gate = jnp.zeros((TOP_K, tm), jnp.float32)
    hits = []
    for k in range(TOP_K):
        _, ei = first_argmax(masked, iota_e, N_EXPERTS)
        hit = iota_e == ei
        hits.append(hit)
        gk = jnp.sum(jnp.where(hit, scores, 0.0), axis=0, keepdims=True)
        masked = jnp.where(hit, neg, masked)
        top_idx = jnp.where(iota_k == k, ei, top_idx)
        top_gate = jnp.where(iota_k == k, gk, top_gate)
    gate_ref[...] = top_gate / jnp.sum(top_gate, axis=0, keepdims=True) * ROUTE_SCALE

    @pl.when(pl.program_id(0) == 0)
    def _():
        run_ref[...] = jnp.zeros_like(run_ref)

    onehot = jnp.zeros((N_EXPERTS, tm), jnp.float32)
    for hit in hits:
        onehot = jnp.where(hit, 1.0, onehot)
    before = jnp.dot(onehot.astype(jnp.bfloat16), tri_ref[...], preferred_element_type=jnp.float32)

    @pl.when(pl.program_id(0) == 0)
    def _():
        tstart_ref[...] = jnp.zeros_like(tstart_ref)
        tcnt_ref[...] = jnp.zeros_like(tcnt_ref)

    lane = lax.broadcasted_iota(jnp.int32, (1, tm), 1)
    tile_col = lax.broadcasted_iota(jnp.int32, tstart_ref.shape, 1)
    subs = tm // MOE_TILE
    slot = before
    seen = jnp.zeros((N_EXPERTS, 1), jnp.float32)
    for s in range(subs):
        in_sub = (lane >= s * MOE_TILE) & (lane < (s + 1) * MOE_TILE)
        sub_cnt = jnp.sum(jnp.where(in_sub, onehot, 0.0), axis=1, keepdims=True)
        seg_chunks = jnp.floor((sub_cnt + (SEG_ROWS - 1)) * (1.0 / SEG_ROWS))
        seg_off = SEG_ROWS * jnp.dot(etri_ref[...],
                                     jnp.broadcast_to(seg_chunks, (N_EXPERTS, LANES)).astype(jnp.bfloat16),
                                     preferred_element_type=jnp.float32)[:, :1]
        slot = jnp.where(in_sub, slot + (seg_off - seen), slot)
        this_tile = tile_col == pl.program_id(0) * subs + s
        tstart_ref[...] = jnp.where(this_tile, (run_ref[...] + seen).astype(jnp.int32), tstart_ref[...])
        tcnt_ref[...] = jnp.where(this_tile, sub_cnt.astype(jnp.int32), tcnt_ref[...])
        seen = seen + sub_cnt
    loc = jnp.zeros((TOP_K, tm), jnp.float32)
    for k in range(TOP_K):
        loc = jnp.where(iota_k == k, jnp.sum(jnp.where(hits[k], slot, 0.0), axis=0, keepdims=True), loc)
    loc_ref[...] = loc.astype(jnp.int32)
    run_ref[...] += seen
    cnt_ref[...] = run_ref[...].astype(jnp.int32)


def _mix_router(olat, ob, h, wuvo, wob, ln_g, ln_b, wrt, rbias, tm):
    n = h.shape[0]
    row = lambda i: (i, 0)
    c2 = lambda i: (0, 0)
    c3 = lambda i: (0, 0, 0)
    col = lambda i: (0, i)
    v_w = R_HEADS * R_V_DIM
    tiles_pad = -(-(n // MOE_TILE) // LANES) * LANES
    tri = jnp.triu(jnp.ones((tm, tm), jnp.bfloat16), k=1)
    etri = jnp.tril(jnp.ones((N_EXPERTS, N_EXPERTS), jnp.bfloat16), k=-1)
    return pl.pallas_call(
        functools.partial(_mix_router_kernel, tm=tm),
        out_shape=(jax.ShapeDtypeStruct((n, D_MODEL), jnp.float32),
                   jax.ShapeDtypeStruct((n, D_MODEL), jnp.bfloat16),
                   jax.ShapeDtypeStruct((n * PAIRS, LANES), jnp.int32),
                   jax.ShapeDtypeStruct((TOP_K, n), jnp.float32),
                   jax.ShapeDtypeStruct((TOP_K, n), jnp.int32),
                   jax.ShapeDtypeStruct((N_EXPERTS, 1), jnp.int32),
                   jax.ShapeDtypeStruct((N_EXPERTS, tiles_pad), jnp.int32),
                   jax.ShapeDtypeStruct((N_EXPERTS, tiles_pad), jnp.int32)),
        grid=(n // tm,),
        in_specs=[pl.BlockSpec((A_HEADS, tm, KV_LATENT), lambda i: (0, i, 0)),
                  pl.BlockSpec((tm, v_w), row),
                  pl.BlockSpec((tm, D_MODEL), row),
                  pl.BlockSpec((A_HEADS, KV_LATENT, D_MODEL), c3),
                  pl.BlockSpec((v_w, D_MODEL), c2),
                  pl.BlockSpec((1, D_MODEL), c2),
                  pl.BlockSpec((1, D_MODEL), c2),
                  pl.BlockSpec((N_EXPERTS, D_MODEL), c2),
                  pl.BlockSpec((N_EXPERTS, 1), c2),
                  pl.BlockSpec((tm, tm), c2),
                  pl.BlockSpec((N_EXPERTS, N_EXPERTS), c2)],
        out_specs=(pl.BlockSpec((tm, D_MODEL), row),
                   pl.BlockSpec((tm, D_MODEL), row),
                   pl.BlockSpec((tm * PAIRS, LANES), row),
                   pl.BlockSpec((TOP_K, tm), col),
                   pl.BlockSpec((TOP_K, tm), col),
                   pl.BlockSpec((N_EXPERTS, 1), c2),
                   pl.BlockSpec((N_EXPERTS, tiles_pad), c2),
                   pl.BlockSpec((N_EXPERTS, tiles_pad), c2)),
        scratch_shapes=[pltpu.VMEM((N_EXPERTS, 1), jnp.float32)],
        compiler_params=_cparams(("arbitrary",)),
        name="mix_ln_router",
    )(olat, ob, h, wuvo, wob, ln_g, ln_b, wrt, rbias, tri, etri)


def _row_copy(src, dst, sem):
    return pltpu.make_async_copy(src, dst, sem)


def _chunk_tables(tstart, tcnt, cap, max_chunks):
    chunks = (tcnt + SEG_ROWS - 1) // SEG_ROWS
    cum = jnp.cumsum(chunks, axis=1)
    first = cum - chunks
    c = jnp.arange(max_chunks, dtype=jnp.int32)
    expert = jnp.minimum(jnp.sum((cum[:, None, :] <= c[None, :, None]).astype(jnp.int32), axis=2), N_EXPERTS - 1)
    is_e = expert[:, :, None] == jnp.arange(N_EXPERTS, dtype=jnp.int32)[None, None, :]
    pick = lambda a: jnp.sum(jnp.where(is_e, a[:, None, :], 0), axis=2)
    local = c[None, :] - pick(first)
    stage_row = (pick(first) + local) * SEG_ROWS
    buffer_row = expert * cap + pick(tstart) + local * SEG_ROWS
    as_smem = lambda a: a.astype(jnp.int32)[:, None, :]
    return cum[:, -1].astype(jnp.int32), as_smem(stage_row), as_smem(buffer_row)


def _dispatch_kernel(nchunk_ref, pad_lo_ref, pad_hi_ref, loc_ref, cstage_ref, crow_ref, x3_ref, rows_ref,
                     stage_ref, zero_ref, sem, *, td):
    i = pl.program_id(0)
    last = pl.num_programs(0) - 1
    slot = i % 2

    @pl.when(i == 0)
    def _():
        stage_ref[...] = jnp.zeros_like(stage_ref)

    def place(j, carry):
        row = x3_ref[j]
        for k in range(TOP_K):
            stage_ref[slot, loc_ref[k, j]] = row
        return carry

    lax.fori_loop(0, td, place, 0)

    def drain(n_chunks, which):
        def one(c, carry):
            _row_copy(stage_ref.at[which, pl.ds(0, SEG_ROWS)], rows_ref.at[pl.ds(0, SEG_ROWS)], sem.at[which]).wait()
            return carry
        lax.fori_loop(0, n_chunks, one, 0)

    @pl.when(i > 0)
    def _():
        drain(nchunk_ref[jnp.maximum(i - 1, 0)], 1 - slot)

    def send(c, carry):
        _row_copy(stage_ref.at[slot, pl.ds(cstage_ref[0, 0, c], SEG_ROWS)],
                  rows_ref.at[pl.ds(crow_ref[0, 0, c], SEG_ROWS)], sem.at[slot]).start()
        return carry

    lax.fori_loop(0, nchunk_ref[i], send, 0)

    @pl.when(i == last)
    def _():
        drain(nchunk_ref[i], slot)
        zero_ref[...] = jnp.zeros_like(zero_ref)

        def fill_chunks(e):
            return (pad_hi_ref[e] - pad_lo_ref[e] + SEG_ROWS - 1) // SEG_ROWS

        def fill(e, carry):
            def one(c, inner):
                _row_copy(zero_ref, rows_ref.at[pl.ds(pad_lo_ref[e] + c * SEG_ROWS, SEG_ROWS)], sem.at[0]).start()
                return inner
            return lax.fori_loop(0, fill_chunks(e), one, carry)

        def fill_wait(e, carry):
            def one(c, inner):
                _row_copy(zero_ref, rows_ref.at[pl.ds(0, SEG_ROWS)], sem.at[0]).wait()
                return inner
            return lax.fori_loop(0, fill_chunks(e), one, carry)

        lax.fori_loop(0, N_EXPERTS, fill, 0)
        lax.fori_loop(0, N_EXPERTS, fill_wait, 0)


def _stage_rows(td):
    return -(-(td * TOP_K + N_EXPERTS * (SEG_ROWS - 1)) // SEG_ROWS) * SEG_ROWS


def _max_chunks(td):
    return td * TOP_K // SEG_ROWS + N_EXPERTS


def _dispatch(n_chunks, pad_lo, pad_hi, loc, chunk_stage, chunk_row, x3, n_rows, td):
    n = loc.shape[1]
    col = pl.BlockSpec((TOP_K, td), lambda i, *_: (0, i), memory_space=pltpu.SMEM)
    per_tile = pl.BlockSpec((1, 1, _max_chunks(td)), lambda i, *_: (i, 0, 0), memory_space=pltpu.SMEM)
    return pl.pallas_call(
        functools.partial(_dispatch_kernel, td=td),
        out_shape=jax.ShapeDtypeStruct((n_rows, PAIRS, LANES), jnp.int32),
        grid_spec=pltpu.PrefetchScalarGridSpec(
            num_scalar_prefetch=3,
            grid=(n // td,),
            in_specs=[col, per_tile, per_tile,
                      pl.BlockSpec((td, PAIRS, LANES), lambda i, *_: (i, 0, 0))],
            out_specs=pl.BlockSpec(memory_space=pl.ANY),
            scratch_shapes=[pltpu.VMEM((2, _stage_rows(td), PAIRS, LANES), jnp.int32),
                            pltpu.VMEM((SEG_ROWS, PAIRS, LANES), jnp.int32),
                            pltpu.SemaphoreType.DMA((2,))]),
        compiler_params=pltpu.CompilerParams(dimension_semantics=("arbitrary",), has_side_effects=True,
                                             vmem_limit_bytes=VMEM_LIMIT),
        name="moe_dispatch",
    )(n_chunks, pad_lo, pad_hi, loc, chunk_stage, chunk_row, x3)


def _expert_kernel(be_ref, br_ref, nu_ref, x_ref, wg_ref, wu_ref, wd_ref, y_ref, wgu_scr, wd_scr):
    i = pl.program_id(0)

    @pl.when(i < nu_ref[0])
    def _():
        @pl.when((i == 0) | (be_ref[i] != be_ref[jnp.maximum(i - 1, 0)]))
        def _():
            wgu_scr[:, :D_EXPERT] = wg_ref[0, 0].astype(wgu_scr.dtype)
            wgu_scr[:, D_EXPERT:] = wu_ref[0, 0].astype(wgu_scr.dtype)
            wd_scr[...] = wd_ref[0, 0].astype(wd_scr.dtype)

        pair = 2 * LANES
        gu = jnp.zeros((ROW_BLOCK, 2 * D_EXPERT), jnp.float32)
        for p in range(PAIRS):
            lo, hi = _unpack_pair(x_ref[pl.ds(p, ROW_BLOCK, stride=PAIRS), :])
            xs = jnp.concatenate([lo, hi], axis=1).astype(jnp.bfloat16)
            gu = gu + jnp.dot(xs, wgu_scr[p * pair:(p + 1) * pair, :], preferred_element_type=jnp.float32)
        g, u = gu[:, :D_EXPERT], gu[:, D_EXPERT:]
        hdn = (g / (1.0 + jnp.exp(-g)) * u).astype(jnp.bfloat16)
        y = jnp.dot(hdn, wd_scr[...], preferred_element_type=jnp.float32)
        for p in range(PAIRS):
            y_ref[pl.ds(p, ROW_BLOCK, stride=PAIRS), :] = _pack_pair(y[:, 2 * p * LANES:(2 * p + 1) * LANES],
                                                                     y[:, (2 * p + 1) * LANES:(2 * p + 2) * LANES])


def _experts(block_expert, block_row, n_used, x_rows2, wg, wu, wd, layer):
    n_blocks = block_expert.shape[0]
    blk = lambda i, be, br, nu: (br[jnp.minimum(i, nu[0] - 1)], 0)
    wsel = lambda i, be, br, nu: (layer, be[jnp.minimum(i, nu[0] - 1)], 0, 0)
    return pl.pallas_call(
        _expert_kernel,
        out_shape=jax.ShapeDtypeStruct(x_rows2.shape, jnp.int32),
        grid_spec=pltpu.PrefetchScalarGridSpec(
            num_scalar_prefetch=3,
            grid=(n_blocks,),
            in_specs=[pl.BlockSpec((ROW_BLOCK * PAIRS, LANES), blk),
                      pl.BlockSpec((1, 1, D_MODEL, D_EXPERT), wsel),
                      pl.BlockSpec((1, 1, D_MODEL, D_EXPERT), wsel),
                      pl.BlockSpec((1, 1, D_EXPERT, D_MODEL), wsel)],
            out_specs=pl.BlockSpec((ROW_BLOCK * PAIRS, LANES), blk),
            scratch_shapes=[pltpu.VMEM((D_MODEL, 2 * D_EXPERT), jnp.bfloat16),
                            pltpu.VMEM((D_EXPERT, D_MODEL), jnp.bfloat16)]),
        compiler_params=_cparams(("arbitrary",)),
        name="moe_experts",
    )(block_expert, block_row, n_used, x_rows2, wg, wu, wd)


def _combine_kernel(nchunk_ref, loc_ref, gate_ref, cstage_ref, crow_ref, cstage_next_ref, crow_next_ref,
                    y3_ref, h1_ref, h1b_ref, wsg_ref, wsu_ref, wsd_ref, g_ref, b_ref, h2_ref, h2b_ref,
                    buf_ref, comb_ref, sem, *, tc):
    i = pl.program_id(0)
    last = pl.num_programs(0) - 1
    slot = i % 2

    def fetch(stage_tbl, row_tbl, n_chunks, which):
        def one(c, carry):
            _row_copy(y3_ref.at[pl.ds(row_tbl[0, 0, c], SEG_ROWS)],
                      buf_ref.at[which, pl.ds(stage_tbl[0, 0, c], SEG_ROWS)], sem.at[which]).start()
            return carry
        lax.fori_loop(0, n_chunks, one, 0)

    @pl.when(i == 0)
    def _():
        fetch(cstage_ref, crow_ref, nchunk_ref[0], 0)

    @pl.when(i < last)
    def _():
        fetch(cstage_next_ref, crow_next_ref, nchunk_ref[jnp.minimum(i + 1, last)], 1 - slot)

    def drain_chunk(c, carry):
        _row_copy(y3_ref.at[pl.ds(0, SEG_ROWS)], buf_ref.at[slot, pl.ds(0, SEG_ROWS)], sem.at[slot]).wait()
        return carry

    lax.fori_loop(0, nchunk_ref[i], drain_chunk, 0)

    def row_values(k, j):
        return jnp.concatenate(_unpack_pair(buf_ref[slot, loc_ref[k, j]]), axis=0)

    def weigh(j, carry):
        acc = gate_ref[0, j] * row_values(0, j)
        for k in range(1, TOP_K):
            acc = acc + gate_ref[k, j] * row_values(k, j)
        comb_ref[pl.ds(pl.multiple_of(j * SLABS, SLABS), SLABS), :] = acc
        return carry

    lax.fori_loop(0, tc, weigh, 0)

    xb = h1b_ref[...]
    gs = jnp.dot(xb, wsg_ref[...], preferred_element_type=jnp.float32)
    us = jnp.dot(xb, wsu_ref[...], preferred_element_type=jnp.float32)
    hs = (gs / (1.0 + jnp.exp(-gs)) * us).astype(jnp.bfloat16)
    shared = jnp.dot(hs, wsd_ref[...], preferred_element_type=jnp.float32)
    slab_at = lambda s: (s // 2) + PAIRS * (s % 2)
    routed = jnp.concatenate([comb_ref[pl.ds(slab_at(s), tc, stride=SLABS), :] for s in range(SLABS)], axis=1)
    h2 = _layer_norm(DN_ALPHA * h1_ref[...] + (routed + shared), g_ref[...], b_ref[...])
    h2_ref[...] = h2
    h2b_ref[...] = h2.astype(h2b_ref.dtype)


def _combine(n_chunks, loc, gates, chunk_stage, chunk_row, y3, h1, h1b, wsg, wsu, wsd, ln_g, ln_b, tc):
    n = h1.shape[0]
    row = lambda i, *_: (i, 0)
    c2 = lambda i, *_: (0, 0)
    smem_col = pl.BlockSpec((TOP_K, tc), lambda i, *_: (0, i), memory_space=pltpu.SMEM)
    per_tile = pl.BlockSpec((1, 1, _max_chunks(tc)), lambda i, *_: (i, 0, 0), memory_space=pltpu.SMEM)
    next_tile = pl.BlockSpec((1, 1, _max_chunks(tc)), lambda i, *_: (jnp.minimum(i + 1, n // tc - 1), 0, 0),
                             memory_space=pltpu.SMEM)
    return pl.pallas_call(
        functools.partial(_combine_kernel, tc=tc),
        out_shape=(jax.ShapeDtypeStruct((n, D_MODEL), jnp.float32),
                   jax.ShapeDtypeStruct((n, D_MODEL), jnp.bfloat16)),
        grid_spec=pltpu.PrefetchScalarGridSpec(
            num_scalar_prefetch=1,
            grid=(n // tc,),
            in_specs=[smem_col, smem_col, per_tile, per_tile, next_tile, next_tile,
                      pl.BlockSpec(memory_space=pl.ANY),
                      pl.BlockSpec((tc, D_MODEL), row),
                      pl.BlockSpec((tc, D_MODEL), row),
                      pl.BlockSpec((D_MODEL, D_SHARED), c2),
                      pl.BlockSpec((D_MODEL, D_SHARED), c2),
                      pl.BlockSpec((D_SHARED, D_MODEL), c2),
                      pl.BlockSpec((1, D_MODEL), c2),
                      pl.BlockSpec((1, D_MODEL), c2)],
            out_specs=(pl.BlockSpec((tc, D_MODEL), row),
                       pl.BlockSpec((tc, D_MODEL), row)),
            scratch_shapes=[pltpu.VMEM((2, _stage_rows(tc), PAIRS, LANES), jnp.int32),
                            pltpu.VMEM((tc * SLABS, LANES), jnp.float32),
                            pltpu.SemaphoreType.DMA((2,))]),
        compiler_params=_cparams(("arbitrary",)),
        name="moe_combine",
    )(n_chunks, loc, gates, chunk_stage, chunk_row, chunk_stage, chunk_row, y3, h1, h1b, wsg, wsu, wsd, ln_g, ln_b)


def _routing_plan(counts, n_blocks, cap):
    blocks_per_e = (counts + ROW_BLOCK - 1) // ROW_BLOCK
    blk_end = jnp.cumsum(blocks_per_e)
    blk_start = blk_end - blocks_per_e
    block_ids = jnp.arange(n_blocks, dtype=jnp.int32)
    block_expert = jnp.minimum(jnp.sum((blk_end[None, :] <= block_ids[:, None]).astype(jnp.int32), axis=1),
                               N_EXPERTS - 1)
    start_of = jnp.sum(jnp.where(block_expert[:, None] == jnp.arange(N_EXPERTS, dtype=jnp.int32)[None, :],
                                 blk_start[None, :], 0), axis=1)
    block_row = block_expert * (cap // ROW_BLOCK) + (block_ids - start_of)
    expert_row0 = jnp.arange(N_EXPERTS, dtype=jnp.int32) * cap
    pad_lo = expert_row0 + counts
    pad_hi = expert_row0 + blocks_per_e * ROW_BLOCK
    return block_expert, block_row.astype(jnp.int32), blk_end[-1:], pad_lo, pad_hi


def _prepare_weights(w_in, w_uk, w_uv, w_o):
    L = w_in.shape[0]
    o = IN_OFFS
    bf = jnp.bfloat16
    w_qa = w_in[:, :, o[0]:o[1]].reshape(L, D_MODEL, A_HEADS, A_HEAD_DIM).transpose(0, 2, 1, 3)
    w_qlat = _fold(w_qa, w_uk, A_HEAD_DIM ** -0.5)
    w_qlat = w_qlat.transpose(0, 2, 1, 3).reshape(L, D_MODEL, A_HEADS * KV_LATENT)
    w_oa = w_o[:, :A_HEADS * A_HEAD_DIM].reshape(L, A_HEADS, A_HEAD_DIM, D_MODEL)
    w_uvo = _fold(w_uv, w_oa, 1.0)
    w_ob = w_o[:, A_HEADS * A_HEAD_DIM:].astype(bf)

    def pad_cols(w, width):
        return jnp.pad(w, ((0, 0), (0, 0), (0, width - w.shape[-1])))

    half = R_QK_DIM // 2
    perm = np.concatenate([np.arange(half) + R_QK_DIM * h for h in range(R_HEADS)]
                          + [np.arange(half) + R_QK_DIM * h + half for h in range(R_HEADS)])
    w_cat = jnp.concatenate([
        w_qlat,
        w_in[:, :, o[1]:o[2]].astype(bf),
        w_in[:, :, o[2]:o[3]].astype(bf),
        pad_cols(w_in[:, :, o[3]:o[4]], LANES).astype(bf),
        pad_cols(w_in[:, :, o[4]:o[5]], LANES).astype(bf),
        w_in[:, :, o[5]:o[6]][:, :, perm].astype(bf),
        w_in[:, :, o[6]:o[7]][:, :, perm].astype(bf),
        w_in[:, :, o[7]:o[8]].astype(bf),
        w_in[:, :, o[8]:o[9]].astype(bf)], axis=-1)
    return w_cat, w_uvo, w_ob


def _rotary_tables(t_pad):
    half = R_QK_DIM // 2
    inv = ROPE_BASE ** (-jnp.arange(half, dtype=jnp.float32) / half)
    ang = jnp.arange(t_pad, dtype=jnp.float32)[:, None] * inv
    return jnp.tile(jnp.cos(ang), (1, R_HEADS)), jnp.tile(jnp.sin(ang), (1, R_HEADS))


def _pick_tile(n, prefer):
    for t in prefer:
        if n % t == 0:
            return t
    raise ValueError(f"no tile for {n}")


def kernel(x, meta_tokens, w_in, w_uk, w_uv, kv_norm_w, ret_norm_w, w_o, ln1_g, ln1_b, w_router, router_bias,
           w_gate, w_up, w_down, ws_gate, ws_up, ws_down, ln2_g, ln2_b):
    b, s, d = x.shape
    assert d == D_MODEL
    L = w_in.shape[0]
    t = s + N_META
    n_sel = min(TOPK_MAX, s // 4)
    t_pad = -(-t // LANES) * LANES
    n = b * t_pad
    bf = jnp.bfloat16

    meta = jnp.broadcast_to(meta_tokens.astype(x.dtype)[None], (b, N_META, d))
    h = jnp.concatenate([meta, x, jnp.zeros((b, t_pad - t, d), x.dtype)], axis=1).reshape(n, d)
    hb = h.astype(bf)

    w_cat, w_uvo, w_ob = _prepare_weights(w_in, w_uk, w_uv, w_o)
    cos_t, sin_t = _rotary_tables(t_pad)
    wsg, wsu, wsd = ws_gate.astype(bf), ws_up.astype(bf), ws_down.astype(bf)
    w_rt = jnp.swapaxes(w_router, 1, 2)

    tm = _pick_tile(n, (2 * MOE_TILE, MOE_TILE))
    n_tiles = n // MOE_TILE
    n_blocks = -(-(n * TOP_K) // ROW_BLOCK) + N_EXPERTS
    cap = (-(-n // ROW_BLOCK) + 1) * ROW_BLOCK
    n_rows = N_EXPERTS * cap

    for l in range(L):
        qlat, ckv, qi, ki, wi, qr, kr, vr, sg = _inproj(hb, w_cat[l], kv_norm_w[l][None], cos_t, sin_t, t_pad)
        olat = _attention(qlat, qi, wi, ckv, ki, b, t_pad, n_sel)
        ob = _retention(qr, kr, vr, sg, ret_norm_w[l][None], b, t_pad)
        h1, h1b, x3, gates, loc, counts, tstart, tcnt = _mix_router(
            olat, ob, h, w_uvo[l], w_ob[l], ln1_g[l][None], ln1_b[l][None], w_rt[l], router_bias[l][:, None], tm)
        n_chunks, chunk_stage, chunk_row = _chunk_tables(tstart[:, :n_tiles].T, tcnt[:, :n_tiles].T, cap,
                                                         _max_chunks(MOE_TILE))
        block_expert, block_row, n_used, pad_lo, pad_hi = _routing_plan(counts[:, 0], n_blocks, cap)
        x_rows = _dispatch(n_chunks, pad_lo, pad_hi, loc, chunk_stage, chunk_row, x3.reshape(n, PAIRS, LANES),
                           n_rows, MOE_TILE)
        y_rows = _experts(block_expert, block_row, n_used, x_rows.reshape(n_rows * PAIRS, LANES),
                          w_gate, w_up, w_down, l)
        h, hb = _combine(n_chunks, loc, gates, chunk_stage, chunk_row, y_rows.reshape(n_rows, PAIRS, LANES), h1, h1b,
                         wsg[l], wsu[l], wsd[l], ln2_g[l][None], ln2_b[l][None], MOE_TILE)
    return h.reshape(b, t_pad, d)[:, N_META:t]
```

```python
import functools
import math

import numpy as np
import jax
import jax.numpy as jnp
from jax import lax
from jax.experimental import pallas as pl
from jax.experimental.pallas import tpu as pltpu

D_MODEL = 1024
N_META = 16
A_HEADS = 8
A_HEAD_DIM = 64
KV_LATENT = 128
IDX_HEADS = 8
IDX_DIM = 64
TOPK_MAX = 256
R_HEADS = 4
R_QK_DIM = 64
R_V_DIM = 128
ROPE_BASE = 10000.0
N_EXPERTS = 64
N_GROUPS = 8
GROUP_SIZE = N_EXPERTS // N_GROUPS
TOP_GROUPS = 4
TOP_K = 8
D_EXPERT = 256
D_SHARED = 256
ROUTE_SCALE = 2.5
DEPTH = 4
DN_ALPHA = (2 * DEPTH) ** 0.25
EPS = 1e-6

IN_COLS = (A_HEADS * A_HEAD_DIM, KV_LATENT, IDX_HEADS * IDX_DIM, IDX_DIM, IDX_HEADS,
           R_HEADS * R_QK_DIM, R_HEADS * R_QK_DIM, R_HEADS * R_V_DIM, R_HEADS * R_V_DIM)
IN_OFFS = tuple(int(v) for v in np.cumsum((0,) + IN_COLS))

LANES = 128
SUBLANES = 8
Q_BLOCK = 128
KEY_CHUNK = 256
SEARCH_ALWAYS = 20
SEARCH_GROUP = 4
R_CHUNK = 128
ROW_BLOCK = 512
MOE_TILE = 256
SEG_ROWS = 64
SLABS = D_MODEL // LANES
PAIRS = SLABS // 2
VMEM_LIMIT = 56 * 1024 * 1024

C_QLAT = 0
C_CKV = C_QLAT + A_HEADS * KV_LATENT
C_QI = C_CKV + KV_LATENT
C_KI = C_QI + IDX_HEADS * IDX_DIM
C_WI = C_KI + LANES
C_QR = C_WI + LANES
C_KR = C_QR + R_HEADS * R_QK_DIM
C_VR = C_KR + R_HEADS * R_QK_DIM
C_GR = C_VR + R_HEADS * R_V_DIM
C_END = C_GR + R_HEADS * R_V_DIM

INT_MIN = -2 ** 31
NEG_BIG = -1e30

_NT = (((1,), (1,)), ((), ()))
_TN = (((0,), (0,)), ((), ()))


def _cparams(sem):
    return pltpu.CompilerParams(dimension_semantics=sem, vmem_limit_bytes=VMEM_LIMIT)


def _fold_kernel(a_ref, b_ref, o_ref, *, scale):
    o_ref[0, 0] = (jnp.dot(a_ref[0, 0], b_ref[0, 0], preferred_element_type=jnp.float32,
                           precision=lax.Precision.HIGHEST) * scale).astype(o_ref.dtype)


def _fold(a, b, scale):
    L, H, M, K = a.shape
    N = b.shape[-1]
    return pl.pallas_call(
        functools.partial(_fold_kernel, scale=scale),
        out_shape=jax.ShapeDtypeStruct((L, H, M, N), jnp.bfloat16),
        grid=(L, H),
        in_specs=[pl.BlockSpec((1, 1, M, K), lambda l, h: (l, h, 0, 0)),
                  pl.BlockSpec((1, 1, K, N), lambda l, h: (l, h, 0, 0))],
        out_specs=pl.BlockSpec((1, 1, M, N), lambda l, h: (l, h, 0, 0)),
        compiler_params=_cparams(("parallel", "parallel")),
        name="weight_fold",
    )(a, b)


def _inproj_kernel(x_ref, w_ref, kvg_ref, cos_ref, sin_ref,
                   qlat_ref, ckv_ref, qi_ref, ki_ref, wi_ref, qr_ref, kr_ref, vr_ref, sg_ref):
    x = x_ref[...]

    def proj(lo, hi):
        return jnp.dot(x, w_ref[:, lo:hi], preferred_element_type=jnp.float32)

    r = proj(C_QLAT, C_CKV)
    for h in range(A_HEADS):
        qlat_ref[h] = r[:, h * KV_LATENT:(h + 1) * KV_LATENT].astype(qlat_ref.dtype)

    r = proj(C_CKV, C_QI)
    r = r * lax.rsqrt(jnp.mean(r * r, axis=-1, keepdims=True) + EPS) * kvg_ref[...]
    ckv_ref[...] = r.astype(ckv_ref.dtype)

    r = proj(C_QI, C_KI)
    for h in range(IDX_HEADS):
        qi_ref[h] = r[:, h * IDX_DIM:(h + 1) * IDX_DIM].astype(qi_ref.dtype)

    r = proj(C_KI, C_WI)
    ki_ref[...] = r[:, :IDX_DIM].astype(ki_ref.dtype)
    r = proj(C_WI, C_QR)
    wi_ref[...] = r[:, :IDX_HEADS] * (IDX_HEADS ** -0.5)

    cos = cos_ref[...]
    sin = sin_ref[...]
    half = R_HEADS * R_QK_DIM // 2

    def rot(r, out_ref, scale):
        x1, x2 = r[:, :half], r[:, half:]
        out_ref[:, :half] = ((x1 * cos - x2 * sin) * scale).astype(out_ref.dtype)
        out_ref[:, half:] = ((x1 * sin + x2 * cos) * scale).astype(out_ref.dtype)

    rot(proj(C_QR, C_KR), qr_ref, 1.0)
    rot(proj(C_KR, C_VR), kr_ref, R_QK_DIM ** -0.5)
    vr_ref[...] = proj(C_VR, C_GR).astype(vr_ref.dtype)
    g = proj(C_GR, C_END)
    sg_ref[...] = (g / (1.0 + jnp.exp(-g))).astype(sg_ref.dtype)


def _inproj(xb, w_cat, kv_g, cos_t, sin_t, t_pad):
    n = xb.shape[0]
    tm = t_pad // 4
    per_b = t_pad // tm
    bf = jnp.bfloat16
    row = lambda i: (i, 0)
    head = lambda i: (0, i, 0)
    const = lambda i: (0, 0)
    pos = lambda i: (i % per_b, 0)
    return pl.pallas_call(
        _inproj_kernel,
        out_shape=(jax.ShapeDtypeStruct((A_HEADS, n, KV_LATENT), bf),
                   jax.ShapeDtypeStruct((n, KV_LATENT), bf),
                   jax.ShapeDtypeStruct((IDX_HEADS, n, IDX_DIM), bf),
                   jax.ShapeDtypeStruct((n, IDX_DIM), bf),
                   jax.ShapeDtypeStruct((n, IDX_HEADS), jnp.float32),
                   jax.ShapeDtypeStruct((n, R_HEADS * R_QK_DIM), bf),
                   jax.ShapeDtypeStruct((n, R_HEADS * R_QK_DIM), bf),
                   jax.ShapeDtypeStruct((n, R_HEADS * R_V_DIM), bf),
                   jax.ShapeDtypeStruct((n, R_HEADS * R_V_DIM), bf)),
        grid=(n // tm,),
        in_specs=[pl.BlockSpec((tm, D_MODEL), row),
                  pl.BlockSpec((D_MODEL, C_END), const),
                  pl.BlockSpec((1, KV_LATENT), const),
                  pl.BlockSpec((tm, LANES), pos),
                  pl.BlockSpec((tm, LANES), pos)],
        out_specs=(pl.BlockSpec((A_HEADS, tm, KV_LATENT), head),
                   pl.BlockSpec((tm, KV_LATENT), row),
                   pl.BlockSpec((IDX_HEADS, tm, IDX_DIM), head),
                   pl.BlockSpec((tm, IDX_DIM), row),
                   pl.BlockSpec((tm, IDX_HEADS), row),
                   pl.BlockSpec((tm, R_HEADS * R_QK_DIM), row),
                   pl.BlockSpec((tm, R_HEADS * R_QK_DIM), row),
                   pl.BlockSpec((tm, R_HEADS * R_V_DIM), row),
                   pl.BlockSpec((tm, R_HEADS * R_V_DIM), row)),
        compiler_params=_cparams(("parallel",)),
        name="in_proj",
    )(xb, w_cat, kv_g, cos_t, sin_t)


def _fold_rows(x, op):
    out = x[:SUBLANES]
    for r in range(1, x.shape[0] // SUBLANES):
        out = op(out, x[r * SUBLANES:(r + 1) * SUBLANES])
    return out


def _attn_kernel(qlat_ref, qi_ref, wit_ref, ckv_ref, ckvt_ref, ki_ref, o_ref, key_scr, lg_scr, acc_scr,
                 *, n_sel, t_pad):
    i = pl.program_id(1)
    nk = ((i + 1) * Q_BLOCK + KEY_CHUNK - 1) // KEY_CHUNK
    tile = (KEY_CHUNK, Q_BLOCK)
    cols_all = A_HEADS * Q_BLOCK
    q_all = qlat_ref[...].reshape(cols_all, KV_LATENT)
    qi_all = qi_ref[...].reshape(IDX_HEADS * Q_BLOCK, IDX_DIM)
    wt = wit_ref[...]
    k_off = lax.broadcasted_iota(jnp.int32, tile, 0)
    q_off = lax.broadcasted_iota(jnp.int32, tile, 1)

    def key_rows(kt):
        return pl.ds(pl.multiple_of(kt * KEY_CHUNK, KEY_CHUNK), KEY_CHUNK)

    def key_pos(kt):
        return kt * KEY_CHUNK + k_off

    def over_chunks(body, carry):
        def pair(p, c):
            return body(2 * p + 1, body(2 * p, c))
        carry = lax.fori_loop(0, nk // 2, pair, carry)
        return lax.cond(nk % 2 == 1, lambda c: body(nk - 1, c), lambda c: c, carry)

    def index_tile(kt, carry):
        qk = lax.dot_general(ki_ref[key_rows(kt), :], qi_all, _NT, preferred_element_type=jnp.float32)
        score = wt[0:1] * jnp.maximum(qk[:, :Q_BLOCK], 0.0)
        for h in range(1, IDX_HEADS):
            score = score + wt[h:h + 1] * jnp.maximum(qk[:, h * Q_BLOCK:(h + 1) * Q_BLOCK], 0.0)
        bits = lax.bitcast_convert_type(score, jnp.int32)
        key = bits ^ ((bits >> 31) & jnp.int32(0x7FFFFFFF))
        key = jnp.where(score == 0.0, 0, key)
        causal = key_pos(kt) <= (i * Q_BLOCK + q_off)
        key_scr[kt] = jnp.where(causal, key, INT_MIN)
        return carry

    over_chunks(index_tile, 0)

    def count(pred):
        def body(kt, acc):
            return acc + jnp.where(pred(key_scr[kt], kt), 1.0, 0.0)
        acc = over_chunks(body, jnp.zeros(tile, jnp.float32))
        return jnp.sum(acc, axis=0, keepdims=True)

    def thr_step(it, state):
        t, n_ge = state
        cand = t ^ lax.shift_left(jnp.int32(1), 31 - it)
        n_cand = count(lambda k, kt: k >= cand)
        keep = n_cand >= n_sel
        return jnp.where(keep, cand, t), jnp.where(keep, n_cand, n_ge)

    n_real = count(lambda k, kt: k != INT_MIN)
    state = (jnp.full((1, Q_BLOCK), INT_MIN, jnp.int32),
             jnp.zeros((1, Q_BLOCK), jnp.float32) + (nk * KEY_CHUNK).astype(jnp.float32))
    state = lax.fori_loop(0, SEARCH_ALWAYS, thr_step, state)

    def unsettled(state):
        t, n_ge = state
        n_gt = count(lambda k, kt: k > t)
        done = (n_ge == n_sel) | (n_real < n_sel) | (n_gt < n_sel)
        return (jnp.max(jnp.where(done, 0.0, 1.0)) > 0.0).astype(jnp.int32)

    def more_bits(carry):
        it, state, _ = carry
        for _ in range(SEARCH_GROUP):
            state = thr_step(it, state)
            it = it + 1
        return it, state, lax.cond(it < 32, lambda: unsettled(state), lambda: jnp.int32(0))

    _, (t, _), _ = lax.while_loop(lambda c: c[2] > 0, more_bits,
                                  (jnp.int32(SEARCH_ALWAYS), state, unsettled(state)))

    need = n_sel - count(lambda k, kt: k > t)
    n_eq = count(lambda k, kt: k == t)
    idx_bits = max(1, (t_pad - 1).bit_length())
    surplus = jnp.max(jnp.where((t > INT_MIN) & (n_eq > need), 1.0, 0.0)) > 0.0

    def tie_search():
        def tie_step(it, c):
            cand = c | lax.shift_left(jnp.int32(1), idx_bits - 1 - it)
            below = count(lambda k, kt: (k == t) & (key_pos(kt) < cand))
            return jnp.where(below <= need, cand, c)
        return lax.fori_loop(0, idx_bits, tie_step, jnp.zeros((1, Q_BLOCK), jnp.int32))

    c = lax.cond(surplus, tie_search, lambda: jnp.full((1, Q_BLOCK), 1 << idx_bits, jnp.int32))

    def logit_tile(kt, m8):
        key = key_scr[kt]
        sel = ((key > t) | ((key == t) & (key_pos(kt) < c))) & (key != INT_MIN)
        lg = lax.dot_general(ckv_ref[key_rows(kt), :], q_all, _NT, preferred_element_type=jnp.float32)
        lg = jnp.where(jnp.concatenate([sel] * A_HEADS, axis=1), lg, NEG_BIG)
        lg_scr[kt] = lg
        return jnp.maximum(m8, _fold_rows(lg, jnp.maximum))

    m8 = over_chunks(logit_tile, jnp.full((SUBLANES, cols_all), NEG_BIG, jnp.float32))
    m = jnp.max(m8, axis=0, keepdims=True)
    acc_scr[...] = jnp.zeros_like(acc_scr)

    def pv_tile(kt, l8):
        p = jnp.exp(lg_scr[kt] - m)
        acc_scr[...] += jnp.dot(ckvt_ref[kt], p.astype(ckvt_ref.dtype), preferred_element_type=jnp.float32)
        return l8 + _fold_rows(p, jnp.add)

    l8 = over_chunks(pv_tile, jnp.zeros((SUBLANES, cols_all), jnp.float32))
    o_t = acc_scr[...] / jnp.sum(l8, axis=0, keepdims=True)
    for h in range(A_HEADS):
        o_ref[h] = o_t[:, h * Q_BLOCK:(h + 1) * Q_BLOCK].T.astype(o_ref.dtype)


def _attention(qlat, qi, wi, ckv, ki, batch, t_pad, n_sel):
    n = ckv.shape[0]
    nq = t_pad // Q_BLOCK
    nkc = -(-t_pad // KEY_CHUNK)
    t_keys = nkc * KEY_CHUNK

    def pad_keys(a):
        a = a.reshape(batch, t_pad, a.shape[-1])
        return jnp.pad(a, ((0, 0), (0, t_keys - t_pad), (0, 0))).reshape(batch * t_keys, a.shape[-1])

    ckv_p, ki_p = pad_keys(ckv), pad_keys(ki)
    ckv_t = jnp.swapaxes(ckv_p.reshape(batch * nkc, KEY_CHUNK, KV_LATENT), 1, 2)
    qmap = lambda b, i: (0, b * nq + i, 0)
    cols_all = A_HEADS * Q_BLOCK
    scratch = [pltpu.VMEM((nkc, KEY_CHUNK, Q_BLOCK), jnp.int32),
               pltpu.VMEM((nkc, KEY_CHUNK, cols_all), jnp.float32),
               pltpu.VMEM((KV_LATENT, cols_all), jnp.float32)]
    return pl.pallas_call(
        functools.partial(_attn_kernel, n_sel=n_sel, t_pad=t_keys),
        out_shape=jax.ShapeDtypeStruct((A_HEADS, n, KV_LATENT), jnp.bfloat16),
        grid=(batch, nq),
        in_specs=[pl.BlockSpec((A_HEADS, Q_BLOCK, KV_LATENT), qmap),
                  pl.BlockSpec((IDX_HEADS, Q_BLOCK, IDX_DIM), qmap),
                  pl.BlockSpec((IDX_HEADS, Q_BLOCK), lambda b, i: (0, b * nq + i)),
                  pl.BlockSpec((t_keys, KV_LATENT), lambda b, i: (b, 0)),
                  pl.BlockSpec((nkc, KV_LATENT, KEY_CHUNK), lambda b, i: (b, 0, 0)),
                  pl.BlockSpec((t_keys, IDX_DIM), lambda b, i: (b, 0))],
        out_specs=pl.BlockSpec((A_HEADS, Q_BLOCK, KV_LATENT), qmap),
        scratch_shapes=scratch,
        compiler_params=_cparams(("parallel", "parallel")),
        name="sparse_attention",
    )(qlat, qi, wi.T, ckv_p, ckv_t, ki_p)


def _retention_tables():
    log_g = np.log1p(-np.exp(np.linspace(math.log(1.0 / 32), math.log(1.0 / 512), R_HEADS))).astype(np.float32)
    idx = np.arange(R_CHUNK, dtype=np.float32)
    diff = idx[:, None] - idx[None, :]
    decay = np.where(diff[None] >= 0, np.exp(diff[None] * log_g[:, None, None]), 0.0).astype(np.float32)
    q_decay = np.exp((idx + 1.0)[None, :] * log_g[:, None])[:, :, None].astype(np.float32)
    k_decay = np.exp((R_CHUNK - 1.0 - idx)[None, :] * log_g[:, None])[:, :, None].astype(np.float32)
    chunk_decay = np.exp(R_CHUNK * log_g).astype(np.float32)
    return decay, q_decay, k_decay, chunk_decay


def _retention_kernel(q_ref, k_ref, v_ref, sg_ref, rw_ref, dec_ref, qd_ref, kd_ref, o_ref, state_ref,
                      *, t_pad, chunk_decay):
    state_ref[...] = jnp.zeros_like(state_ref)
    qk_w = R_HEADS * R_QK_DIM
    lane = lax.broadcasted_iota(jnp.int32, (1, qk_w), 1)
    half_w = R_QK_DIM // 2
    head_mask = [((lane % (qk_w // 2)) // half_w) == h for h in range(R_HEADS)]

    def chunk(c, carry):
        r0 = pl.multiple_of(c * R_CHUNK, R_CHUNK)
        rows = pl.ds(r0, R_CHUNK)
        qc = q_ref[rows, :]
        kc = k_ref[rows, :]
        for h in range(R_HEADS):
            cols = slice(h * R_V_DIM, (h + 1) * R_V_DIM)
            qm = jnp.where(head_mask[h], qc, jnp.zeros_like(qc))
            km = jnp.where(head_mask[h], kc, jnp.zeros_like(kc))
            vh = v_ref[rows, cols]
            inner = lax.dot_general(qm, kc, _NT, preferred_element_type=jnp.float32) * dec_ref[h]
            st = state_ref[h]
            out = (jnp.dot(inner.astype(vh.dtype), vh, preferred_element_type=jnp.float32)
                   + jnp.dot(qm, st.astype(qm.dtype), preferred_element_type=jnp.float32) * qd_ref[h])
            kdec = (km.astype(jnp.float32) * kd_ref[h]).astype(km.dtype)
            state_ref[h] = chunk_decay[h] * st + lax.dot_general(kdec, vh, _TN,
                                                                  preferred_element_type=jnp.float32)
            mu = jnp.mean(out, axis=-1, keepdims=True)
            d = out - mu
            var = jnp.mean(d * d, axis=-1, keepdims=True)
            normed = d * lax.rsqrt(var + EPS) * rw_ref[:, cols]
            o_ref[rows, cols] = (sg_ref[rows, cols].astype(jnp.float32) * normed).astype(o_ref.dtype)
        return carry

    lax.fori_loop(0, t_pad // R_CHUNK, chunk, 0)


def _retention(qr, kr, vr, sg, ret_w, batch, t_pad):
    n = qr.shape[0]
    decay, q_decay, k_decay, chunk_decay = _retention_tables()
    qk_w = R_HEADS * R_QK_DIM
    v_w = R_HEADS * R_V_DIM
    per_b = lambda b: (b, 0)
    c2 = lambda b: (0, 0)
    c3 = lambda b: (0, 0, 0)
    return pl.pallas_call(
        functools.partial(_retention_kernel, t_pad=t_pad, chunk_decay=[float(v) for v in chunk_decay]),
        out_shape=jax.ShapeDtypeStruct((n, v_w), jnp.bfloat16),
        grid=(batch,),
        in_specs=[pl.BlockSpec((t_pad, qk_w), per_b),
                  pl.BlockSpec((t_pad, qk_w), per_b),
                  pl.BlockSpec((t_pad, v_w), per_b),
                  pl.BlockSpec((t_pad, v_w), per_b),
                  pl.BlockSpec((1, v_w), c2),
                  pl.BlockSpec((R_HEADS, R_CHUNK, R_CHUNK), c3),
                  pl.BlockSpec((R_HEADS, R_CHUNK, 1), c3),
                  pl.BlockSpec((R_HEADS, R_CHUNK, 1), c3)],
        out_specs=pl.BlockSpec((t_pad, v_w), per_b),
        scratch_shapes=[pltpu.VMEM((R_HEADS, qk_w, R_V_DIM), jnp.float32)],
        compiler_params=_cparams(("parallel",)),
        name="retention",
    )(qr, kr, vr, sg, ret_w, jnp.asarray(decay), jnp.asarray(q_decay), jnp.asarray(k_decay))


def _pack_pair(lo, hi):
    as_bits = lambda a: lax.bitcast_convert_type(a.astype(jnp.bfloat16).astype(jnp.float32), jnp.int32)
    return lax.shift_right_logical(as_bits(lo), 16) | as_bits(hi)


def _unpack_pair(words):
    lo = lax.bitcast_convert_type(lax.shift_left(words, 16), jnp.float32)
    hi = lax.bitcast_convert_type(words & jnp.int32(-65536), jnp.float32)
    return lo, hi


def _layer_norm(y, g, b):
    mu = jnp.mean(y, axis=-1, keepdims=True)
    d = y - mu
    var = jnp.mean(d * d, axis=-1, keepdims=True)
    return d * lax.rsqrt(var + EPS) * g + b


def _mix_router_kernel(olat_ref, ob_ref, h_ref, wuvo_ref, wob_ref, g_ref, b_ref, wr_ref, rb_ref, tri_ref, etri_ref,
                       h1_ref, h1b_ref, x3_ref, gate_ref, loc_ref, cnt_ref, tstart_ref, tcnt_ref, run_ref,
                       *, tm):
    mix = jnp.dot(ob_ref[...], wob_ref[...], preferred_element_type=jnp.float32)
    for h in range(0, A_HEADS, 2):
        pair = jnp.concatenate([olat_ref[h], olat_ref[h + 1]], axis=1)
        w_pair = wuvo_ref[h:h + 2].reshape(2 * KV_LATENT, D_MODEL)
        mix = mix + jnp.dot(pair, w_pair, preferred_element_type=jnp.float32)
    h1 = _layer_norm(DN_ALPHA * h_ref[...] + mix, g_ref[...], b_ref[...])
    h1_ref[...] = h1
    h1b_ref[...] = h1.astype(h1b_ref.dtype)
    for p in range(PAIRS):
        x3_ref[pl.ds(p, tm, stride=PAIRS), :] = _pack_pair(h1[:, 2 * p * LANES:(2 * p + 1) * LANES],
                                                           h1[:, (2 * p + 1) * LANES:(2 * p + 2) * LANES])

    def split(a):
        hi = a.astype(jnp.bfloat16)
        return hi, (a - hi.astype(jnp.float32)).astype(jnp.bfloat16)

    w_hi, w_lo = split(wr_ref[...])
    h_hi, h_lo = split(h1)
    nt = lambda a, b: lax.dot_general(a, b, _NT, preferred_element_type=jnp.float32)
    logits = nt(w_hi, h_hi) + (nt(w_hi, h_lo) + nt(w_lo, h_hi))
    scores = 1.0 / (1.0 + jnp.exp(-logits))
    sel = scores + rb_ref[...]
    neg = -jnp.inf
    iota_g = lax.broadcasted_iota(jnp.int32, (GROUP_SIZE, tm), 0)
    iota_n = lax.broadcasted_iota(jnp.int32, (N_GROUPS, tm), 0)

    def first_argmax(v, iota, big):
        m = jnp.max(v, axis=0, keepdims=True)
        return m, jnp.min(jnp.where(v == m, iota, big), axis=0, keepdims=True)

    grp_score = jnp.zeros((N_GROUPS, tm), jnp.float32)
    for g in range(N_GROUPS):
        blk = sel[g * GROUP_SIZE:(g + 1) * GROUP_SIZE]
        m1, i1 = first_argmax(blk, iota_g, GROUP_SIZE)
        m2 = jnp.max(jnp.where(iota_g == i1, neg, blk), axis=0, keepdims=True)
        grp_score = jnp.where(iota_n == g, m1 + m2, grp_score)

    grp_on = jnp.zeros((N_GROUPS, tm), jnp.float32)
    work = grp_score
    for _ in range(TOP_GROUPS):
        _, gi = first_argmax(work, iota_n, N_GROUPS)
        hit = iota_n == gi
        grp_on = jnp.where(hit, 1.0, grp_on)
        work = jnp.where(hit, neg, work)

    masked = jnp.concatenate(
        [jnp.where(grp_on[g:g + 1] > 0.0, sel[g * GROUP_SIZE:(g + 1) * GROUP_SIZE], neg)
         for g in range(N_GROUPS)], axis=0)
    iota_e = lax.broadcasted_iota(jnp.int32, (N_EXPERTS, tm), 0)
    iota_k = lax.broadcasted_iota(jnp.int32, (TOP_K, tm), 0)
    top_idx = jnp.zeros((TOP_K, tm), jnp.int32)
    top_gate = jnp.zeros((TOP_K, tm), jnp.float32)
    hits = []
    for k in range(TOP_K):
        _, ei = first_argmax(masked, iota_e, N_EXPERTS)
        hit = iota_e == ei
        hits.append(hit)
        gk = jnp.sum(jnp.where(hit, scores, 0.0), axis=0, keepdims=True)
        masked = jnp.where(hit, neg, masked)
        top_idx = jnp.where(iota_k == k, ei, top_idx)
        top_gate = jnp.where(iota_k == k, gk, top_gate)
    gate_ref[...] = top_gate / jnp.sum(top_gate, axis=0, keepdims=True) * ROUTE_SCALE

    @pl.when(pl.program_id(0) == 0)
    def _():
        run_ref[...] = jnp.zeros_like(run_ref)

    onehot = jnp.zeros((N_EXPERTS, tm), jnp.float32)
    for hit in hits:
        onehot = jnp.where(hit, 1.0, onehot)
    before = jnp.dot(onehot.astype(jnp.bfloat16), tri_ref[...], preferred_element_type=jnp.float32)

    @pl.when(pl.program_id(0) == 0)
    def _():
        tstart_ref[...] = jnp.zeros_like(tstart_ref)
        tcnt_ref[...] = jnp.zeros_like(tcnt_ref)

    lane = lax.broadcasted_iota(jnp.int32, (1, tm), 1)
    tile_col = lax.broadcasted_iota(jnp.int32, tstart_ref.shape, 1)
    subs = tm // MOE_TILE
    slot = before
    seen = jnp.zeros((N_EXPERTS, 1), jnp.float32)
    for s in range(subs):
        in_sub = (lane >= s * MOE_TILE) & (lane < (s + 1) * MOE_TILE)
        sub_cnt = jnp.sum(jnp.where(in_sub, onehot, 0.0), axis=1, keepdims=True)
        seg_chunks = jnp.floor((sub_cnt + (SEG_ROWS - 1)) * (1.0 / SEG_ROWS))
        seg_off = SEG_ROWS * jnp.dot(etri_ref[...],
                                     jnp.broadcast_to(seg_chunks, (N_EXPERTS, LANES)).astype(jnp.bfloat16),
                                     preferred_element_type=jnp.float32)[:, :1]
        slot = jnp.where(in_sub, slot + (seg_off - seen), slot)
        this_tile = tile_col == pl.program_id(0) * subs + s
        tstart_ref[...] = jnp.where(this_tile, (run_ref[...] + seen).astype(jnp.int32), tstart_ref[...])
        tcnt_ref[...] = jnp.where(this_tile, sub_cnt.astype(jnp.int32), tcnt_ref[...])
        seen = seen + sub_cnt
    loc = jnp.zeros((TOP_K, tm), jnp.float32)
    for k in range(TOP_K):
        loc = jnp.where(iota_k == k, jnp.sum(jnp.where(hits[k], slot, 0.0), axis=0, keepdims=True), loc)
    loc_ref[...] = loc.astype(jnp.int32)
    run_ref[...] += seen
    cnt_ref[...] = run_ref[...].astype(jnp.int32)


def _mix_router(olat, ob, h, wuvo, wob, ln_g, ln_b, wrt, rbias, tm):
    n = h.shape[0]
    row = lambda i: (i, 0)
    c2 = lambda i: (0, 0)
    c3 = lambda i: (0, 0, 0)
    col = lambda i: (0, i)
    v_w = R_HEADS * R_V_DIM
    tiles_pad = -(-(n // MOE_TILE) // LANES) * LANES
    tri = jnp.triu(jnp.ones((tm, tm), jnp.bfloat16), k=1)
    etri = jnp.tril(jnp.ones((N_EXPERTS, N_EXPERTS), jnp.bfloat16), k=-1)
    return pl.pallas_call(
        functools.partial(_mix_router_kernel, tm=tm),
        out_shape=(jax.ShapeDtypeStruct((n, D_MODEL), jnp.float32),
                   jax.ShapeDtypeStruct((n, D_MODEL), jnp.bfloat16),
                   jax.ShapeDtypeStruct((n * PAIRS, LANES), jnp.int32),
                   jax.ShapeDtypeStruct((TOP_K, n), jnp.float32),
                   jax.ShapeDtypeStruct((TOP_K, n), jnp.int32),
                   jax.ShapeDtypeStruct((N_EXPERTS, 1), jnp.int32),
                   jax.ShapeDtypeStruct((N_EXPERTS, tiles_pad), jnp.int32),
                   jax.ShapeDtypeStruct((N_EXPERTS, tiles_pad), jnp.int32)),
        grid=(n // tm,),
        in_specs=[pl.BlockSpec((A_HEADS, tm, KV_LATENT), lambda i: (0, i, 0)),
                  pl.BlockSpec((tm, v_w), row),
                  pl.BlockSpec((tm, D_MODEL), row),
                  pl.BlockSpec((A_HEADS, KV_LATENT, D_MODEL), c3),
                  pl.BlockSpec((v_w, D_MODEL), c2),
                  pl.BlockSpec((1, D_MODEL), c2),
                  pl.BlockSpec((1, D_MODEL), c2),
                  pl.BlockSpec((N_EXPERTS, D_MODEL), c2),
                  pl.BlockSpec((N_EXPERTS, 1), c2),
                  pl.BlockSpec((tm, tm), c2),
                  pl.BlockSpec((N_EXPERTS, N_EXPERTS), c2)],
        out_specs=(pl.BlockSpec((tm, D_MODEL), row),
                   pl.BlockSpec((tm, D_MODEL), row),
                   pl.BlockSpec((tm * PAIRS, LANES), row),
                   pl.BlockSpec((TOP_K, tm), col),
                   pl.BlockSpec((TOP_K, tm), col),
                   pl.BlockSpec((N_EXPERTS, 1), c2),
                   pl.BlockSpec((N_EXPERTS, tiles_pad), c2),
                   pl.BlockSpec((N_EXPERTS, tiles_pad), c2)),
        scratch_shapes=[pltpu.VMEM((N_EXPERTS, 1), jnp.float32)],
        compiler_params=_cparams(("arbitrary",)),
        name="mix_ln_router",
    )(olat, ob, h, wuvo, wob, ln_g, ln_b, wrt, rbias, tri, etri)


def _row_copy(src, dst, sem):
    return pltpu.make_async_copy(src, dst, sem)


def _chunk_tables(tstart, tcnt, cap, max_chunks):
    chunks = (tcnt + SEG_ROWS - 1) // SEG_ROWS
    cum = jnp.cumsum(chunks, axis=1)
    first = cum - chunks
    c = jnp.arange(max_chunks, dtype=jnp.int32)
    expert = jnp.minimum(jnp.sum((cum[:, None, :] <= c[None, :, None]).astype(jnp.int32), axis=2), N_EXPERTS - 1)
    is_e = expert[:, :, None] == jnp.arange(N_EXPERTS, dtype=jnp.int32)[None, None, :]
    pick = lambda a: jnp.sum(jnp.where(is_e, a[:, None, :], 0), axis=2)
    local = c[None, :] - pick(first)
    stage_row = (pick(first) + local) * SEG_ROWS
    buffer_row = expert * cap + pick(tstart) + local * SEG_ROWS
    as_smem = lambda a: a.astype(jnp.int32)[:, None, :]
    return cum[:, -1].astype(jnp.int32), as_smem(stage_row), as_smem(buffer_row)


def _dispatch_kernel(nchunk_ref, pad_lo_ref, pad_hi_ref, loc_ref, cstage_ref, crow_ref, x3_ref, rows_ref,
                     stage_ref, zero_ref, sem, *, td):
    i = pl.program_id(0)
    last = pl.num_programs(0) - 1
    slot = i % 2

    @pl.when(i == 0)
    def _():
        stage_ref[...] = jnp.zeros_like(stage_ref)

    def place(j, carry):
        row = x3_ref[j]
        for k in range(TOP_K):
            stage_ref[slot, loc_ref[k, j]] = row
        return carry

    lax.fori_loop(0, td, place, 0)

    def drain(n_chunks, which):
        def one(c, carry):
            _row_copy(stage_ref.at[which, pl.ds(0, SEG_ROWS)], rows_ref.at[pl.ds(0, SEG_ROWS)], sem.at[which]).wait()
            return carry
        lax.fori_loop(0, n_chunks, one, 0)

    @pl.when(i > 0)
    def _():
        drain(nchunk_ref[jnp.maximum(i - 1, 0)], 1 - slot)

    def send(c, carry):
        _row_copy(stage_ref.at[slot, pl.ds(cstage_ref[0, 0, c], SEG_ROWS)],
                  rows_ref.at[pl.ds(crow_ref[0, 0, c], SEG_ROWS)], sem.at[slot]).start()
        return carry

    lax.fori_loop(0, nchunk_ref[i], send, 0)

    @pl.when(i == last)
    def _():
        drain(nchunk_ref[i], slot)
        zero_ref[...] = jnp.zeros_like(zero_ref)

        def fill_chunks(e):
            return (pad_hi_ref[e] - pad_lo_ref[e] + SEG_ROWS - 1) // SEG_ROWS

        def fill(e, carry):
            def one(c, inner):
                _row_copy(zero_ref, rows_ref.at[pl.ds(pad_lo_ref[e] + c * SEG_ROWS, SEG_ROWS)], sem.at[0]).start()
                return inner
            return lax.fori_loop(0, fill_chunks(e), one, carry)

        def fill_wait(e, carry):
            def one(c, inner):
                _row_copy(zero_ref, rows_ref.at[pl.ds(0, SEG_ROWS)], sem.at[0]).wait()
                return inner
            return lax.fori_loop(0, fill_chunks(e), one, carry)

        lax.fori_loop(0, N_EXPERTS, fill, 0)
        lax.fori_loop(0, N_EXPERTS, fill_wait, 0)


def _stage_rows(td):
    return -(-(td * TOP_K + N_EXPERTS * (SEG_ROWS - 1)) // SEG_ROWS) * SEG_ROWS


def _max_chunks(td):
    return td * TOP_K // SEG_ROWS + N_EXPERTS


def _dispatch(n_chunks, pad_lo, pad_hi, loc, chunk_stage, chunk_row, x3, n_rows, td):
    n = loc.shape[1]
    col = pl.BlockSpec((TOP_K, td), lambda i, *_: (0, i), memory_space=pltpu.SMEM)
    per_tile = pl.BlockSpec((1, 1, _max_chunks(td)), lambda i, *_: (i, 0, 0), memory_space=pltpu.SMEM)
    return pl.pallas_call(
        functools.partial(_dispatch_kernel, td=td),
        out_shape=jax.ShapeDtypeStruct((n_rows, PAIRS, LANES), jnp.int32),
        grid_spec=pltpu.PrefetchScalarGridSpec(
            num_scalar_prefetch=3,
            grid=(n // td,),
            in_specs=[col, per_tile, per_tile,
                      pl.BlockSpec((td, PAIRS, LANES), lambda i, *_: (i, 0, 0))],
            out_specs=pl.BlockSpec(memory_space=pl.ANY),
            scratch_shapes=[pltpu.VMEM((2, _stage_rows(td), PAIRS, LANES), jnp.int32),
                            pltpu.VMEM((SEG_ROWS, PAIRS, LANES), jnp.int32),
                            pltpu.SemaphoreType.DMA((2,))]),
        compiler_params=pltpu.CompilerParams(dimension_semantics=("arbitrary",), has_side_effects=True,
                                             vmem_limit_bytes=VMEM_LIMIT),
        name="moe_dispatch",
    )(n_chunks, pad_lo, pad_hi, loc, chunk_stage, chunk_row, x3)


def _expert_kernel(be_ref, br_ref, nu_ref, x_ref, wg_ref, wu_ref, wd_ref, y_ref, wgu_scr, wd_scr):
    i = pl.program_id(0)

    @pl.when(i < nu_ref[0])
    def _():
        @pl.when((i == 0) | (be_ref[i] != be_ref[jnp.maximum(i - 1, 0)]))
        def _():
            wgu_scr[:, :D_EXPERT] = wg_ref[0, 0].astype(wgu_scr.dtype)
            wgu_scr[:, D_EXPERT:] = wu_ref[0, 0].astype(wgu_scr.dtype)
            wd_scr[...] = wd_ref[0, 0].astype(wd_scr.dtype)

        pair = 2 * LANES
        gu = jnp.zeros((ROW_BLOCK, 2 * D_EXPERT), jnp.float32)
        for p in range(PAIRS):
            lo, hi = _unpack_pair(x_ref[pl.ds(p, ROW_BLOCK, stride=PAIRS), :])
            xs = jnp.concatenate([lo, hi], axis=1).astype(jnp.bfloat16)
            gu = gu + jnp.dot(xs, wgu_scr[p * pair:(p + 1) * pair, :], preferred_element_type=jnp.float32)
        g, u = gu[:, :D_EXPERT], gu[:, D_EXPERT:]
        hdn = (g / (1.0 + jnp.exp(-g)) * u).astype(jnp.bfloat16)
        y = jnp.dot(hdn, wd_scr[...], preferred_element_type=jnp.float32)
        for p in range(PAIRS):
            y_ref[pl.ds(p, ROW_BLOCK, stride=PAIRS), :] = _pack_pair(y[:, 2 * p * LANES:(2 * p + 1) * LANES],
                                                                     y[:, (2 * p + 1) * LANES:(2 * p + 2) * LANES])


def _experts(block_expert, block_row, n_used, x_rows2, wg, wu, wd, layer):
    n_blocks = block_expert.shape[0]
    blk = lambda i, be, br, nu: (br[jnp.minimum(i, nu[0] - 1)], 0)
    wsel = lambda i, be, br, nu: (layer, be[jnp.minimum(i, nu[0] - 1)], 0, 0)
    return pl.pallas_call(
        _expert_kernel,
        out_shape=jax.ShapeDtypeStruct(x_rows2.shape, jnp.int32),
        grid_spec=pltpu.PrefetchScalarGridSpec(
            num_scalar_prefetch=3,
            grid=(n_blocks,),
            in_specs=[pl.BlockSpec((ROW_BLOCK * PAIRS, LANES), blk),
                      pl.BlockSpec((1, 1, D_MODEL, D_EXPERT), wsel),
                      pl.BlockSpec((1, 1, D_MODEL, D_EXPERT), wsel),
                      pl.BlockSpec((1, 1, D_EXPERT, D_MODEL), wsel)],
            out_specs=pl.BlockSpec((ROW_BLOCK * PAIRS, LANES), blk),
            scratch_shapes=[pltpu.VMEM((D_MODEL, 2 * D_EXPERT), jnp.bfloat16),
                            pltpu.VMEM((D_EXPERT, D_MODEL), jnp.bfloat16)]),
        compiler_params=_cparams(("arbitrary",)),
        name="moe_experts",
    )(block_expert, block_row, n_used, x_rows2, wg, wu, wd)


def _combine_kernel(nchunk_ref, loc_ref, gate_ref, cstage_ref, crow_ref, cstage_next_ref, crow_next_ref,
                    y3_ref, h1_ref, h1b_ref, wsg_ref, wsu_ref, wsd_ref, g_ref, b_ref, h2_ref, h2b_ref,
                    buf_ref, comb_ref, sem, *, tc):
    i = pl.program_id(0)
    last = pl.num_programs(0) - 1
    slot = i % 2

    def fetch(stage_tbl, row_tbl, n_chunks, which):
        def one(c, carry):
            _row_copy(y3_ref.at[pl.ds(row_tbl[0, 0, c], SEG_ROWS)],
                      buf_ref.at[which, pl.ds(stage_tbl[0, 0, c], SEG_ROWS)], sem.at[which]).start()
            return carry
        lax.fori_loop(0, n_chunks, one, 0)

    @pl.when(i == 0)
    def _():
        fetch(cstage_ref, crow_ref, nchunk_ref[0], 0)

    @pl.when(i < last)
    def _():
        fetch(cstage_next_ref, crow_next_ref, nchunk_ref[jnp.minimum(i + 1, last)], 1 - slot)

    def drain_chunk(c, carry):
        _row_copy(y3_ref.at[pl.ds(0, SEG_ROWS)], buf_ref.at[slot, pl.ds(0, SEG_ROWS)], sem.at[slot]).wait()
        return carry

    lax.fori_loop(0, nchunk_ref[i], drain_chunk, 0)

    def row_values(k, j):
        return jnp.concatenate(_unpack_pair(buf_ref[slot, loc_ref[k, j]]), axis=0)

    def weigh(j, carry):
        acc = gate_ref[0, j] * row_values(0, j)
        for k in range(1, TOP_K):
            acc = acc + gate_ref[k, j] * row_values(k, j)
        comb_ref[pl.ds(pl.multiple_of(j * SLABS, SLABS), SLABS), :] = acc
        return carry

    lax.fori_loop(0, tc, weigh, 0)

    xb = h1b_ref[...]
    gs = jnp.dot(xb, wsg_ref[...], preferred_element_type=jnp.float32)
    us = jnp.dot(xb, wsu_ref[...], preferred_element_type=jnp.float32)
    hs = (gs / (1.0 + jnp.exp(-gs)) * us).astype(jnp.bfloat16)
    shared = jnp.dot(hs, wsd_ref[...], preferred_element_type=jnp.float32)
    slab_at = lambda s: (s // 2) + PAIRS * (s % 2)
    routed = jnp.concatenate([comb_ref[pl.ds(slab_at(s), tc, stride=SLABS), :] for s in range(SLABS)], axis=1)
    h2 = _layer_norm(DN_ALPHA * h1_ref[...] + (routed + shared), g_ref[...], b_ref[...])
    h2_ref[...] = h2
    h2b_ref[...] = h2.astype(h2b_ref.dtype)


def _combine(n_chunks, loc, gates, chunk_stage, chunk_row, y3, h1, h1b, wsg, wsu, wsd, ln_g, ln_b, tc):
    n = h1.shape[0]
    row = lambda i, *_: (i, 0)
    c2 = lambda i, *_: (0, 0)
    smem_col = pl.BlockSpec((TOP_K, tc), lambda i, *_: (0, i), memory_space=pltpu.SMEM)
    per_tile = pl.BlockSpec((1, 1, _max_chunks(tc)), lambda i, *_: (i, 0, 0), memory_space=pltpu.SMEM)
    next_tile = pl.BlockSpec((1, 1, _max_chunks(tc)), lambda i, *_: (jnp.minimum(i + 1, n // tc - 1), 0, 0),
                             memory_space=pltpu.SMEM)
    return pl.pallas_call(
        functools.partial(_combine_kernel, tc=tc),
        out_shape=(jax.ShapeDtypeStruct((n, D_MODEL), jnp.float32),
                   jax.ShapeDtypeStruct((n, D_MODEL), jnp.bfloat16)),
        grid_spec=pltpu.PrefetchScalarGridSpec(
            num_scalar_prefetch=1,
            grid=(n // tc,),
            in_specs=[smem_col, smem_col, per_tile, per_tile, next_tile, next_tile,
                      pl.BlockSpec(memory_space=pl.ANY),
                      pl.BlockSpec((tc, D_MODEL), row),
                      pl.BlockSpec((tc, D_MODEL), row),
                      pl.BlockSpec((D_MODEL, D_SHARED), c2),
                      pl.BlockSpec((D_MODEL, D_SHARED), c2),
                      pl.BlockSpec((D_SHARED, D_MODEL), c2),
                      pl.BlockSpec((1, D_MODEL), c2),
                      pl.BlockSpec((1, D_MODEL), c2)],
            out_specs=(pl.BlockSpec((tc, D_MODEL), row),
                       pl.BlockSpec((tc, D_MODEL), row)),
            scratch_shapes=[pltpu.VMEM((2, _stage_rows(tc), PAIRS, LANES), jnp.int32),
                            pltpu.VMEM((tc * SLABS, LANES), jnp.float32),
                            pltpu.SemaphoreType.DMA((2,))]),
        compiler_params=_cparams(("arbitrary",)),
        name="moe_combine",
    )(n_chunks, loc, gates, chunk_stage, chunk_row, chunk_stage, chunk_row, y3, h1, h1b, wsg, wsu, wsd, ln_g, ln_b)


def _routing_plan(counts, n_blocks, cap):
    blocks_per_e = (counts + ROW_BLOCK - 1) // ROW_BLOCK
    blk_end = jnp.cumsum(blocks_per_e)
    blk_start = blk_end - blocks_per_e
    block_ids = jnp.arange(n_blocks, dtype=jnp.int32)
    block_expert = jnp.minimum(jnp.sum((blk_end[None, :] <= block_ids[:, None]).astype(jnp.int32), axis=1),
                               N_EXPERTS - 1)
    start_of = jnp.sum(jnp.where(block_expert[:, None] == jnp.arange(N_EXPERTS, dtype=jnp.int32)[None, :],
                                 blk_start[None, :], 0), axis=1)
    block_row = block_expert * (cap // ROW_BLOCK) + (block_ids - start_of)
    expert_row0 = jnp.arange(N_EXPERTS, dtype=jnp.int32) * cap
    pad_lo = expert_row0 + counts
    pad_hi = expert_row0 + blocks_per_e * ROW_BLOCK
    return block_expert, block_row.astype(jnp.int32), blk_end[-1:], pad_lo, pad_hi


def _prepare_weights(w_in, w_uk, w_uv, w_o):
    L = w_in.shape[0]
    o = IN_OFFS
    bf = jnp.bfloat16
    w_qa = w_in[:, :, o[0]:o[1]].reshape(L, D_MODEL, A_HEADS, A_HEAD_DIM).transpose(0, 2, 1, 3)
    w_qlat = _fold(w_qa, w_uk, A_HEAD_DIM ** -0.5)
    w_qlat = w_qlat.transpose(0, 2, 1, 3).reshape(L, D_MODEL, A_HEADS * KV_LATENT)
    w_oa = w_o[:, :A_HEADS * A_HEAD_DIM].reshape(L, A_HEADS, A_HEAD_DIM, D_MODEL)
    w_uvo = _fold(w_uv, w_oa, 1.0)
    w_ob = w_o[:, A_HEADS * A_HEAD_DIM:].astype(bf)

    def pad_cols(w, width):
        return jnp.pad(w, ((0, 0), (0, 0), (0, width - w.shape[-1])))

    half = R_QK_DIM // 2
    perm = np.concatenate([np.arange(half) + R_QK_DIM * h for h in range(R_HEADS)]
                          + [np.arange(half) + R_QK_DIM * h + half for h in range(R_HEADS)])
    w_cat = jnp.concatenate([
        w_qlat,
        w_in[:, :, o[1]:o[2]].astype(bf),
        w_in[:, :, o[2]:o[3]].astype(bf),
        pad_cols(w_in[:, :, o[3]:o[4]], LANES).astype(bf),
        pad_cols(w_in[:, :, o[4]:o[5]], LANES).astype(bf),
        w_in[:, :, o[5]:o[6]][:, :, perm].astype(bf),
        w_in[:, :, o[6]:o[7]][:, :, perm].astype(bf),
        w_in[:, :, o[7]:o[8]].astype(bf),
        w_in[:, :, o[8]:o[9]].astype(bf)], axis=-1)
    return w_cat, w_uvo, w_ob


def _rotary_tables(t_pad):
    half = R_QK_DIM // 2
    inv = ROPE_BASE ** (-jnp.arange(half, dtype=jnp.float32) / half)
    ang = jnp.arange(t_pad, dtype=jnp.float32)[:, None] * inv
    return jnp.tile(jnp.cos(ang), (1, R_HEADS)), jnp.tile(jnp.sin(ang), (1, R_HEADS))


def _pick_tile(n, prefer):
    for t in prefer:
        if n % t == 0:
            return t
    raise ValueError(f"no tile for {n}")


def kernel(x, meta_tokens, w_in, w_uk, w_uv, kv_norm_w, ret_norm_w, w_o, ln1_g, ln1_b, w_router, router_bias,
           w_gate, w_up, w_down, ws_gate, ws_up, ws_down, ln2_g, ln2_b):
    b, s, d = x.shape
    assert d == D_MODEL
    L = w_in.shape[0]
    t = s + N_META
    n_sel = min(TOPK_MAX, s // 4)
    t_pad = -(-t // LANES) * LANES
    n = b * t_pad
    bf = jnp.bfloat16

    meta = jnp.broadcast_to(meta_tokens.astype(x.dtype)[None], (b, N_META, d))
    h = jnp.concatenate([meta, x, jnp.zeros((b, t_pad - t, d), x.dtype)], axis=1).reshape(n, d)
    hb = h.astype(bf)

    w_cat, w_uvo, w_ob = _prepare_weights(w_in, w_uk, w_uv, w_o)
    cos_t, sin_t = _rotary_tables(t_pad)
    wsg, wsu, wsd = ws_gate.astype(bf), ws_up.astype(bf), ws_down.astype(bf)
    w_rt = jnp.swapaxes(w_router, 1, 2)

    tm = _pick_tile(n, (2 * MOE_TILE, MOE_TILE))
    n_tiles = n // MOE_TILE
    n_blocks = -(-(n * TOP_K) // ROW_BLOCK) + N_EXPERTS
    cap = (-(-n // ROW_BLOCK) + 1) * ROW_BLOCK
    n_rows = N_EXPERTS * cap

    for l in range(L):
        qlat, ckv, qi, ki, wi, qr, kr, vr, sg = _inproj(hb, w_cat[l], kv_norm_w[l][None], cos_t, sin_t, t_pad)
        olat = _attention(qlat, qi, wi, ckv, ki, b, t_pad, n_sel)
        ob = _retention(qr, kr, vr, sg, ret_norm_w[l][None], b, t_pad)
        h1, h1b, x3, gates, loc, counts, tstart, tcnt = _mix_router(
            olat, ob, h, w_uvo[l], w_ob[l], ln1_g[l][None], ln1_b[l][None], w_rt[l], router_bias[l][:, None], tm)
        n_chunks, chunk_stage, chunk_row = _chunk_tables(tstart[:, :n_tiles].T, tcnt[:, :n_tiles].T, cap,
                                                         _max_chunks(MOE_TILE))
        block_expert, block_row, n_used, pad_lo, pad_hi = _routing_plan(counts[:, 0], n_blocks, cap)
        x_rows = _dispatch(n_chunks, pad_lo, pad_hi, loc, chunk_stage, chunk_row, x3.reshape(n, PAIRS, LANES),
                           n_rows, MOE_TILE)
        y_rows = _experts(block_expert, block_row, n_used, x_rows.reshape(n_rows * PAIRS, LANES),
                          w_gate, w_up, w_down, l)
        h, hb = _combine(n_chunks, loc, gates, chunk_stage, chunk_row, y_rows.reshape(n_rows, PAIRS, LANES), h1, h1b,
                         wsg[l], wsu[l], wsd[l], ln2_g[l][None], ln2_b[l][None], MOE_TILE)
    return h.reshape(b, t_pad, d)[:, N_META:t]
```
